```python
import jax, jax.numpy as jnp
from jax import lax
import numpy as np

D_MODEL = 2048
BATCH = 8
SEQ = 2048
DEPTH = 1

RET_HEADS = 8
RET_HEAD_DIM = D_MODEL // 16
RET_WIDTH = RET_HEADS * RET_HEAD_DIM
CHUNK = 128
ROPE_THETA = 10000.0
LRU_WIDTH = D_MODEL // 2
LRU_BLOCKS = 8
LRU_BLOCK_DIM = LRU_WIDTH // LRU_BLOCKS
CONV_WIDTH = 4
LRU_C = 8.0
MIX_WIDTH = RET_WIDTH + LRU_WIDTH
IN_WIDTH = 4 * RET_WIDTH + 2 * LRU_WIDTH
N_EXPERTS = 64
N_GROUPS = 8
EXPERTS_PER_GROUP = N_EXPERTS // N_GROUPS
TOPK_GROUPS = 4
TOP_K = 8
EXPERT_HIDDEN = D_MODEL // 4
ROUTED_SCALE = 2.5
MOE_BLOCK = 256
EPS = 1e-6

kernel_name = "hymba_style_retention_rglru_moe_adaln"


def rmsnorm(x, g):
    x32 = x.astype(jnp.float32)
    y = x32 * lax.rsqrt(jnp.mean(x32 * x32, axis=-1, keepdims=True) + EPS)
    return (y * g.astype(jnp.float32)).astype(x.dtype)


def rope(t, positions):
    dh = t.shape[-1]
    inv_freq = 1.0 / (ROPE_THETA ** (jnp.arange(0, dh, 2, dtype=jnp.float32) / dh))
    ang = positions.astype(jnp.float32)[..., None] * inv_freq
    cos = jnp.cos(ang)[:, :, None, :]
    sin = jnp.sin(ang)[:, :, None, :]
    t1, t2 = jnp.split(t, 2, axis=-1)
    return jnp.concatenate([t1 * cos - t2 * sin, t2 * cos + t1 * sin], axis=-1)


def retention_chunkwise(q, k, v):
    b, s, h, dh = q.shape
    n = s // CHUNK
    lg = jnp.log1p(-jnp.exp2(-5.0 - jnp.arange(h, dtype=jnp.float32)))
    q = q.reshape(b, n, CHUNK, h, dh)
    k = k.reshape(b, n, CHUNK, h, dh)
    v = v.reshape(b, n, CHUNK, h, dh)
    idx = jnp.arange(CHUNK, dtype=jnp.float32)
    diff = idx[:, None] - idx[None, :]
    intra = jnp.where(diff >= 0, jnp.exp(lg[:, None, None] * jnp.maximum(diff, 0.0)), 0.0)
    scores = jnp.einsum('bnchd,bnmhd->bnhcm', q, k) * intra
    inner = jnp.einsum('bnhcm,bnmhe->bnche', scores, v)
    k_dec = jnp.exp(lg[None, :] * (CHUNK - 1.0 - idx)[:, None])
    kv = jnp.einsum('bnmhd,bnmhe->bnhde', k * k_dec[None, None, :, :, None], v)
    chunk_dec = jnp.exp(lg * CHUNK)[None, :, None, None]

    def step(state, kv_i):
        return chunk_dec * state + kv_i, state

    _, r_prev = lax.scan(step, jnp.zeros((b, h, dh, dh), jnp.float32), jnp.moveaxis(kv, 1, 0))
    r_prev = jnp.moveaxis(r_prev, 0, 1)
    q_dec = jnp.exp(lg[None, :] * (idx + 1.0)[:, None])
    cross = jnp.einsum('bnchd,bnhde->bnche', q * q_dec[None, None, :, :, None], r_prev)
    return (inner + cross).reshape(b, s, h, dh)


def causal_dwconv(x, w, bias):
    s = x.shape[1]
    xp = jnp.pad(x, ((0, 0), (CONV_WIDTH - 1, 0), (0, 0)))
    y = bias
    for tap in range(CONV_WIDTH):
        y = y + xp[:, tap:tap + s] * w[tap]
    return y


def rg_lru(xc, w_a, b_a, w_x, b_x, lam):
    b, s, wd = xc.shape
    xb = xc.reshape(b, s, LRU_BLOCKS, LRU_BLOCK_DIM)
    r = jax.nn.sigmoid(jnp.einsum('bsgi,gij->bsgj', xb, w_a).reshape(b, s, wd) + b_a)
    i = jax.nn.sigmoid(jnp.einsum('bsgi,gij->bsgj', xb, w_x).reshape(b, s, wd) + b_x)
    log_a = -LRU_C * r * jax.nn.softplus(-lam)
    a = jnp.exp(log_a)
    bt = jnp.sqrt(-jnp.expm1(2.0 * log_a)) * (i * xc)

    def combine(p, q):
        return p[0] * q[0], q[0] * p[1] + q[1]

    _, hs = lax.associative_scan(combine, (a, bt), axis=1)
    return hs


def mixer(h, positions, w_in, w_conv, b_conv, w_rg_a, b_rg_a, w_rg_x, b_rg_x, lam, beta_ret, beta_lru, w_out):
    b, s, _ = h.shape
    proj = h @ w_in
    rw, lw = RET_WIDTH, LRU_WIDTH
    q = proj[..., 0:rw].reshape(b, s, RET_HEADS, RET_HEAD_DIM)
    k = proj[..., rw:2 * rw].reshape(b, s, RET_HEADS, RET_HEAD_DIM)
    v = proj[..., 2 * rw:3 * rw].reshape(b, s, RET_HEADS, RET_HEAD_DIM)
    g = proj[..., 3 * rw:4 * rw]
    xr = proj[..., 4 * rw:4 * rw + lw]
    yr = proj[..., 4 * rw + lw:4 * rw + 2 * lw]
    q = rope(q.astype(jnp.float32), positions) * (RET_HEAD_DIM ** -0.5)
    k = rope(k.astype(jnp.float32), positions)
    o = retention_chunkwise(q, k, v.astype(jnp.float32))
    mu = jnp.mean(o, axis=-1, keepdims=True)
    oc = o - mu
    o = oc * lax.rsqrt(jnp.mean(oc * oc, axis=-1, keepdims=True) + EPS)
    ret = jax.nn.silu(g.astype(jnp.float32)) * o.reshape(b, s, rw)
    xc = causal_dwconv(xr.astype(jnp.float32), w_conv, b_conv)
    lru = rg_lru(xc, w_rg_a, b_rg_a, w_rg_x, b_rg_x, lam) * jax.nn.gelu(yr.astype(jnp.float32), approximate=True)
    ret = rmsnorm(ret, beta_ret).astype(h.dtype)
    lru = rmsnorm(lru, beta_lru).astype(h.dtype)
    return jnp.concatenate([ret, lru], axis=-1) @ w_out


def swiglu(t, w_io, w_down):
    gu = t @ w_io
    gate, up = jnp.split(gu, 2, axis=-1)
    return (jax.nn.silu(gate) * up) @ w_down


def moe(h, w_router, b_router, w_exp_in, w_exp_out, w_sh_in, w_sh_out):
    b, s, d = h.shape
    t = b * s
    hf = h.reshape(t, d)
    scores = jax.nn.sigmoid((hf @ w_router).astype(jnp.float32))
    biased = scores + b_router.astype(jnp.float32)
    grp = biased.reshape(t, N_GROUPS, EXPERTS_PER_GROUP)
    grp_score = lax.top_k(grp, 2)[0].sum(-1)
    _, grp_idx = lax.top_k(grp_score, TOPK_GROUPS)
    grp_mask = jax.nn.one_hot(grp_idx, N_GROUPS).sum(1) > 0
    exp_mask = jnp.repeat(grp_mask, EXPERTS_PER_GROUP, axis=1)
    masked = jnp.where(exp_mask, biased, -jnp.inf)
    _, eidx = lax.top_k(masked, TOP_K)
    wsel = jnp.take_along_axis(scores, eidx, axis=1)
    wsel = wsel / jnp.sum(wsel, axis=-1, keepdims=True) * ROUTED_SCALE
    a_n = t * TOP_K
    flat_e = eidx.reshape(a_n)
    flat_t = jnp.repeat(jnp.arange(t, dtype=jnp.int32), TOP_K)
    flat_w = wsel.reshape(a_n)
    order = jnp.argsort(flat_e, stable=True)
    se, st, sw = flat_e[order], flat_t[order], flat_w[order]
    counts = jnp.bincount(flat_e, length=N_EXPERTS)
    starts = jnp.cumsum(counts) - counts
    pcounts = (counts + MOE_BLOCK - 1) // MOE_BLOCK * MOE_BLOCK
    pends = jnp.cumsum(pcounts)
    pstarts = pends - pcounts
    dest = pstarts[se] + jnp.arange(a_n, dtype=jnp.int32) - starts[se]
    n_rows = (a_n + MOE_BLOCK - 1) // MOE_BLOCK * MOE_BLOCK + N_EXPERTS * MOE_BLOCK
    n_blocks = n_rows // MOE_BLOCK
    row_tok = jnp.full((n_rows,), t, jnp.int32).at[dest].set(st)
    row_w = jnp.zeros((n_rows,), jnp.float32).at[dest].set(sw)
    blk_e = jnp.minimum(jnp.searchsorted(pends, jnp.arange(n_blocks, dtype=jnp.int32) * MOE_BLOCK, side='right'), N_EXPERTS - 1)
    hpad = jnp.concatenate([hf, jnp.zeros((1, d), hf.dtype)], axis=0)

    def expert_block(args):
        tok, wt, e = args
        out = swiglu(hpad[tok], w_exp_in[e], w_exp_out[e])
        return out.astype(jnp.float32) * wt[:, None]

    yb = lax.map(expert_block, (row_tok.reshape(n_blocks, MOE_BLOCK), row_w.reshape(n_blocks, MOE_BLOCK), blk_e))
    routed = jnp.zeros((t + 1, d), jnp.float32).at[row_tok].add(yb.reshape(n_rows, d))[:t]
    shared = swiglu(hf, w_sh_in, w_sh_out).astype(jnp.float32)
    return (routed + shared).astype(h.dtype).reshape(b, s, d)


def setup_inputs(seed: int = 0) -> dict:
    key = jax.random.key(seed)
    ks = jax.random.split(key, 28)
    f32 = jnp.float32
    nrm = lambda k, shape, scale: jax.random.normal(k, shape, f32) * scale
    a0 = jax.random.uniform(ks[13], (DEPTH, LRU_WIDTH), f32, 0.9, 0.999)
    return {
        'x': nrm(ks[0], (BATCH, SEQ, D_MODEL), 1.0),
        'c': nrm(ks[1], (BATCH, D_MODEL), 1.0),
        'positions': jnp.arange(SEQ, dtype=jnp.int32)[None, :] + jax.random.randint(ks[2], (BATCH, 1), 0, 4096, jnp.int32),
        'w_ada': nrm(ks[3], (DEPTH, D_MODEL, 6 * D_MODEL), 0.5 * D_MODEL ** -0.5),
        'b_ada': nrm(ks[4], (DEPTH, 6 * D_MODEL), 0.02),
        'g_mix': 1.0 + nrm(ks[5], (DEPTH, D_MODEL), 0.02),
        'w_in': nrm(ks[6], (DEPTH, D_MODEL, IN_WIDTH), D_MODEL ** -0.5),
        'w_conv': nrm(ks[7], (DEPTH, CONV_WIDTH, LRU_WIDTH), CONV_WIDTH ** -0.5),
        'b_conv': nrm(ks[8], (DEPTH, LRU_WIDTH), 0.02),
        'w_rg_a': nrm(ks[9], (DEPTH, LRU_BLOCKS, LRU_BLOCK_DIM, LRU_BLOCK_DIM), LRU_BLOCK_DIM ** -0.5),
        'b_rg_a': nrm(ks[10], (DEPTH, LRU_WIDTH), 0.02),
        'w_rg_x': nrm(ks[11], (DEPTH, LRU_BLOCKS, LRU_BLOCK_DIM, LRU_BLOCK_DIM), LRU_BLOCK_DIM ** -0.5),
        'b_rg_x': nrm(ks[12], (DEPTH, LRU_WIDTH), 0.02),
        'lam': jnp.log(a0) - jnp.log1p(-a0),
        'beta_ret': 1.0 + nrm(ks[14], (DEPTH, RET_WIDTH), 0.02),
        'beta_lru': 1.0 + nrm(ks[15], (DEPTH, LRU_WIDTH), 0.02),
        'w_out': nrm(ks[16], (DEPTH, MIX_WIDTH, D_MODEL), MIX_WIDTH ** -0.5),
        'g_moe': 1.0 + nrm(ks[17], (DEPTH, D_MODEL), 0.02),
        'w_router': nrm(ks[18], (DEPTH, D_MODEL, N_EXPERTS), D_MODEL ** -0.5),
        'b_router': nrm(ks[19], (DEPTH, N_EXPERTS), 0.01),
        'w_exp_in': nrm(ks[20], (DEPTH, N_EXPERTS, D_MODEL, 2 * EXPERT_HIDDEN), D_MODEL ** -0.5),
        'w_exp_out': nrm(ks[21], (DEPTH, N_EXPERTS, EXPERT_HIDDEN, D_MODEL), EXPERT_HIDDEN ** -0.5),
        'w_sh_in': nrm(ks[22], (DEPTH, D_MODEL, 2 * EXPERT_HIDDEN), D_MODEL ** -0.5),
        'w_sh_out': nrm(ks[23], (DEPTH, EXPERT_HIDDEN, D_MODEL), EXPERT_HIDDEN ** -0.5),
        'g_final': 1.0 + nrm(ks[24], (D_MODEL,), 0.02),
    }


def reference(x, c, positions, w_ada, b_ada, g_mix, w_in, w_conv, b_conv, w_rg_a, b_rg_a, w_rg_x, b_rg_x, lam, beta_ret, beta_lru, w_out, g_moe, w_router, b_router, w_exp_in, w_exp_out, w_sh_in, w_sh_out, g_final):
    cs = jax.nn.silu(c)
    for l in range(DEPTH):
        mod = (cs @ w_ada[l] + b_ada[l]).reshape(c.shape[0], 6, 1, D_MODEL)
        sh1, sc1, gt1, sh2, sc2, gt2 = mod[:, 0], mod[:, 1], mod[:, 2], mod[:, 3], mod[:, 4], mod[:, 5]
        h = rmsnorm(x, g_mix[l]) * (1.0 + sc1) + sh1
        x = x + gt1 * mixer(h, positions, w_in[l], w_conv[l], b_conv[l], w_rg_a[l], b_rg_a[l], w_rg_x[l], b_rg_x[l], lam[l], beta_ret[l], beta_lru[l], w_out[l])
        h = rmsnorm(x, g_moe[l]) * (1.0 + sc2) + sh2
        x = x + gt2 * moe(h, w_router[l], b_router[l], w_exp_in[l], w_exp_out[l], w_sh_in[l], w_sh_out[l])
    return rmsnorm(x, g_final)
```

```python
import functools

import numpy as np
import jax
import jax.numpy as jnp
from jax import lax
from jax.experimental import pallas as pl
from jax.experimental.pallas import tpu as pltpu

F32 = jnp.float32
BF16 = jnp.bfloat16
I32 = jnp.int32
U32 = jnp.uint32

RET_HEADS = 8
HEAD_DIM = 128
CHUNK = 128
ROPE_THETA = 10000.0
LRU_BLOCKS = 8
CONV_WIDTH = 4
LRU_C = 8.0
N_EXPERTS = 64
N_GROUPS = 8
GROUP_SIZE = N_EXPERTS // N_GROUPS
TOPK_GROUPS = 4
TOP_K = 8
ROUTED_SCALE = 2.5
EPS = 1e-6

LANES = 128
SUBLANES = 8
VMEM_LIMIT_BYTES = 56 * 1024 * 1024

MOE_BLOCK = 256
NEG_INF = float("-inf")


def _params(semantics, vmem=VMEM_LIMIT_BYTES):
    return pltpu.CompilerParams(dimension_semantics=semantics, vmem_limit_bytes=vmem)


def _silu(x):
    return x * jax.nn.sigmoid(x)


def _rms(x):
    return x * lax.rsqrt(jnp.mean(x * x, axis=-1, keepdims=True) + EPS)


def _pack2(lo, hi):
    return pltpu.pack_elementwise([lo, hi], packed_dtype=jnp.bfloat16)


def _unpack2(w):
    lo = pltpu.unpack_elementwise(w, index=0, packed_dtype=jnp.bfloat16, unpacked_dtype=F32)
    hi = pltpu.unpack_elementwise(w, index=1, packed_dtype=jnp.bfloat16, unpacked_dtype=F32)
    return lo, hi


def _load_rows(ref):
    return jnp.concatenate([ref[:, j, :] for j in range(ref.shape[1])], axis=1)


def _store_rows(ref, v):
    for j in range(ref.shape[1]):
        ref[:, j, :] = v[:, j * LANES:(j + 1) * LANES]


def _zero_rows(ref):
    z = jnp.zeros((ref.shape[0], ref.shape[2]), F32)
    w = _pack2(z, z)
    for j in range(ref.shape[1]):
        ref[:, j, :] = w


def _mod_kernel(c_ref, w_ref, b_ref, o_ref):
    cs = _silu(c_ref[...])
    o_ref[...] = jnp.dot(cs, w_ref[...], preferred_element_type=F32,
                         precision=lax.Precision.HIGHEST) + b_ref[...]


def _mod_call(c, w_ada, b_ada):
    b, d = c.shape
    n = w_ada.shape[1]
    tn = 1024
    return pl.pallas_call(
        _mod_kernel,
        grid=(n // tn,),
        in_specs=[pl.BlockSpec((b, d), lambda j: (0, 0)),
                  pl.BlockSpec((d, tn), lambda j: (0, j)),
                  pl.BlockSpec((1, tn), lambda j: (0, j))],
        out_specs=pl.BlockSpec((b, tn), lambda j: (0, j)),
        out_shape=jax.ShapeDtypeStruct((b, n), F32),
        compiler_params=_params(("parallel",)),
        name="adaln_mod",
    )(c, w_ada, b_ada.reshape(1, n))


def _in_kernel(x_ref, mod_ref, g_ref, w_ref, o_ref, h_ref):
    @pl.when(pl.program_id(1) == 0)
    def _():
        y = _rms(x_ref[...]) * g_ref[...]
        h_ref[...] = (y * (1.0 + mod_ref[1:2, :]) + mod_ref[0:1, :]).astype(BF16)

    o_ref[...] = jnp.dot(h_ref[...], w_ref[...], preferred_element_type=F32).astype(o_ref.dtype)


def _in_call(x2, mod3, g_mix, w_in_bf, seq):
    t, d = x2.shape
    n_out = w_in_bf.shape[1] // 1024
    tm = min(1024, seq)
    per_b = seq // tm
    return pl.pallas_call(
        _in_kernel,
        grid=(t // tm, n_out),
        in_specs=[pl.BlockSpec((tm, d), lambda i, j: (i, 0)),
                  pl.BlockSpec((None, 6, d), lambda i, j: (i // per_b, 0, 0)),
                  pl.BlockSpec((1, d), lambda i, j: (0, 0)),
                  pl.BlockSpec((d, 1024), lambda i, j: (0, j))],
        out_specs=pl.BlockSpec((None, tm, 1024), lambda i, j: (j, i, 0)),
        out_shape=jax.ShapeDtypeStruct((n_out, t, 1024), BF16),
        scratch_shapes=[pltpu.VMEM((tm, d), BF16)],
        compiler_params=_params(("parallel", "arbitrary")),
        name="in_proj",
    )(x2, mod3, g_mix.reshape(1, d), w_in_bf)


def _rope_kernel(pos_ref, invf_ref, sign_ref, cos_ref, sin_ref):
    ang = pos_ref[...].astype(F32) * invf_ref[...]
    cos_ref[...] = jnp.cos(ang)
    sin_ref[...] = jnp.sin(ang) * sign_ref[...]


def _rope_call(positions):
    t = positions.size
    tp = min(2048, t)
    half = HEAD_DIM // 2
    inv_freq = 1.0 / (ROPE_THETA ** (jnp.arange(0, HEAD_DIM, 2, dtype=F32) / HEAD_DIM))
    invf = jnp.concatenate([inv_freq, inv_freq]).reshape(1, HEAD_DIM)
    sign = jnp.concatenate([-jnp.ones((half,), F32), jnp.ones((half,), F32)]).reshape(1, HEAD_DIM)
    return pl.pallas_call(
        _rope_kernel,
        grid=(t // tp,),
        in_specs=[pl.BlockSpec((tp, 1), lambda i: (i, 0)),
                  pl.BlockSpec((1, HEAD_DIM), lambda i: (0, 0)),
                  pl.BlockSpec((1, HEAD_DIM), lambda i: (0, 0))],
        out_specs=[pl.BlockSpec((tp, HEAD_DIM), lambda i: (i, 0)),
                   pl.BlockSpec((tp, HEAD_DIM), lambda i: (i, 0))],
        out_shape=[jax.ShapeDtypeStruct((t, HEAD_DIM), F32)] * 2,
        compiler_params=_params(("parallel",)),
        name="rope_tables",
    )(positions.reshape(t, 1), invf, sign)


def _ret_tables():
    h = np.arange(RET_HEADS, dtype=np.float64)
    lg = np.log1p(-np.exp2(-5.0 - h))
    idx = np.arange(CHUNK, dtype=np.float64)
    diff = idx[:, None] - idx[None, :]
    intra = np.where(diff >= 0, np.exp(lg[:, None, None] * np.maximum(diff, 0.0)), 0.0)
    kdec = np.exp(lg[:, None] * (CHUNK - 1.0 - idx)[None, :])
    qdec = np.exp(lg[:, None] * (idx + 1.0)[None, :])
    cdec = np.exp(lg * CHUNK)
    bc = lambda a: np.ascontiguousarray(np.broadcast_to(a[:, :, None], (RET_HEADS, CHUNK, HEAD_DIM)))
    return (intra.astype(np.float32), bc(qdec).astype(np.float32), bc(kdec).astype(np.float32),
            [float(v) for v in cdec])


def _ret_kernel(q_ref, k_ref, v_ref, g_ref, cos_ref, sin_ref, intra_ref, qdec_ref, kdec_ref,
                beta_ref, o_ref, st_ref, *, cdec):
    @pl.when(pl.program_id(1) == 0)
    def _():
        st_ref[...] = jnp.zeros_like(st_ref)

    cos = cos_ref[...]
    sin = sin_ref[...]
    scale = HEAD_DIM ** -0.5
    nt = (((1,), (1,)), ((), ()))
    tn = (((0,), (0,)), ((), ()))
    outs = []
    for h in range(RET_HEADS):
        sl = slice(h * HEAD_DIM, (h + 1) * HEAD_DIM)
        q = q_ref[:, sl].astype(F32)
        k = k_ref[:, sl].astype(F32)
        v = v_ref[:, sl]
        qr = (q * cos + pltpu.roll(q, HEAD_DIM // 2, 1) * sin) * scale
        kr = k * cos + pltpu.roll(k, HEAD_DIM // 2, 1) * sin
        s = lax.dot_general(qr.astype(BF16), kr.astype(BF16), nt,
                            preferred_element_type=F32) * intra_ref[h]
        inner = jnp.dot(s.astype(BF16), v, preferred_element_type=F32)
        st = st_ref[h]
        cross = jnp.dot((qr * qdec_ref[h]).astype(BF16), st.astype(BF16),
                        preferred_element_type=F32)
        o = inner + cross
        kv = lax.dot_general((kr * kdec_ref[h]).astype(BF16), v, tn, preferred_element_type=F32)
        st_ref[h] = cdec[h] * st + kv
        oc = o - jnp.mean(o, axis=-1, keepdims=True)
        on = oc * lax.rsqrt(jnp.mean(oc * oc, axis=-1, keepdims=True) + EPS)
        outs.append(_silu(g_ref[:, sl].astype(F32)) * on)
    ret = jnp.concatenate(outs, axis=1)
    o_ref[...] = (_rms(ret) * beta_ref[...]).astype(o_ref.dtype)


def _ret_call(proj, cos, sin, beta_ret, batch, seq):
    t = batch * seq
    rw = RET_HEADS * HEAD_DIM
    nc = seq // CHUNK
    intra, qdec, kdec, cdec = _ret_tables()
    row = lambda which: pl.BlockSpec((None, CHUNK, rw), lambda b, n: (which, b * nc + n, 0))
    tab = lambda: pl.BlockSpec((RET_HEADS, CHUNK, HEAD_DIM), lambda b, n: (0, 0, 0))
    cs = lambda: pl.BlockSpec((CHUNK, HEAD_DIM), lambda b, n: (b * nc + n, 0))
    return pl.pallas_call(
        functools.partial(_ret_kernel, cdec=cdec),
        grid=(batch, nc),
        in_specs=[row(0), row(1), row(2), row(3), cs(), cs(), tab(), tab(), tab(),
                  pl.BlockSpec((1, rw), lambda b, n: (0, 0))],
        out_specs=pl.BlockSpec((CHUNK, rw), lambda b, n: (b * nc + n, 0)),
        out_shape=jax.ShapeDtypeStruct((t, rw), BF16),
        scratch_shapes=[pltpu.VMEM((RET_HEADS, HEAD_DIM, HEAD_DIM), F32)],
        compiler_params=_params(("parallel", "arbitrary")),
        name="retention",
    )(proj, proj, proj, proj, cos, sin, jnp.asarray(intra), jnp.asarray(qdec), jnp.asarray(kdec),
      beta_ret.reshape(1, rw))


def _lru_kernel(xr_ref, yr_ref, wc_ref, bc_ref, wg_ref, ba_ref, bx_ref, lam_ref, beta_ref,
                o_ref, xbuf_ref, h_ref, *, ts):
    @pl.when(pl.program_id(1) == 0)
    def _():
        xbuf_ref[0:SUBLANES, :] = jnp.zeros((SUBLANES, xbuf_ref.shape[1]), F32)
        h_ref[...] = jnp.zeros_like(h_ref)

    x = xr_ref[...].astype(F32)
    w = x.shape[1]
    xbuf_ref[SUBLANES:SUBLANES + ts, :] = x
    xc = bc_ref[...]
    for tap in range(CONV_WIDTH):
        back = CONV_WIDTH - 1 - tap
        xs = x if back == 0 else xbuf_ref[SUBLANES - back:SUBLANES - back + ts, :]
        xc = xc + xs * wc_ref[tap:tap + 1, :]
    xbuf_ref[0:SUBLANES, :] = x[ts - SUBLANES:ts, :]

    xcb = xc.astype(BF16)
    bd = w // LRU_BLOCKS
    rs, gs = [], []
    for g in range(LRU_BLOCKS):
        z = jnp.dot(xcb[:, g * bd:(g + 1) * bd], wg_ref[g], preferred_element_type=F32)
        rs.append(z[:, :bd])
        gs.append(z[:, bd:])
    r = jax.nn.sigmoid(jnp.concatenate(rs, axis=1) + ba_ref[...])
    i = jax.nn.sigmoid(jnp.concatenate(gs, axis=1) + bx_ref[...])
    log_a = -LRU_C * r * jax.nn.softplus(-lam_ref[...])
    a = jnp.exp(log_a)
    bt = jnp.sqrt(-jnp.tanh(log_a) * (a * a + 1.0)) * (i * xc)

    rows = lax.broadcasted_iota(I32, (ts, w), 0)
    acc_a, acc_b = a, bt
    sh = 1
    while sh < ts:
        prev_a = pltpu.roll(acc_a, sh, 0)
        prev_b = pltpu.roll(acc_b, sh, 0)
        m = rows >= sh
        acc_b = jnp.where(m, acc_a * prev_b + acc_b, acc_b)
        acc_a = jnp.where(m, acc_a * prev_a, acc_a)
        sh *= 2
    hs = acc_a * h_ref[...] + acc_b
    h_ref[...] = hs[ts - 1:ts, :]

    lru = hs * jax.nn.gelu(yr_ref[...].astype(F32), approximate=True)
    o_ref[...] = (_rms(lru) * beta_ref[...]).astype(o_ref.dtype)


def _lru_call(proj, w_conv, b_conv, w_rg_a, b_rg_a, w_rg_x, b_rg_x, lam, beta_lru, batch, seq):
    t = batch * seq
    w = w_conv.shape[1]
    ts = min(256, seq)
    nt = seq // ts
    wg = jnp.concatenate([w_rg_a, w_rg_x], axis=-1).astype(BF16)
    vec = lambda: pl.BlockSpec((1, w), lambda b, j: (0, 0))
    row = lambda which: pl.BlockSpec((None, ts, w), lambda b, j: (which, b * nt + j, 0))
    return pl.pallas_call(
        functools.partial(_lru_kernel, ts=ts),
        grid=(batch, nt),
        in_specs=[row(4), row(5),
                  pl.BlockSpec((CONV_WIDTH, w), lambda b, j: (0, 0)), vec(),
                  pl.BlockSpec(wg.shape, lambda b, j: (0, 0, 0)),
                  vec(), vec(), vec(), vec()],
        out_specs=pl.BlockSpec((ts, w), lambda b, j: (b * nt + j, 0)),
        out_shape=jax.ShapeDtypeStruct((t, w), BF16),
        scratch_shapes=[pltpu.VMEM((ts + SUBLANES, w), F32), pltpu.VMEM((1, w), F32)],
        compiler_params=_params(("parallel", "arbitrary")),
        name="rg_lru",
    )(proj, proj, w_conv, b_conv.reshape(1, w), wg, b_rg_a.reshape(1, w), b_rg_x.reshape(1, w),
      lam.reshape(1, w), beta_lru.reshape(1, w))


def _out_kernel(ret_ref, lru_ref, x_ref, mod_ref, g_ref, w_ref, x1_ref, h2_ref):
    rw = ret_ref.shape[1]
    mix = jnp.dot(ret_ref[...], w_ref[0:rw, :], preferred_element_type=F32)
    mix = mix + jnp.dot(lru_ref[...], w_ref[rw:, :], preferred_element_type=F32)
    x1 = x_ref[...] + mod_ref[2:3, :] * mix
    x1_ref[...] = x1
    h2 = _rms(x1) * g_ref[...] * (1.0 + mod_ref[4:5, :]) + mod_ref[3:4, :]
    half = h2.shape[1] // 2
    _store_rows(h2_ref, _pack2(h2[:, :half], h2[:, half:]))


def _out_call(ret, lru, x2, mod3, g_moe, w_out_bf, seq):
    t, d = x2.shape
    rw, lw = ret.shape[1], lru.shape[1]
    tm = min(512, seq)
    per_b = seq // tm
    return pl.pallas_call(
        _out_kernel,
        grid=(t // tm,),
        in_specs=[pl.BlockSpec((tm, rw), lambda i: (i, 0)),
                  pl.BlockSpec((tm, lw), lambda i: (i, 0)),
                  pl.BlockSpec((tm, d), lambda i: (i, 0)),
                  pl.BlockSpec((None, 6, d), lambda i: (i // per_b, 0, 0)),
                  pl.BlockSpec((1, d), lambda i: (0, 0)),
                  pl.BlockSpec((rw + lw, d), lambda i: (0, 0))],
        out_specs=[pl.BlockSpec((tm, d), lambda i: (i, 0)),
                   pl.BlockSpec((tm, d // 2 // LANES, LANES), lambda i: (i, 0, 0))],
        out_shape=[jax.ShapeDtypeStruct((t, d), F32),
                   jax.ShapeDtypeStruct((t, d // 2 // LANES, LANES), U32)],
        compiler_params=_params(("parallel",)),
        name="out_proj",
    )(ret, lru, x2, mod3, g_moe.reshape(1, d), w_out_bf)


def _route_kernel(h_ref, wlo_ref, whi_ref, bias_ref, tri_ref, eidx_ref, wsel_ref, pos_ref,
                  cnt_ref, carry_ref):
    @pl.when(pl.program_id(0) == 0)
    def _():
        carry_ref[...] = jnp.zeros_like(carry_ref)

    tr = h_ref.shape[0]
    lo, hi = _unpack2(_load_rows(h_ref))
    nt = (((1,), (1,)), ((), ()))
    logits = lax.dot_general(wlo_ref[...], lo.astype(BF16), nt, preferred_element_type=F32)
    logits = logits + lax.dot_general(whi_ref[...], hi.astype(BF16), nt,
                                      preferred_element_type=F32)
    scores = jax.nn.sigmoid(logits)
    biased = scores + bias_ref[:, 0:1]
    shape3 = (N_GROUPS, GROUP_SIZE, tr)
    s3 = scores.reshape(shape3)
    b3 = biased.reshape(shape3)
    member = lax.broadcasted_iota(I32, shape3, 1)
    group = lax.broadcasted_iota(I32, shape3, 0)
    expert = group * GROUP_SIZE + member

    m1 = jnp.max(b3, axis=1, keepdims=True)
    i1 = jnp.min(jnp.where(b3 == m1, member, GROUP_SIZE), axis=1, keepdims=True)
    m2 = jnp.max(jnp.where(member == i1, NEG_INF, b3), axis=1, keepdims=True)
    gscore = m1 + m2

    gid = lax.broadcasted_iota(I32, (N_GROUPS, 1, tr), 0)
    rank = jnp.zeros((N_GROUPS, 1, tr), I32)
    for j in range(N_GROUPS):
        gj = gscore[j:j + 1]
        ahead = (gj > gscore) | ((gj == gscore) & (gid > j))
        rank = rank + ahead.astype(I32)
    masked = jnp.where(rank < TOPK_GROUPS, b3, NEG_INF)

    idxs, vals = [], []
    picked = jnp.zeros(shape3, F32)
    for _ in range(TOP_K):
        m = jnp.max(jnp.max(masked, axis=0, keepdims=True), axis=1, keepdims=True)
        cand = jnp.where(masked == m, expert, N_EXPERTS)
        idx = jnp.min(jnp.min(cand, axis=0, keepdims=True), axis=1, keepdims=True)
        hit = expert == idx
        val = jnp.sum(jnp.sum(jnp.where(hit, s3, 0.0), axis=0, keepdims=True), axis=1, keepdims=True)
        masked = jnp.where(hit, NEG_INF, masked)
        picked = jnp.where(hit, 1.0, picked)
        idxs.append(idx)
        vals.append(val)
    total = vals[0]
    for v in vals[1:]:
        total = total + v

    before = jnp.dot(picked.reshape(N_EXPERTS, tr).astype(BF16), tri_ref[...],
                     preferred_element_type=F32)
    carry = carry_ref[...]
    posm = (before + carry[:, 0:1]).reshape(shape3)
    carry = carry + jnp.sum(picked.reshape(N_EXPERTS, tr), axis=1, keepdims=True)
    carry_ref[...] = carry
    cnt_ref[...] = carry

    for k in range(TOP_K):
        hit = expert == idxs[k]
        p = jnp.sum(jnp.sum(jnp.where(hit, posm, 0.0), axis=0, keepdims=True), axis=1, keepdims=True)
        eidx_ref[k:k + 1, :] = idxs[k].reshape(1, tr)
        wsel_ref[k:k + 1, :] = (vals[k] / total * ROUTED_SCALE).reshape(1, tr)
        pos_ref[k:k + 1, :] = p.reshape(1, tr).astype(I32)


def _route_call(h2p, w_router, b_router):
    t, sub, lanes = h2p.shape
    half = sub * lanes
    tr = min(512, t)
    wt = w_router.T.astype(BF16)
    bias = jnp.broadcast_to(b_router.astype(F32)[:, None], (N_EXPERTS, LANES))
    tri = jnp.asarray(np.triu(np.ones((tr, tr), np.float32), k=1), dtype=BF16)
    kt = lambda dt: jax.ShapeDtypeStruct((TOP_K, t), dt)
    krow = lambda: pl.BlockSpec((TOP_K, tr), lambda i: (0, i))
    return pl.pallas_call(
        _route_kernel,
        grid=(t // tr,),
        in_specs=[pl.BlockSpec((tr, sub, lanes), lambda i: (i, 0, 0)),
                  pl.BlockSpec((N_EXPERTS, half), lambda i: (0, 0)),
                  pl.BlockSpec((N_EXPERTS, half), lambda i: (0, 0)),
                  pl.BlockSpec((N_EXPERTS, LANES), lambda i: (0, 0)),
                  pl.BlockSpec((tr, tr), lambda i: (0, 0))],
        out_specs=[krow(), krow(), krow(),
                   pl.BlockSpec((N_EXPERTS, LANES), lambda i: (0, 0))],
        out_shape=[kt(I32), kt(F32), kt(I32), jax.ShapeDtypeStruct((N_EXPERTS, LANES), F32)],
        scratch_shapes=[pltpu.VMEM((N_EXPERTS, LANES), F32)],
        compiler_params=_params(("arbitrary",)),
        name="router",
    )(h2p, wt[:, :half], wt[:, half:], bias, tri)


def _dest_kernel(pst_ref, eidx_ref, pos_ref, o_ref):
    e = eidx_ref[...]
    acc = pos_ref[...]
    for j in range(N_EXPERTS):
        acc = acc + jnp.where(e == j, pst_ref[j], 0)
    o_ref[...] = acc


def _dest_call(pstarts, eidx, pos):
    k, t = eidx.shape
    tb = min(2048, t)
    blk = lambda: pl.BlockSpec((k, tb), lambda i, pst: (0, i))
    return pl.pallas_call(
        _dest_kernel,
        grid_spec=pltpu.PrefetchScalarGridSpec(
            num_scalar_prefetch=1, grid=(t // tb,),
            in_specs=[blk(), blk()], out_specs=blk()),
        out_shape=jax.ShapeDtypeStruct((k, t), I32),
        compiler_params=_params(("parallel",)),
        name="dest_rows",
    )(pstarts, eidx, pos)


def _swiglu_packed(x_ref, w1_ref, w2_ref):
    xw = _load_rows(x_ref)
    half = xw.shape[1]
    lo, hi = _unpack2(xw)
    gu = jnp.dot(lo.astype(BF16), w1_ref[0:half, :], preferred_element_type=F32)
    gu = gu + jnp.dot(hi.astype(BF16), w1_ref[half:, :], preferred_element_type=F32)
    hid = gu.shape[1] // 2
    act = _silu(gu[:, :hid]) * gu[:, hid:]
    return jnp.dot(act.astype(BF16), w2_ref[...], preferred_element_type=F32)


def _dispatch_kernel(meta_ref, dest_ref, h_ref, w1_ref, w2_ref, xs_ref, sh_ref, zero_ref,
                     sem_z, sem_r, *, td):

    @pl.when(pl.program_id(0) == 0)
    def _():
        _zero_rows(zero_ref)

        def tail_copy(e):
            start = meta_ref[0, e] + meta_ref[2, e] - MOE_BLOCK
            return pltpu.make_async_copy(zero_ref, xs_ref.at[pl.ds(start, MOE_BLOCK)], sem_z)


        def unused_copy(j):
            start = (meta_ref[3, 0] + j) * MOE_BLOCK
            return pltpu.make_async_copy(zero_ref, xs_ref.at[pl.ds(start, MOE_BLOCK)], sem_z)

        n_blocks = xs_ref.shape[0] // MOE_BLOCK
        for e in range(N_EXPERTS):
            @pl.when(meta_ref[2, e] > meta_ref[1, e])
            def _():
                tail_copy(e).start()

            @pl.when(meta_ref[3, 0] + e < n_blocks)
            def _():
                unused_copy(e).start()
        for e in range(N_EXPERTS):
            @pl.when(meta_ref[2, e] > meta_ref[1, e])
            def _():
                tail_copy(e).wait()

            @pl.when(meta_ref[3, 0] + e < n_blocks)
            def _():
                unused_copy(e).wait()

    def issue(r, carry):
        for k in range(TOP_K):
            p = dest_ref[k, r]
            pltpu.make_async_copy(h_ref.at[r], xs_ref.at[p], sem_r).start()
        return carry

    lax.fori_loop(0, td, issue, 0)

    sh_ref[...] = _swiglu_packed(h_ref, w1_ref, w2_ref).astype(sh_ref.dtype)

    for k in range(TOP_K):
        pltpu.make_async_copy(h_ref, xs_ref.at[pl.ds(0, td)], sem_r).wait()


def _dispatch_call(meta, dest, h2p, w_sh_in_bf, w_sh_out_bf, n_rows):
    t, sub, lanes = h2p.shape
    d = 2 * sub * lanes
    td = min(256, t)
    return pl.pallas_call(
        functools.partial(_dispatch_kernel, td=td),
        grid=(t // td,),
        in_specs=[pl.BlockSpec(memory_space=pltpu.SMEM),
                  pl.BlockSpec((TOP_K, td), lambda i: (0, i), memory_space=pltpu.SMEM),
                  pl.BlockSpec((td, sub, lanes), lambda i: (i, 0, 0)),
                  pl.BlockSpec(w_sh_in_bf.shape, lambda i: (0, 0)),
                  pl.BlockSpec(w_sh_out_bf.shape, lambda i: (0, 0))],
        out_specs=[pl.BlockSpec(memory_space=pl.ANY),
                   pl.BlockSpec((td, d), lambda i: (i, 0))],
        out_shape=[jax.ShapeDtypeStruct((n_rows, sub, lanes), U32),
                   jax.ShapeDtypeStruct((t, d), BF16)],
        scratch_shapes=[pltpu.VMEM((MOE_BLOCK, sub, lanes), U32),
                        pltpu.SemaphoreType.DMA, pltpu.SemaphoreType.DMA],
        compiler_params=_params(("arbitrary",)),
        name="dispatch_shared",
    )(meta, dest, h2p, w_sh_in_bf, w_sh_out_bf)


def _expert_kernel(be_ref, nu_ref, xs_ref, w1_ref, w2_ref, ys_ref, w1b_ref, w2b_ref):
    i = pl.program_id(0)
    prev = be_ref[jnp.maximum(i - 1, 0)]
    fresh = jnp.logical_or(i == 0, be_ref[i] != prev)

    @pl.when(fresh)
    def _():
        w1b_ref[...] = w1_ref[...].astype(BF16)
        w2b_ref[...] = w2_ref[...].astype(BF16)

    @pl.when(i < nu_ref[0])
    def _():
        out = _swiglu_packed(xs_ref, w1b_ref, w2b_ref)
        half = out.shape[1] // 2
        _store_rows(ys_ref, _pack2(out[:, :half], out[:, half:]))

    @pl.when(i >= nu_ref[0])
    def _():
        _zero_rows(ys_ref)


def _expert_call(blk_e, n_used, xs, w_exp_in, w_exp_out):
    n_rows, sub, lanes = xs.shape
    _, d, h2 = w_exp_in.shape
    hid = w_exp_out.shape[1]
    n_blocks = n_rows // MOE_BLOCK
    rows_in = pl.BlockSpec((MOE_BLOCK, sub, lanes),
                           lambda i, be, nu: (jnp.minimum(i, nu[0] - 1), 0, 0))
    rows_out = pl.BlockSpec((MOE_BLOCK, sub, lanes), lambda i, be, nu: (i, 0, 0))
    return pl.pallas_call(
        _expert_kernel,
        grid_spec=pltpu.PrefetchScalarGridSpec(
            num_scalar_prefetch=2, grid=(n_blocks,),
            in_specs=[rows_in,
                      pl.BlockSpec((None, d, h2), lambda i, be, nu: (be[i], 0, 0)),
                      pl.BlockSpec((None, hid, d), lambda i, be, nu: (be[i], 0, 0))],
            out_specs=rows_out,
            scratch_shapes=[pltpu.VMEM((d, h2), BF16), pltpu.VMEM((hid, d), BF16)]),
        out_shape=jax.ShapeDtypeStruct((n_rows, sub, lanes), U32),
        compiler_params=_params(("arbitrary",)),
        name="experts",
    )(blk_e, n_used, xs, w_exp_in, w_exp_out)


def _combine_kernel(dest_ref, w_ref, ys_ref, sh_ref, x1_ref, mod_ref, g_ref, o_ref, buf_ref, sem,
                    *, tc, final_norm):
    def issue(r, carry):
        for k in range(TOP_K):
            p = dest_ref[k, r]
            pltpu.make_async_copy(ys_ref.at[p], buf_ref.at[k, r], sem).start()
        return carry

    lax.fori_loop(0, tc, issue, 0)
    for k in range(TOP_K):
        pltpu.make_async_copy(ys_ref.at[pl.ds(0, tc)], buf_ref.at[k], sem).wait()

    acc_lo = acc_hi = None
    for k in range(TOP_K):
        wk = w_ref[:, k:k + 1]
        lo, hi = _unpack2(_load_rows(buf_ref.at[k]))
        acc_lo = wk * lo if acc_lo is None else acc_lo + wk * lo
        acc_hi = wk * hi if acc_hi is None else acc_hi + wk * hi
    routed = jnp.concatenate([acc_lo, acc_hi], axis=1)
    moe = routed + sh_ref[...].astype(F32)
    x2 = x1_ref[...] + mod_ref[5:6, :] * moe
    o_ref[...] = _rms(x2) * g_ref[...] if final_norm else x2


def _combine_call(dest, wsel_t, ys, shared, x1, mod3, g_final, seq, final_norm):
    t, d = x1.shape
    half = d // 2
    tc = min(256, seq)
    per_b = seq // tc
    return pl.pallas_call(
        functools.partial(_combine_kernel, tc=tc, final_norm=final_norm),
        grid=(t // tc,),
        in_specs=[pl.BlockSpec((TOP_K, tc), lambda i: (0, i), memory_space=pltpu.SMEM),
                  pl.BlockSpec((tc, TOP_K), lambda i: (i, 0)),
                  pl.BlockSpec(memory_space=pl.ANY),
                  pl.BlockSpec((tc, d), lambda i: (i, 0)),
                  pl.BlockSpec((tc, d), lambda i: (i, 0)),
                  pl.BlockSpec((None, 6, d), lambda i: (i // per_b, 0, 0)),
                  pl.BlockSpec((1, d), lambda i: (0, 0))],
        out_specs=pl.BlockSpec((tc, d), lambda i: (i, 0)),
        out_shape=jax.ShapeDtypeStruct((t, d), F32),
        scratch_shapes=[pltpu.VMEM((TOP_K, tc, half // LANES, LANES), U32),
                        pltpu.SemaphoreType.DMA],
        compiler_params=_params(("arbitrary",)),
        name="combine_final",
    )(dest, wsel_t, ys, shared, x1, mod3, g_final.reshape(1, d))


def _moe_plan(counts):
    pcounts = (counts + MOE_BLOCK - 1) // MOE_BLOCK * MOE_BLOCK
    pends = jnp.cumsum(pcounts)
    pstarts = pends - pcounts
    return pstarts.astype(I32), pcounts.astype(I32), pends.astype(I32)


def kernel(x, c, positions, w_ada, b_ada, g_mix, w_in, w_conv, b_conv, w_rg_a, b_rg_a, w_rg_x, b_rg_x, lam, beta_ret, beta_lru, w_out, g_moe, w_router, b_router, w_exp_in, w_exp_out, w_sh_in, w_sh_out, g_final):
    batch, seq, d = x.shape
    t = batch * seq
    depth = w_ada.shape[0]
    xcur = x.reshape(t, d)
    cos, sin = _rope_call(positions)
    n_rows = (t * TOP_K + MOE_BLOCK - 1) // MOE_BLOCK * MOE_BLOCK + N_EXPERTS * MOE_BLOCK
    n_blocks = n_rows // MOE_BLOCK
    for l in range(depth):
        mod3 = _mod_call(c, w_ada[l], b_ada[l]).reshape(batch, 6, d)
        proj = _in_call(xcur, mod3, g_mix[l], w_in[l].astype(BF16), seq)
        ret = _ret_call(proj, cos, sin, beta_ret[l], batch, seq)
        lru = _lru_call(proj, w_conv[l], b_conv[l], w_rg_a[l], b_rg_a[l], w_rg_x[l], b_rg_x[l],
                        lam[l], beta_lru[l], batch, seq)
        x1, h2p = _out_call(ret, lru, xcur, mod3, g_moe[l], w_out[l].astype(BF16), seq)

        eidx, wsel, pos, cnt = _route_call(h2p, w_router[l], b_router[l])
        counts = cnt[:, 0].astype(I32)
        pstarts, pcounts, pends = _moe_plan(counts)
        dest = _dest_call(pstarts, eidx, pos)
        n_used = (pends[-1] // MOE_BLOCK).astype(I32)
        blk = jnp.minimum(jnp.arange(n_blocks, dtype=I32), n_used - 1) * MOE_BLOCK
        blk_e = jnp.sum((pends[None, :] <= blk[:, None]).astype(I32), axis=1)
        blk_e = jnp.minimum(blk_e, N_EXPERTS - 1)
        meta = jnp.stack([pstarts, counts, pcounts, jnp.full_like(counts, n_used)])

        xs, shared = _dispatch_call(meta, dest, h2p, w_sh_in[l].astype(BF16),
                                    w_sh_out[l].astype(BF16), n_rows)
        ys = _expert_call(blk_e, n_used.reshape(1), xs, w_exp_in[l], w_exp_out[l])
        xcur = _combine_call(dest, wsel.T, ys, shared, x1, mod3, g_final, seq,
                             final_norm=(l == depth - 1))
    return xcur.reshape(batch, seq, d)
```

```python
import functools

import numpy as np
import jax
import jax.numpy as jnp
from jax import lax
from jax.experimental import pallas as pl
from jax.experimental.pallas import tpu as pltpu

F32 = jnp.float32
BF16 = jnp.bfloat16
I32 = jnp.int32
U32 = jnp.uint32

RET_HEADS = 8
HEAD_DIM = 128
CHUNK = 128
ROPE_THETA = 10000.0
LRU_BLOCKS = 8
CONV_WIDTH = 4
LRU_C = 8.0
N_EXPERTS = 64
N_GROUPS = 8
GROUP_SIZE = N_EXPERTS // N_GROUPS
TOPK_GROUPS = 4
TOP_K = 8
ROUTED_SCALE = 2.5
EPS = 1e-6

LANES = 128
SUBLANES = 8
VMEM_LIMIT_BYTES = 56 * 1024 * 1024

MOE_BLOCK = 256
NEG_INF = float("-inf")


def _params(semantics, vmem=VMEM_LIMIT_BYTES):
    return pltpu.CompilerParams(dimension_semantics=semantics, vmem_limit_bytes=vmem)


def _silu(x):
    return x * jax.nn.sigmoid(x)


def _rms(x):
    return x * lax.rsqrt(jnp.mean(x * x, axis=-1, keepdims=True) + EPS)


def _pack2(lo, hi):
    return pltpu.pack_elementwise([lo, hi], packed_dtype=jnp.bfloat16)


def _unpack2(w):
    lo = pltpu.unpack_elementwise(w, index=0, packed_dtype=jnp.bfloat16, unpacked_dtype=F32)
    hi = pltpu.unpack_elementwise(w, index=1, packed_dtype=jnp.bfloat16, unpacked_dtype=F32)
    return lo, hi


def _load_rows(ref):
    rows = ref.shape[0] // SUBLANES
    return jnp.concatenate([ref[pl.ds(j, rows, stride=SUBLANES), :] for j in range(SUBLANES)],
                           axis=1)


def _store_rows(ref, v):
    rows = ref.shape[0] // SUBLANES
    for j in range(SUBLANES):
        ref[pl.ds(j, rows, stride=SUBLANES), :] = v[:, j * LANES:(j + 1) * LANES]


def _zero_rows(ref):
    rows = ref.shape[0] // SUBLANES
    z = jnp.zeros((rows, LANES), F32)
    w = _pack2(z, z)
    for j in range(SUBLANES):
        ref[pl.ds(j, rows, stride=SUBLANES), :] = w


def _token(ref, r):
    return ref.at[pl.ds(pl.multiple_of(r * SUBLANES, SUBLANES), SUBLANES)]


def _mod_kernel(c_ref, w_ref, b_ref, o_ref):
    cs = _silu(c_ref[...])
    o_ref[...] = jnp.dot(cs, w_ref[...], preferred_element_type=F32,
                         precision=lax.Precision.HIGHEST) + b_ref[...]


def _mod_call(c, w_ada, b_ada):
    b, d = c.shape
    n = w_ada.shape[1]
    tn = 1024
    return pl.pallas_call(
        _mod_kernel,
        grid=(n // tn,),
        in_specs=[pl.BlockSpec((b, d), lambda j: (0, 0)),
                  pl.BlockSpec((d, tn), lambda j: (0, j)),
                  pl.BlockSpec((1, tn), lambda j: (0, j))],
        out_specs=pl.BlockSpec((b, tn), lambda j: (0, j)),
        out_shape=jax.ShapeDtypeStruct((b, n), F32),
        compiler_params=_params(("parallel",)),
        name="adaln_mod",
    )(c, w_ada, b_ada.reshape(1, n))


def _in_kernel(x_ref, mod_ref, g_ref, w_ref, o_ref, h_ref):
    @pl.when(pl.program_id(1) == 0)
    def _():
        y = _rms(x_ref[...]) * g_ref[...]
        h_ref[...] = (y * (1.0 + mod_ref[1:2, :]) + mod_ref[0:1, :]).astype(BF16)

    o_ref[...] = jnp.dot(h_ref[...], w_ref[...], preferred_element_type=F32).astype(o_ref.dtype)


def _in_call(x2, mod3, g_mix, w_in_bf, seq):
    t, d = x2.shape
    n_out = w_in_bf.shape[1] // 1024
    tm = min(1024, seq)
    per_b = seq // tm
    return pl.pallas_call(
        _in_kernel,
        grid=(t // tm, n_out),
        in_specs=[pl.BlockSpec((tm, d), lambda i, j: (i, 0)),
                  pl.BlockSpec((None, 6, d), lambda i, j: (i // per_b, 0, 0)),
                  pl.BlockSpec((1, d), lambda i, j: (0, 0)),
                  pl.BlockSpec((d, 1024), lambda i, j: (0, j))],
        out_specs=pl.BlockSpec((None, tm, 1024), lambda i, j: (j, i, 0)),
        out_shape=jax.ShapeDtypeStruct((n_out, t, 1024), BF16),
        scratch_shapes=[pltpu.VMEM((tm, d), BF16)],
        compiler_params=_params(("parallel", "arbitrary")),
        name="in_proj",
    )(x2, mod3, g_mix.reshape(1, d), w_in_bf)


def _rope_kernel(pos_ref, invf_ref, sign_ref, cos_ref, sin_ref):
    ang = pos_ref[...].astype(F32) * invf_ref[...]
    cos_ref[...] = jnp.cos(ang)
    sin_ref[...] = jnp.sin(ang) * sign_ref[...]


def _rope_call(positions):
    t = positions.size
    tp = min(2048, t)
    half = HEAD_DIM // 2
    inv_freq = 1.0 / (ROPE_THETA ** (jnp.arange(0, HEAD_DIM, 2, dtype=F32) / HEAD_DIM))
    invf = jnp.concatenate([inv_freq, inv_freq]).reshape(1, HEAD_DIM)
    sign = jnp.concatenate([-jnp.ones((half,), F32), jnp.ones((half,), F32)]).reshape(1, HEAD_DIM)
    return pl.pallas_call(
        _rope_kernel,
        grid=(t // tp,),
        in_specs=[pl.BlockSpec((tp, 1), lambda i: (i, 0)),
                  pl.BlockSpec((1, HEAD_DIM), lambda i: (0, 0)),
                  pl.BlockSpec((1, HEAD_DIM), lambda i: (0, 0))],
        out_specs=[pl.BlockSpec((tp, HEAD_DIM), lambda i: (i, 0)),
                   pl.BlockSpec((tp, HEAD_DIM), lambda i: (i, 0))],
        out_shape=[jax.ShapeDtypeStruct((t, HEAD_DIM), F32)] * 2,
        compiler_params=_params(("parallel",)),
        name="rope_tables",
    )(positions.reshape(t, 1), invf, sign)


def _ret_tables():
    h = np.arange(RET_HEADS, dtype=np.float64)
    lg = np.log1p(-np.exp2(-5.0 - h))
    idx = np.arange(CHUNK, dtype=np.float64)
    diff = idx[:, None] - idx[None, :]
    intra = np.where(diff >= 0, np.exp(lg[:, None, None] * np.maximum(diff, 0.0)), 0.0)
    kdec = np.exp(lg[:, None] * (CHUNK - 1.0 - idx)[None, :])
    qdec = np.exp(lg[:, None] * (idx + 1.0)[None, :])
    cdec = np.exp(lg * CHUNK)
    bc = lambda a: np.ascontiguousarray(np.broadcast_to(a[:, :, None], (RET_HEADS, CHUNK, HEAD_DIM)))
    return (intra.astype(np.float32), bc(qdec).astype(np.float32), bc(kdec).astype(np.float32),
            [float(v) for v in cdec])


def _ret_kernel(q_ref, k_ref, v_ref, g_ref, cos_ref, sin_ref, intra_ref, qdec_ref, kdec_ref,
                beta_ref, o_ref, st_ref, *, cdec):
    @pl.when(pl.program_id(1) == 0)
    def _():
        st_ref[...] = jnp.zeros_like(st_ref)

    cos = cos_ref[...]
    sin = sin_ref[...]
    scale = HEAD_DIM ** -0.5
    nt = (((1,), (1,)), ((), ()))
    tn = (((0,), (0,)), ((), ()))
    outs = []
    for h in range(RET_HEADS):
        sl = slice(h * HEAD_DIM, (h + 1) * HEAD_DIM)
        q = q_ref[:, sl].astype(F32)
        k = k_ref[:, sl].astype(F32)
        v = v_ref[:, sl]
        qr = (q * cos + pltpu.roll(q, HEAD_DIM // 2, 1) * sin) * scale
        kr = k * cos + pltpu.roll(k, HEAD_DIM // 2, 1) * sin
        s = lax.dot_general(qr.astype(BF16), kr.astype(BF16), nt,
                            preferred_element_type=F32) * intra_ref[h]
        inner = jnp.dot(s.astype(BF16), v, preferred_element_type=F32)
        st = st_ref[h]
        cross = jnp.dot((qr * qdec_ref[h]).astype(BF16), st.astype(BF16),
                        preferred_element_type=F32)
        o = inner + cross
        kv = lax.dot_general((kr * kdec_ref[h]).astype(BF16), v, tn, preferred_element_type=F32)
        st_ref[h] = cdec[h] * st + kv
        oc = o - jnp.mean(o, axis=-1, keepdims=True)
        on = oc * lax.rsqrt(jnp.mean(oc * oc, axis=-1, keepdims=True) + EPS)
        outs.append(_silu(g_ref[:, sl].astype(F32)) * on)
    ret = jnp.concatenate(outs, axis=1)
    o_ref[...] = (_rms(ret) * beta_ref[...]).astype(o_ref.dtype)


def _ret_call(proj, cos, sin, beta_ret, batch, seq):
    t = batch * seq
    rw = RET_HEADS * HEAD_DIM
    nc = seq // CHUNK
    intra, qdec, kdec, cdec = _ret_tables()
    row = lambda which: pl.BlockSpec((None, CHUNK, rw), lambda b, n: (which, b * nc + n, 0))
    tab = lambda: pl.BlockSpec((RET_HEADS, CHUNK, HEAD_DIM), lambda b, n: (0, 0, 0))
    cs = lambda: pl.BlockSpec((CHUNK, HEAD_DIM), lambda b, n: (b * nc + n, 0))
    return pl.pallas_call(
        functools.partial(_ret_kernel, cdec=cdec),
        grid=(batch, nc),
        in_specs=[row(0), row(1), row(2), row(3), cs(), cs(), tab(), tab(), tab(),
                  pl.BlockSpec((1, rw), lambda b, n: (0, 0))],
        out_specs=pl.BlockSpec((CHUNK, rw), lambda b, n: (b * nc + n, 0)),
        out_shape=jax.ShapeDtypeStruct((t, rw), BF16),
        scratch_shapes=[pltpu.VMEM((RET_HEADS, HEAD_DIM, HEAD_DIM), F32)],
        compiler_params=_params(("parallel", "arbitrary")),
        name="retention",
    )(proj, proj, proj, proj, cos, sin, jnp.asarray(intra), jnp.asarray(qdec), jnp.asarray(kdec),
      beta_ret.reshape(1, rw))


def _lru_kernel(xr_ref, yr_ref, wc_ref, bc_ref, wg_ref, ba_ref, bx_ref, lam_ref, beta_ref,
                o_ref, xbuf_ref, h_ref, *, ts):
    @pl.when(pl.program_id(1) == 0)
    def _():
        xbuf_ref[0:SUBLANES, :] = jnp.zeros((SUBLANES, xbuf_ref.shape[1]), F32)
        h_ref[...] = jnp.zeros_like(h_ref)

    x = xr_ref[...].astype(F32)
    w = x.shape[1]
    xbuf_ref[SUBLANES:SUBLANES + ts, :] = x
    xc = bc_ref[...]
    for tap in range(CONV_WIDTH):
        back = CONV_WIDTH - 1 - tap
        xs = x if back == 0 else xbuf_ref[SUBLANES - back:SUBLANES - back + ts, :]
        xc = xc + xs * wc_ref[tap:tap + 1, :]
    xbuf_ref[0:SUBLANES, :] = x[ts - SUBLANES:ts, :]

    xcb = xc.astype(BF16)
    bd = w // LRU_BLOCKS
    rs, gs = [], []
    for g in range(LRU_BLOCKS):
        z = jnp.dot(xcb[:, g * bd:(g + 1) * bd], wg_ref[g], preferred_element_type=F32)
        rs.append(z[:, :bd])
        gs.append(z[:, bd:])
    r = jax.nn.sigmoid(jnp.concatenate(rs, axis=1) + ba_ref[...])
    i = jax.nn.sigmoid(jnp.concatenate(gs, axis=1) + bx_ref[...])
    log_a = -LRU_C * r * jax.nn.softplus(-lam_ref[...])
    a = jnp.exp(log_a)
    bt = jnp.sqrt(-jnp.tanh(log_a) * (a * a + 1.0)) * (i * xc)

    rows = lax.broadcasted_iota(I32, (ts, w), 0)
    acc_a, acc_b = a, bt
    sh = 1
    while sh < ts:
        prev_a = pltpu.roll(acc_a, sh, 0)
        prev_b = pltpu.roll(acc_b, sh, 0)
        m = rows >= sh
        acc_b = jnp.where(m, acc_a * prev_b + acc_b, acc_b)
        acc_a = jnp.where(m, acc_a * prev_a, acc_a)
        sh *= 2
    hs = acc_a * h_ref[...] + acc_b
    h_ref[...] = hs[ts - 1:ts, :]

    lru = hs * jax.nn.gelu(yr_ref[...].astype(F32), approximate=True)
    o_ref[...] = (_rms(lru) * beta_ref[...]).astype(o_ref.dtype)


def _lru_call(proj, w_conv, b_conv, w_rg_a, b_rg_a, w_rg_x, b_rg_x, lam, beta_lru, batch, seq):
    t = batch * seq
    w = w_conv.shape[1]
    ts = min(256, seq)
    nt = seq // ts
    wg = jnp.concatenate([w_rg_a, w_rg_x], axis=-1).astype(BF16)
    vec = lambda: pl.BlockSpec((1, w), lambda b, j: (0, 0))
    row = lambda which: pl.BlockSpec((None, ts, w), lambda b, j: (which, b * nt + j, 0))
    return pl.pallas_call(
        functools.partial(_lru_kernel, ts=ts),
        grid=(batch, nt),
        in_specs=[row(4), row(5),
                  pl.BlockSpec((CONV_WIDTH, w), lambda b, j: (0, 0)), vec(),
                  pl.BlockSpec(wg.shape, lambda b, j: (0, 0, 0)),
                  vec(), vec(), vec(), vec()],
        out_specs=pl.BlockSpec((ts, w), lambda b, j: (b * nt + j, 0)),
        out_shape=jax.ShapeDtypeStruct((t, w), BF16),
        scratch_shapes=[pltpu.VMEM((ts + SUBLANES, w), F32), pltpu.VMEM((1, w), F32)],
        compiler_params=_params(("parallel", "arbitrary")),
        name="rg_lru",
    )(proj, proj, w_conv, b_conv.reshape(1, w), wg, b_rg_a.reshape(1, w), b_rg_x.reshape(1, w),
      lam.reshape(1, w), beta_lru.reshape(1, w))


def _out_kernel(ret_ref, lru_ref, x_ref, mod_ref, g_ref, w_ref, x1_ref, h2_ref):
    rw = ret_ref.shape[1]
    mix = jnp.dot(ret_ref[...], w_ref[0:rw, :], preferred_element_type=F32)
    mix = mix + jnp.dot(lru_ref[...], w_ref[rw:, :], preferred_element_type=F32)
    x1 = x_ref[...] + mod_ref[2:3, :] * mix
    x1_ref[...] = x1
    h2 = _rms(x1) * g_ref[...] * (1.0 + mod_ref[4:5, :]) + mod_ref[3:4, :]
    half = h2.shape[1] // 2
    _store_rows(h2_ref, _pack2(h2[:, :half], h2[:, half:]))


def _out_call(ret, lru, x2, mod3, g_moe, w_out_bf, seq):
    t, d = x2.shape
    assert d // 2 == SUBLANES * LANES, "packed token rows are exactly one (8, 128) word tile"
    rw, lw = ret.shape[1], lru.shape[1]
    tm = min(512, seq)
    per_b = seq // tm
    return pl.pallas_call(
        _out_kernel,
        grid=(t // tm,),
        in_specs=[pl.BlockSpec((tm, rw), lambda i: (i, 0)),
                  pl.BlockSpec((tm, lw), lambda i: (i, 0)),
                  pl.BlockSpec((tm, d), lambda i: (i, 0)),
                  pl.BlockSpec((None, 6, d), lambda i: (i // per_b, 0, 0)),
                  pl.BlockSpec((1, d), lambda i: (0, 0)),
                  pl.BlockSpec((rw + lw, d), lambda i: (0, 0))],
        out_specs=[pl.BlockSpec((tm, d), lambda i: (i, 0)),
                   pl.BlockSpec((tm * SUBLANES, LANES), lambda i: (i, 0))],
        out_shape=[jax.ShapeDtypeStruct((t, d), F32),
                   jax.ShapeDtypeStruct((t * SUBLANES, LANES), U32)],
        compiler_params=_params(("parallel",)),
        name="out_proj",
    )(ret, lru, x2, mod3, g_moe.reshape(1, d), w_out_bf)


def _route_kernel(h_ref, wlo_ref, whi_ref, bias_ref, tri_ref, eidx_ref, wsel_ref, pos_ref,
                  cnt_ref, carry_ref):
    @pl.when(pl.program_id(0) == 0)
    def _():
        carry_ref[...] = jnp.zeros_like(carry_ref)

    tr = h_ref.shape[0] // SUBLANES
    lo, hi = _unpack2(_load_rows(h_ref))
    nt = (((1,), (1,)), ((), ()))
    logits = lax.dot_general(wlo_ref[...], lo.astype(BF16), nt, preferred_element_type=F32)
    logits = logits + lax.dot_general(whi_ref[...], hi.astype(BF16), nt,
                                      preferred_element_type=F32)
    scores = jax.nn.sigmoid(logits)
    biased = scores + bias_ref[:, 0:1]
    shape3 = (N_GROUPS, GROUP_SIZE, tr)
    s3 = scores.reshape(shape3)
    b3 = biased.reshape(shape3)
    member = lax.broadcasted_iota(I32, shape3, 1)
    group = lax.broadcasted_iota(I32, shape3, 0)
    expert = group * GROUP_SIZE + member

    m1 = jnp.max(b3, axis=1, keepdims=True)
    i1 = jnp.min(jnp.where(b3 == m1, member, GROUP_SIZE), axis=1, keepdims=True)
    m2 = jnp.max(jnp.where(member == i1, NEG_INF, b3), axis=1, keepdims=True)
    gscore = m1 + m2

    gid = lax.broadcasted_iota(I32, (N_GROUPS, 1, tr), 0)
    rank = jnp.zeros((N_GROUPS, 1, tr), I32)
    for j in range(N_GROUPS):
        gj = gscore[j:j + 1]
        ahead = (gj > gscore) | ((gj == gscore) & (gid > j))
        rank = rank + ahead.astype(I32)
    masked = jnp.where(rank < TOPK_GROUPS, b3, NEG_INF)

    idxs, vals = [], []
    picked = jnp.zeros(shape3, F32)
    for _ in range(TOP_K):
        m = jnp.max(jnp.max(masked, axis=0, keepdims=True), axis=1, keepdims=True)
        cand = jnp.where(masked == m, expert, N_EXPERTS)
        idx = jnp.min(jnp.min(cand, axis=0, keepdims=True), axis=1, keepdims=True)
        hit = expert == idx
        val = jnp.sum(jnp.sum(jnp.where(hit, s3, 0.0), axis=0, keepdims=True), axis=1, keepdims=True)
        masked = jnp.where(hit, NEG_INF, masked)
        picked = jnp.where(hit, 1.0, picked)
        idxs.append(idx)
        vals.append(val)
    total = vals[0]
    for v in vals[1:]:
        total = total + v

    before = jnp.dot(picked.reshape(N_EXPERTS, tr).astype(BF16), tri_ref[...],
                     preferred_element_type=F32)
    carry = carry_ref[...]
    posm = (before + carry[:, 0:1]).reshape(shape3)
    carry = carry + jnp.sum(picked.reshape(N_EXPERTS, tr), axis=1, keepdims=True)
    carry_ref[...] = carry
    cnt_ref[...] = carry

    for k in range(TOP_K):
        hit = expert == idxs[k]
        p = jnp.sum(jnp.sum(jnp.where(hit, posm, 0.0), axis=0, keepdims=True), axis=1, keepdims=True)
        eidx_ref[k:k + 1, :] = idxs[k].reshape(1, tr)
        wsel_ref[k:k + 1, :] = (vals[k] / total * ROUTED_SCALE).reshape(1, tr)
        pos_ref[k:k + 1, :] = p.reshape(1, tr).astype(I32)


def _route_call(h2p, w_router, b_router):
    t = h2p.shape[0] // SUBLANES
    half = SUBLANES * LANES
    tr = min(512, t)
    wt = w_router.T.astype(BF16)
    bias = jnp.broadcast_to(b_router.astype(F32)[:, None], (N_EXPERTS, LANES))
    tri = jnp.asarray(np.triu(np.ones((tr, tr), np.float32), k=1), dtype=BF16)
    kt = lambda dt: jax.ShapeDtypeStruct((TOP_K, t), dt)
    krow = lambda: pl.BlockSpec((TOP_K, tr), lambda i: (0, i))
    return pl.pallas_call(
        _route_kernel,
        grid=(t // tr,),
        in_specs=[pl.BlockSpec((tr * SUBLANES, LANES), lambda i: (i, 0)),
                  pl.BlockSpec((N_EXPERTS, half), lambda i: (0, 0)),
                  pl.BlockSpec((N_EXPERTS, half), lambda i: (0, 0)),
                  pl.BlockSpec((N_EXPERTS, LANES), lambda i: (0, 0)),
                  pl.BlockSpec((tr, tr), lambda i: (0, 0))],
        out_specs=[krow(), krow(), krow(),
                   pl.BlockSpec((N_EXPERTS, LANES), lambda i: (0, 0))],
        out_shape=[kt(I32), kt(F32), kt(I32), jax.ShapeDtypeStruct((N_EXPERTS, LANES), F32)],
        scratch_shapes=[pltpu.VMEM((N_EXPERTS, LANES), F32)],
        compiler_params=_params(("arbitrary",)),
        name="router",
    )(h2p, wt[:, :half], wt[:, half:], bias, tri)


def _dest_kernel(pst_ref, eidx_ref, pos_ref, o_ref):
    e = eidx_ref[...]
    acc = pos_ref[...]
    for j in range(N_EXPERTS):
        acc = acc + jnp.where(e == j, pst_ref[j], 0)
    o_ref[...] = acc


def _dest_call(pstarts, eidx, pos):
    k, t = eidx.shape
    tb = min(2048, t)
    blk = lambda: pl.BlockSpec((k, tb), lambda i, pst: (0, i))
    return pl.pallas_call(
        _dest_kernel,
        grid_spec=pltpu.PrefetchScalarGridSpec(
            num_scalar_prefetch=1, grid=(t // tb,),
            in_specs=[blk(), blk()], out_specs=blk()),
        out_shape=jax.ShapeDtypeStruct((k, t), I32),
        compiler_params=_params(("parallel",)),
        name="dest_rows",
    )(pstarts, eidx, pos)


def _swiglu_packed(x_ref, w1_ref, w2_ref):
    xw = _load_rows(x_ref)
    half = xw.shape[1]
    lo, hi = _unpack2(xw)
    gu = jnp.dot(lo.astype(BF16), w1_ref[0:half, :], preferred_element_type=F32)
    gu = gu + jnp.dot(hi.astype(BF16), w1_ref[half:, :], preferred_element_type=F32)
    hid = gu.shape[1] // 2
    act = _silu(gu[:, :hid]) * gu[:, hid:]
    return jnp.dot(act.astype(BF16), w2_ref[...], preferred_element_type=F32)


def _dispatch_kernel(meta_ref, dest_ref, h_ref, w1_ref, w2_ref, xs_ref, sh_ref, zero_ref,
                     sem_z, sem_r, *, td):

    @pl.when(pl.program_id(0) == 0)
    def _():
        _zero_rows(zero_ref)

        blk_rows = MOE_BLOCK * SUBLANES

        def block_copy(first_token):
            start = pl.multiple_of(first_token * SUBLANES, blk_rows)
            return pltpu.make_async_copy(zero_ref, xs_ref.at[pl.ds(start, blk_rows)], sem_z)

        def tail_copy(e):
            return block_copy(meta_ref[0, e] + meta_ref[2, e] - MOE_BLOCK)

        def unused_copy(j):
            return block_copy((meta_ref[3, 0] + j) * MOE_BLOCK)

        n_blocks = xs_ref.shape[0] // blk_rows
        for e in range(N_EXPERTS):
            @pl.when(meta_ref[2, e] > meta_ref[1, e])
            def _():
                tail_copy(e).start()

            @pl.when(meta_ref[3, 0] + e < n_blocks)
            def _():
                unused_copy(e).start()
        for e in range(N_EXPERTS):
            @pl.when(meta_ref[2, e] > meta_ref[1, e])
            def _():
                tail_copy(e).wait()

            @pl.when(meta_ref[3, 0] + e < n_blocks)
            def _():
                unused_copy(e).wait()

    def issue(r, carry):
        src = _token(h_ref, r)
        for k in range(TOP_K):
            pltpu.make_async_copy(src, _token(xs_ref, dest_ref[k, r]), sem_r).start(priority=k % 2)
        return carry

    lax.fori_loop(0, td, issue, 0)

    sh_ref[...] = _swiglu_packed(h_ref, w1_ref, w2_ref).astype(sh_ref.dtype)

    for k in range(TOP_K):
        pltpu.make_async_copy(h_ref, xs_ref.at[pl.ds(0, td * SUBLANES)], sem_r).wait()


def _dispatch_call(meta, dest, h2p, w_sh_in_bf, w_sh_out_bf, n_rows):
    t = h2p.shape[0] // SUBLANES
    d = 2 * SUBLANES * LANES
    td = min(256, t)
    return pl.pallas_call(
        functools.partial(_dispatch_kernel, td=td),
        grid=(t // td,),
        in_specs=[pl.BlockSpec(memory_space=pltpu.SMEM),
                  pl.BlockSpec((TOP_K, td), lambda i: (0, i), memory_space=pltpu.SMEM),
                  pl.BlockSpec((td * SUBLANES, LANES), lambda i: (i, 0)),
                  pl.BlockSpec(w_sh_in_bf.shape, lambda i: (0, 0)),
                  pl.BlockSpec(w_sh_out_bf.shape, lambda i: (0, 0))],
        out_specs=[pl.BlockSpec(memory_space=pl.ANY),
                   pl.BlockSpec((td, d), lambda i: (i, 0))],
        out_shape=[jax.ShapeDtypeStruct((n_rows * SUBLANES, LANES), U32),
                   jax.ShapeDtypeStruct((t, d), BF16)],
        scratch_shapes=[pltpu.VMEM((MOE_BLOCK * SUBLANES, LANES), U32),
                        pltpu.SemaphoreType.DMA, pltpu.SemaphoreType.DMA],
        compiler_params=_params(("arbitrary",)),
        name="dispatch_shared",
    )(meta, dest, h2p, w_sh_in_bf, w_sh_out_bf)


def _expert_kernel(be_ref, nu_ref, xs_ref, w1_ref, w2_ref, ys_ref, w1b_ref, w2b_ref):
    i = pl.program_id(0)
    prev = be_ref[jnp.maximum(i - 1, 0)]
    fresh = jnp.logical_or(i == 0, be_ref[i] != prev)

    @pl.when(fresh)
    def _():
        w1b_ref[...] = w1_ref[...].astype(BF16)
        w2b_ref[...] = w2_ref[...].astype(BF16)

    @pl.when(i < nu_ref[0])
    def _():
        out = _swiglu_packed(xs_ref, w1b_ref, w2b_ref)
        half = out.shape[1] // 2
        _store_rows(ys_ref, _pack2(out[:, :half], out[:, half:]))

    @pl.when(i >= nu_ref[0])
    def _():
        _zero_rows(ys_ref)


def _expert_call(blk_e, n_used, xs, w_exp_in, w_exp_out):
    blk_rows = MOE_BLOCK * SUBLANES
    _, d, h2 = w_exp_in.shape
    hid = w_exp_out.shape[1]
    n_blocks = xs.shape[0] // blk_rows
    rows_in = pl.BlockSpec((blk_rows, LANES), lambda i, be, nu: (jnp.minimum(i, nu[0] - 1), 0))
    rows_out = pl.BlockSpec((blk_rows, LANES), lambda i, be, nu: (i, 0))
    return pl.pallas_call(
        _expert_kernel,
        grid_spec=pltpu.PrefetchScalarGridSpec(
            num_scalar_prefetch=2, grid=(n_blocks,),
            in_specs=[rows_in,
                      pl.BlockSpec((None, d, h2), lambda i, be, nu: (be[i], 0, 0)),
                      pl.BlockSpec((None, hid, d), lambda i, be, nu: (be[i], 0, 0))],
            out_specs=rows_out,
            scratch_shapes=[pltpu.VMEM((d, h2), BF16), pltpu.VMEM((hid, d), BF16)]),
        out_shape=jax.ShapeDtypeStruct(xs.shape, U32),
        compiler_params=_params(("arbitrary",)),
        name="experts",
    )(blk_e, n_used, xs, w_exp_in, w_exp_out)


def _combine_kernel(dest_ref, w_ref, ys_ref, sh_ref, x1_ref, mod_ref, g_ref, o_ref, buf_ref, sem,
                    *, tc, final_norm):
    def issue(r, carry):
        for k in range(TOP_K):
            pltpu.make_async_copy(_token(ys_ref, dest_ref[k, r]), _token(buf_ref.at[k], r),
                                  sem).start(priority=k % 2)
        return carry

    lax.fori_loop(0, tc, issue, 0)
    for k in range(TOP_K):
        pltpu.make_async_copy(ys_ref.at[pl.ds(0, tc * SUBLANES)], buf_ref.at[k], sem).wait()

    acc_lo = acc_hi = None
    for k in range(TOP_K):
        wk = w_ref[:, k:k + 1]
        lo, hi = _unpack2(_load_rows(buf_ref.at[k]))
        acc_lo = wk * lo if acc_lo is None else acc_lo + wk * lo
        acc_hi = wk * hi if acc_hi is None else acc_hi + wk * hi
    routed = jnp.concatenate([acc_lo, acc_hi], axis=1)
    moe = routed + sh_ref[...].astype(F32)
    x2 = x1_ref[...] + mod_ref[5:6, :] * moe
    o_ref[...] = _rms(x2) * g_ref[...] if final_norm else x2


def _combine_call(dest, wsel_t, ys, shared, x1, mod3, g_final, seq, final_norm):
    t, d = x1.shape
    half = d // 2
    tc = min(256, seq)
    per_b = seq // tc
    return pl.pallas_call(
        functools.partial(_combine_kernel, tc=tc, final_norm=final_norm),
        grid=(t // tc,),
        in_specs=[pl.BlockSpec((TOP_K, tc), lambda i: (0, i), memory_space=pltpu.SMEM),
                  pl.BlockSpec((tc, TOP_K), lambda i: (i, 0)),
                  pl.BlockSpec(memory_space=pl.ANY),
                  pl.BlockSpec((tc, d), lambda i: (i, 0)),
                  pl.BlockSpec((tc, d), lambda i: (i, 0)),
                  pl.BlockSpec((None, 6, d), lambda i: (i // per_b, 0, 0)),
                  pl.BlockSpec((1, d), lambda i: (0, 0))],
        out_specs=pl.BlockSpec((tc, d), lambda i: (i, 0)),
        out_shape=jax.ShapeDtypeStruct((t, d), F32),
        scratch_shapes=[pltpu.VMEM((TOP_K, tc * SUBLANES, LANES), U32),
                        pltpu.SemaphoreType.DMA],
        compiler_params=_params(("arbitrary",)),
        name="combine_final",
    )(dest, wsel_t, ys, shared, x1, mod3, g_final.reshape(1, d))


def _moe_plan(counts):
    pcounts = (counts + MOE_BLOCK - 1) // MOE_BLOCK * MOE_BLOCK
    pends = jnp.cumsum(pcounts)
    pstarts = pends - pcounts
    return pstarts.astype(I32), pcounts.astype(I32), pends.astype(I32)


def kernel(x, c, positions, w_ada, b_ada, g_mix, w_in, w_conv, b_conv, w_rg_a, b_rg_a, w_rg_x, b_rg_x, lam, beta_ret, beta_lru, w_out, g_moe, w_router, b_router, w_exp_in, w_exp_out, w_sh_in, w_sh_out, g_final):
    batch, seq, d = x.shape
    t = batch * seq
    depth = w_ada.shape[0]
    xcur = x.reshape(t, d)
    cos, sin = _rope_call(positions)
    n_rows = (t * TOP_K + MOE_BLOCK - 1) // MOE_BLOCK * MOE_BLOCK + N_EXPERTS * MOE_BLOCK
    n_blocks = n_rows // MOE_BLOCK
    for l in range(depth):
        mod3 = _mod_call(c, w_ada[l], b_ada[l]).reshape(batch, 6, d)
        proj = _in_call(xcur, mod3, g_mix[l], w_in[l].astype(BF16), seq)
        ret = _ret_call(proj, cos, sin, beta_ret[l], batch, seq)
        lru = _lru_call(proj, w_conv[l], b_conv[l], w_rg_a[l], b_rg_a[l], w_rg_x[l], b_rg_x[l],
                        lam[l], beta_lru[l], batch, seq)
        x1, h2p = _out_call(ret, lru, xcur, mod3, g_moe[l], w_out[l].astype(BF16), seq)

        eidx, wsel, pos, cnt = _route_call(h2p, w_router[l], b_router[l])
        counts = cnt[:, 0].astype(I32)
        pstarts, pcounts, pends = _moe_plan(counts)
        dest = _dest_call(pstarts, eidx, pos)
        n_used = (pends[-1] // MOE_BLOCK).astype(I32)
        blk = jnp.minimum(jnp.arange(n_blocks, dtype=I32), n_used - 1) * MOE_BLOCK
        blk_e = jnp.sum((pends[None, :] <= blk[:, None]).astype(I32), axis=1)
        blk_e = jnp.minimum(blk_e, N_EXPERTS - 1)
        meta = jnp.stack([pstarts, counts, pcounts, jnp.full_like(counts, n_used)])

        xs, shared = _dispatch_call(meta, dest, h2p, w_sh_in[l].astype(BF16),
                                    w_sh_out[l].astype(BF16), n_rows)
        ys = _expert_call(blk_e, n_used.reshape(1), xs, w_exp_in[l], w_exp_out[l])
        xcur = _combine_call(dest, wsel.T, ys, shared, x1, mod3, g_final, seq,
                             final_norm=(l == depth - 1))
    return xcur.reshape(batch, seq, d)
```

```python
import functools

import numpy as np
import jax
import jax.numpy as jnp
from jax import lax
from jax.experimental import pallas as pl
from jax.experimental.pallas import tpu as pltpu

F32 = jnp.float32
BF16 = jnp.bfloat16
I32 = jnp.int32
U32 = jnp.uint32

RET_HEADS = 8
HEAD_DIM = 128
CHUNK = 128
ROPE_THETA = 10000.0
LRU_BLOCKS = 8
CONV_WIDTH = 4
LRU_C = 8.0
N_EXPERTS = 64
N_GROUPS = 8
GROUP_SIZE = N_EXPERTS // N_GROUPS
TOPK_GROUPS = 4
TOP_K = 8
ROUTED_SCALE = 2.5
EPS = 1e-6

LANES = 128
SUBLANES = 8
VMEM_LIMIT_BYTES = 56 * 1024 * 1024

MOE_BLOCK = 256
NEG_INF = float("-inf")


def _params(semantics, vmem=VMEM_LIMIT_BYTES):
    return pltpu.CompilerParams(dimension_semantics=semantics, vmem_limit_bytes=vmem)


def _silu(x):
    return x * jax.nn.sigmoid(x)


def _rms(x):
    return x * lax.rsqrt(jnp.mean(x * x, axis=-1, keepdims=True) + EPS)


def _pack2(lo, hi):
    return pltpu.pack_elementwise([lo, hi], packed_dtype=jnp.bfloat16)


def _unpack2(w):
    lo = pltpu.unpack_elementwise(w, index=0, packed_dtype=jnp.bfloat16, unpacked_dtype=F32)
    hi = pltpu.unpack_elementwise(w, index=1, packed_dtype=jnp.bfloat16, unpacked_dtype=F32)
    return lo, hi


def _load_rows(ref):
    rows = ref.shape[0] // SUBLANES
    return jnp.concatenate([ref[pl.ds(j, rows, stride=SUBLANES), :] for j in range(SUBLANES)],
                           axis=1)


def _store_rows(ref, v):
    rows = ref.shape[0] // SUBLANES
    for j in range(SUBLANES):
        ref[pl.ds(j, rows, stride=SUBLANES), :] = v[:, j * LANES:(j + 1) * LANES]


def _zero_rows(ref):
    rows = ref.shape[0] // SUBLANES
    z = jnp.zeros((rows, LANES), F32)
    w = _pack2(z, z)
    for j in range(SUBLANES):
        ref[pl.ds(j, rows, stride=SUBLANES), :] = w


def _token(ref, r):
    return ref.at[pl.ds(pl.multiple_of(r * SUBLANES, SUBLANES), SUBLANES)]


def _mod_kernel(c_ref, w_ref, b_ref, o_ref):
    cs = _silu(c_ref[...])
    o_ref[...] = jnp.dot(cs, w_ref[...], preferred_element_type=F32,
                         precision=lax.Precision.HIGHEST) + b_ref[...]


def _mod_call(c, w_ada, b_ada):
    b, d = c.shape
    n = w_ada.shape[1]
    tn = 1024
    return pl.pallas_call(
        _mod_kernel,
        grid=(n // tn,),
        in_specs=[pl.BlockSpec((b, d), lambda j: (0, 0)),
                  pl.BlockSpec((d, tn), lambda j: (0, j)),
                  pl.BlockSpec((1, tn), lambda j: (0, j))],
        out_specs=pl.BlockSpec((b, tn), lambda j: (0, j)),
        out_shape=jax.ShapeDtypeStruct((b, n), F32),
        compiler_params=_params(("parallel",)),
        name="adaln_mod",
    )(c, w_ada, b_ada.reshape(1, n))


def _in_kernel(x_ref, mod_ref, g_ref, w_ref, o_ref, h_ref):
    @pl.when(pl.program_id(1) == 0)
    def _():
        y = _rms(x_ref[...]) * g_ref[...]
        h_ref[...] = (y * (1.0 + mod_ref[1:2, :]) + mod_ref[0:1, :]).astype(BF16)

    o_ref[...] = jnp.dot(h_ref[...], w_ref[...], preferred_element_type=F32).astype(o_ref.dtype)


def _in_call(x2, mod3, g_mix, w_in_bf, seq):
    t, d = x2.shape
    n_out = w_in_bf.shape[1] // 1024
    tm = min(1024, seq)
    per_b = seq // tm
    return pl.pallas_call(
        _in_kernel,
        grid=(t // tm, n_out),
        in_specs=[pl.BlockSpec((tm, d), lambda i, j: (i, 0)),
                  pl.BlockSpec((None, 6, d), lambda i, j: (i // per_b, 0, 0)),
                  pl.BlockSpec((1, d), lambda i, j: (0, 0)),
                  pl.BlockSpec((d, 1024), lambda i, j: (0, j))],
        out_specs=pl.BlockSpec((None, tm, 1024), lambda i, j: (j, i, 0)),
        out_shape=jax.ShapeDtypeStruct((n_out, t, 1024), BF16),
        scratch_shapes=[pltpu.VMEM((tm, d), BF16)],
        compiler_params=_params(("parallel", "arbitrary")),
        name="in_proj",
    )(x2, mod3, g_mix.reshape(1, d), w_in_bf)


def _rope_kernel(pos_ref, invf_ref, sign_ref, cos_ref, sin_ref):
    ang = pos_ref[...].astype(F32) * invf_ref[...]
    cos_ref[...] = jnp.cos(ang)
    sin_ref[...] = jnp.sin(ang) * sign_ref[...]


def _rope_call(positions):
    t = positions.size
    tp = min(2048, t)
    half = HEAD_DIM // 2
    inv_freq = 1.0 / (ROPE_THETA ** (jnp.arange(0, HEAD_DIM, 2, dtype=F32) / HEAD_DIM))
    invf = jnp.concatenate([inv_freq, inv_freq]).reshape(1, HEAD_DIM)
    sign = jnp.concatenate([-jnp.ones((half,), F32), jnp.ones((half,), F32)]).reshape(1, HEAD_DIM)
    return pl.pallas_call(
        _rope_kernel,
        grid=(t // tp,),
        in_specs=[pl.BlockSpec((tp, 1), lambda i: (i, 0)),
                  pl.BlockSpec((1, HEAD_DIM), lambda i: (0, 0)),
                  pl.BlockSpec((1, HEAD_DIM), lambda i: (0, 0))],
        out_specs=[pl.BlockSpec((tp, HEAD_DIM), lambda i: (i, 0)),
                   pl.BlockSpec((tp, HEAD_DIM), lambda i: (i, 0))],
        out_shape=[jax.ShapeDtypeStruct((t, HEAD_DIM), F32)] * 2,
        compiler_params=_params(("parallel",)),
        name="rope_tables",
    )(positions.reshape(t, 1), invf, sign)


def _ret_tables():
    h = np.arange(RET_HEADS, dtype=np.float64)
    lg = np.log1p(-np.exp2(-5.0 - h))
    idx = np.arange(CHUNK, dtype=np.float64)
    diff = idx[:, None] - idx[None, :]
    intra = np.where(diff >= 0, np.exp(lg[:, None, None] * np.maximum(diff, 0.0)), 0.0)
    kdec = np.exp(lg[:, None] * (CHUNK - 1.0 - idx)[None, :])
    qdec = np.exp(lg[:, None] * (idx + 1.0)[None, :])
    cdec = np.exp(lg * CHUNK)
    bc = lambda a: np.ascontiguousarray(np.broadcast_to(a[:, :, None], (RET_HEADS, CHUNK, HEAD_DIM)))
    return (intra.astype(np.float32), bc(qdec).astype(np.float32), bc(kdec).astype(np.float32),
            [float(v) for v in cdec])


def _ret_kernel(q_ref, k_ref, v_ref, g_ref, cos_ref, sin_ref, intra_ref, qdec_ref, kdec_ref,
                beta_ref, o_ref, st_ref, *, cdec):
    @pl.when(pl.program_id(1) == 0)
    def _():
        st_ref[...] = jnp.zeros_like(st_ref)

    cos = cos_ref[...]
    sin = sin_ref[...]
    scale = HEAD_DIM ** -0.5
    nt = (((1,), (1,)), ((), ()))
    tn = (((0,), (0,)), ((), ()))
    outs = []
    for h in range(RET_HEADS):
        sl = slice(h * HEAD_DIM, (h + 1) * HEAD_DIM)
        q = q_ref[:, sl].astype(F32)
        k = k_ref[:, sl].astype(F32)
        v = v_ref[:, sl]
        qr = (q * cos + pltpu.roll(q, HEAD_DIM // 2, 1) * sin) * scale
        kr = k * cos + pltpu.roll(k, HEAD_DIM // 2, 1) * sin
        s = lax.dot_general(qr.astype(BF16), kr.astype(BF16), nt,
                            preferred_element_type=F32) * intra_ref[h]
        inner = jnp.dot(s.astype(BF16), v, preferred_element_type=F32)
        st = st_ref[h]
        cross = jnp.dot((qr * qdec_ref[h]).astype(BF16), st.astype(BF16),
                        preferred_element_type=F32)
        o = inner + cross
        kv = lax.dot_general((kr * kdec_ref[h]).astype(BF16), v, tn, preferred_element_type=F32)
        st_ref[h] = cdec[h] * st + kv
        oc = o - jnp.mean(o, axis=-1, keepdims=True)
        on = oc * lax.rsqrt(jnp.mean(oc * oc, axis=-1, keepdims=True) + EPS)
        outs.append(_silu(g_ref[:, sl].astype(F32)) * on)
    ret = jnp.concatenate(outs, axis=1)
    o_ref[...] = (_rms(ret) * beta_ref[...]).astype(o_ref.dtype)


def _ret_call(proj, cos, sin, beta_ret, batch, seq):
    t = batch * seq
    rw = RET_HEADS * HEAD_DIM
    nc = seq // CHUNK
    intra, qdec, kdec, cdec = _ret_tables()
    row = lambda which: pl.BlockSpec((None, CHUNK, rw), lambda b, n: (which, b * nc + n, 0))
    tab = lambda: pl.BlockSpec((RET_HEADS, CHUNK, HEAD_DIM), lambda b, n: (0, 0, 0))
    cs = lambda: pl.BlockSpec((CHUNK, HEAD_DIM), lambda b, n: (b * nc + n, 0))
    return pl.pallas_call(
        functools.partial(_ret_kernel, cdec=cdec),
        grid=(batch, nc),
        in_specs=[row(0), row(1), row(2), row(3), cs(), cs(), tab(), tab(), tab(),
                  pl.BlockSpec((1, rw), lambda b, n: (0, 0))],
        out_specs=pl.BlockSpec((CHUNK, rw), lambda b, n: (b * nc + n, 0)),
        out_shape=jax.ShapeDtypeStruct((t, rw), BF16),
        scratch_shapes=[pltpu.VMEM((RET_HEADS, HEAD_DIM, HEAD_DIM), F32)],
        compiler_params=_params(("parallel", "arbitrary")),
        name="retention",
    )(proj, proj, proj, proj, cos, sin, jnp.asarray(intra), jnp.asarray(qdec), jnp.asarray(kdec),
      beta_ret.reshape(1, rw))


def _lru_kernel(xr_ref, yr_ref, wc_ref, bc_ref, wg_ref, ba_ref, bx_ref, lam_ref, beta_ref,
                o_ref, xbuf_ref, h_ref, *, ts):
    @pl.when(pl.program_id(1) == 0)
    def _():
        xbuf_ref[0:SUBLANES, :] = jnp.zeros((SUBLANES, xbuf_ref.shape[1]), F32)
        h_ref[...] = jnp.zeros_like(h_ref)

    x = xr_ref[...].astype(F32)
    w = x.shape[1]
    xbuf_ref[SUBLANES:SUBLANES + ts, :] = x
    xc = bc_ref[...]
    for tap in range(CONV_WIDTH):
        back = CONV_WIDTH - 1 - tap
        xs = x if back == 0 else xbuf_ref[SUBLANES - back:SUBLANES - back + ts, :]
        xc = xc + xs * wc_ref[tap:tap + 1, :]
    xbuf_ref[0:SUBLANES, :] = x[ts - SUBLANES:ts, :]

    xcb = xc.astype(BF16)
    bd = w // LRU_BLOCKS
    rs, gs = [], []
    for g in range(LRU_BLOCKS):
        z = jnp.dot(xcb[:, g * bd:(g + 1) * bd], wg_ref[g], preferred_element_type=F32)
        rs.append(z[:, :bd])
        gs.append(z[:, bd:])
    r = jax.nn.sigmoid(jnp.concatenate(rs, axis=1) + ba_ref[...])
    i = jax.nn.sigmoid(jnp.concatenate(gs, axis=1) + bx_ref[...])
    log_a = -LRU_C * r * jax.nn.softplus(-lam_ref[...])
    a = jnp.exp(log_a)
    bt = jnp.sqrt(-jnp.tanh(log_a) * (a * a + 1.0)) * (i * xc)

    rows = lax.broadcasted_iota(I32, (ts, w), 0)
    acc_a, acc_b = a, bt
    sh = 1
    while sh < ts:
        prev_a = pltpu.roll(acc_a, sh, 0)
        prev_b = pltpu.roll(acc_b, sh, 0)
        m = rows >= sh
        acc_b = jnp.where(m, acc_a * prev_b + acc_b, acc_b)
        acc_a = jnp.where(m, acc_a * prev_a, acc_a)
        sh *= 2
    hs = acc_a * h_ref[...] + acc_b
    h_ref[...] = hs[ts - 1:ts, :]

    lru = hs * jax.nn.gelu(yr_ref[...].astype(F32), approximate=True)
    o_ref[...] = (_rms(lru) * beta_ref[...]).astype(o_ref.dtype)


def _lru_call(proj, w_conv, b_conv, w_rg_a, b_rg_a, w_rg_x, b_rg_x, lam, beta_lru, batch, seq):
    t = batch * seq
    w = w_conv.shape[1]
    ts = min(256, seq)
    nt = seq // ts
    wg = jnp.concatenate([w_rg_a, w_rg_x], axis=-1).astype(BF16)
    vec = lambda: pl.BlockSpec((1, w), lambda b, j: (0, 0))
    row = lambda which: pl.BlockSpec((None, ts, w), lambda b, j: (which, b * nt + j, 0))
    return pl.pallas_call(
        functools.partial(_lru_kernel, ts=ts),
        grid=(batch, nt),
        in_specs=[row(4), row(5),
                  pl.BlockSpec((CONV_WIDTH, w), lambda b, j: (0, 0)), vec(),
                  pl.BlockSpec(wg.shape, lambda b, j: (0, 0, 0)),
                  vec(), vec(), vec(), vec()],
        out_specs=pl.BlockSpec((ts, w), lambda b, j: (b * nt + j, 0)),
        out_shape=jax.ShapeDtypeStruct((t, w), BF16),
        scratch_shapes=[pltpu.VMEM((ts + SUBLANES, w), F32), pltpu.VMEM((1, w), F32)],
        compiler_params=_params(("parallel", "arbitrary")),
        name="rg_lru",
    )(proj, proj, w_conv, b_conv.reshape(1, w), wg, b_rg_a.reshape(1, w), b_rg_x.reshape(1, w),
      lam.reshape(1, w), beta_lru.reshape(1, w))


def _out_kernel(ret_ref, lru_ref, x_ref, mod_ref, g_ref, w_ref, x1_ref, h2_ref):
    rw = ret_ref.shape[1]
    mix = jnp.dot(ret_ref[...], w_ref[0:rw, :], preferred_element_type=F32)
    mix = mix + jnp.dot(lru_ref[...], w_ref[rw:, :], preferred_element_type=F32)
    x1 = x_ref[...] + mod_ref[2:3, :] * mix
    x1_ref[...] = x1
    h2 = _rms(x1) * g_ref[...] * (1.0 + mod_ref[4:5, :]) + mod_ref[3:4, :]
    half = h2.shape[1] // 2
    _store_rows(h2_ref, _pack2(h2[:, :half], h2[:, half:]))


def _out_call(ret, lru, x2, mod3, g_moe, w_out_bf, seq):
    t, d = x2.shape
    assert d // 2 == SUBLANES * LANES, "packed token rows are exactly one (8, 128) word tile"
    rw, lw = ret.shape[1], lru.shape[1]
    tm = min(512, seq)
    per_b = seq // tm
    return pl.pallas_call(
        _out_kernel,
        grid=(t // tm,),
        in_specs=[pl.BlockSpec((tm, rw), lambda i: (i, 0)),
                  pl.BlockSpec((tm, lw), lambda i: (i, 0)),
                  pl.BlockSpec((tm, d), lambda i: (i, 0)),
                  pl.BlockSpec((None, 6, d), lambda i: (i // per_b, 0, 0)),
                  pl.BlockSpec((1, d), lambda i: (0, 0)),
                  pl.BlockSpec((rw + lw, d), lambda i: (0, 0))],
        out_specs=[pl.BlockSpec((tm, d), lambda i: (i, 0)),
                   pl.BlockSpec((tm * SUBLANES, LANES), lambda i: (i, 0))],
        out_shape=[jax.ShapeDtypeStruct((t, d), F32),
                   jax.ShapeDtypeStruct((t * SUBLANES, LANES), U32)],
        compiler_params=_params(("parallel",)),
        name="out_proj",
    )(ret, lru, x2, mod3, g_moe.reshape(1, d), w_out_bf)


def _route_kernel(h_ref, wlo_ref, whi_ref, bias_ref, tri_ref, eidx_ref, wsel_ref, pos_ref,
                  cnt_ref, carry_ref):
    @pl.when(pl.program_id(0) == 0)
    def _():
        carry_ref[...] = jnp.zeros_like(carry_ref)

    tr = h_ref.shape[0] // SUBLANES
    lo, hi = _unpack2(_load_rows(h_ref))
    nt = (((1,), (1,)), ((), ()))
    logits = lax.dot_general(wlo_ref[...], lo.astype(BF16), nt, preferred_element_type=F32)
    logits = logits + lax.dot_general(whi_ref[...], hi.astype(BF16), nt,
                                      preferred_element_type=F32)
    scores = jax.nn.sigmoid(logits)
    biased = scores + bias_ref[:, 0:1]
    shape3 = (N_GROUPS, GROUP_SIZE, tr)
    s3 = scores.reshape(shape3)
    b3 = biased.reshape(shape3)
    member = lax.broadcasted_iota(I32, shape3, 1)
    group = lax.broadcasted_iota(I32, shape3, 0)
    expert = group * GROUP_SIZE + member

    m1 = jnp.max(b3, axis=1, keepdims=True)
    i1 = jnp.min(jnp.where(b3 == m1, member, GROUP_SIZE), axis=1, keepdims=True)
    m2 = jnp.max(jnp.where(member == i1, NEG_INF, b3), axis=1, keepdims=True)
    gscore = m1 + m2

    gid = lax.broadcasted_iota(I32, (N_GROUPS, 1, tr), 0)
    rank = jnp.zeros((N_GROUPS, 1, tr), I32)
    for j in range(N_GROUPS):
        gj = gscore[j:j + 1]
        ahead = (gj > gscore) | ((gj == gscore) & (gid > j))
        rank = rank + ahead.astype(I32)
    masked = jnp.where(rank < TOPK_GROUPS, b3, NEG_INF)

    idxs, vals = [], []
    picked = jnp.zeros(shape3, F32)
    for _ in range(TOP_K):
        m = jnp.max(jnp.max(masked, axis=0, keepdims=True), axis=1, keepdims=True)
        cand = jnp.where(masked == m, expert, N_EXPERTS)
        idx = jnp.min(jnp.min(cand, axis=0, keepdims=True), axis=1, keepdims=True)
        hit = expert == idx
        val = jnp.sum(jnp.sum(jnp.where(hit, s3, 0.0), axis=0, keepdims=True), axis=1, keepdims=True)
        masked = jnp.where(hit, NEG_INF, masked)
        picked = jnp.where(hit, 1.0, picked)
        idxs.append(idx)
        vals.append(val)
    total = vals[0]
    for v in vals[1:]:
        total = total + v

    before = jnp.dot(picked.reshape(N_EXPERTS, tr).astype(BF16), tri_ref[...],
                     preferred_element_type=F32)
    carry = carry_ref[...]
    posm = (before + carry[:, 0:1]).reshape(shape3)
    carry = carry + jnp.sum(picked.reshape(N_EXPERTS, tr), axis=1, keepdims=True)
    carry_ref[...] = carry
    cnt_ref[...] = carry

    for k in range(TOP_K):
        hit = expert == idxs[k]
        p = jnp.sum(jnp.sum(jnp.where(hit, posm, 0.0), axis=0, keepdims=True), axis=1, keepdims=True)
        eidx_ref[k:k + 1, :] = idxs[k].reshape(1, tr)
        wsel_ref[k:k + 1, :] = (vals[k] / total * ROUTED_SCALE).reshape(1, tr)
        pos_ref[k:k + 1, :] = p.reshape(1, tr).astype(I32)


def _route_call(h2p, w_router, b_router):
    t = h2p.shape[0] // SUBLANES
    half = SUBLANES * LANES
    tr = min(512, t)
    wt = w_router.T.astype(BF16)
    bias = jnp.broadcast_to(b_router.astype(F32)[:, None], (N_EXPERTS, LANES))
    tri = jnp.asarray(np.triu(np.ones((tr, tr), np.float32), k=1), dtype=BF16)
    kt = lambda dt: jax.ShapeDtypeStruct((TOP_K, t), dt)
    krow = lambda: pl.BlockSpec((TOP_K, tr), lambda i: (0, i))
    return pl.pallas_call(
        _route_kernel,
        grid=(t // tr,),
        in_specs=[pl.BlockSpec((tr * SUBLANES, LANES), lambda i: (i, 0)),
                  pl.BlockSpec((N_EXPERTS, half), lambda i: (0, 0)),
                  pl.BlockSpec((N_EXPERTS, half), lambda i: (0, 0)),
                  pl.BlockSpec((N_EXPERTS, LANES), lambda i: (0, 0)),
                  pl.BlockSpec((tr, tr), lambda i: (0, 0))],
        out_specs=[krow(), krow(), krow(),
                   pl.BlockSpec((N_EXPERTS, LANES), lambda i: (0, 0))],
        out_shape=[kt(I32), kt(F32), kt(I32), jax.ShapeDtypeStruct((N_EXPERTS, LANES), F32)],
        scratch_shapes=[pltpu.VMEM((N_EXPERTS, LANES), F32)],
        compiler_params=_params(("arbitrary",)),
        name="router",
    )(h2p, wt[:, :half], wt[:, half:], bias, tri)


def _dest_kernel(pst_ref, eidx_ref, pos_ref, o_ref):
    e = eidx_ref[...]
    acc = pos_ref[...]
    for j in range(N_EXPERTS):
        acc = acc + jnp.where(e == j, pst_ref[j], 0)
    o_ref[...] = acc


def _dest_call(pstarts, eidx, pos):
    k, t = eidx.shape
    tb = min(2048, t)
    blk = lambda: pl.BlockSpec((k, tb), lambda i, pst: (0, i))
    return pl.pallas_call(
        _dest_kernel,
        grid_spec=pltpu.PrefetchScalarGridSpec(
            num_scalar_prefetch=1, grid=(t // tb,),
            in_specs=[blk(), blk()], out_specs=blk()),
        out_shape=jax.ShapeDtypeStruct((k, t), I32),
        compiler_params=_params(("parallel",)),
        name="dest_rows",
    )(pstarts, eidx, pos)


def _swiglu_packed(x_ref, w1_ref, w2_ref):
    xw = _load_rows(x_ref)
    half = xw.shape[1]
    lo, hi = _unpack2(xw)
    gu = jnp.dot(lo.astype(BF16), w1_ref[0:half, :], preferred_element_type=F32)
    gu = gu + jnp.dot(hi.astype(BF16), w1_ref[half:, :], preferred_element_type=F32)
    hid = gu.shape[1] // 2
    act = _silu(gu[:, :hid]) * gu[:, hid:]
    return jnp.dot(act.astype(BF16), w2_ref[...], preferred_element_type=F32)


def _dispatch_kernel(meta_ref, dest_ref, h_ref, w1_ref, w2_ref, xs_ref, sh_ref, zero_ref,
                     sem_z, sem_r, *, td):

    @pl.when(pl.program_id(0) == 0)
    def _():
        _zero_rows(zero_ref)

        blk_rows = MOE_BLOCK * SUBLANES

        def block_copy(first_token):
            start = pl.multiple_of(first_token * SUBLANES, blk_rows)
            return pltpu.make_async_copy(zero_ref, xs_ref.at[pl.ds(start, blk_rows)], sem_z)

        def tail_copy(e):
            return block_copy(meta_ref[0, e] + meta_ref[2, e] - MOE_BLOCK)

        def unused_copy(j):
            return block_copy((meta_ref[3, 0] + j) * MOE_BLOCK)

        n_blocks = xs_ref.shape[0] // blk_rows
        for e in range(N_EXPERTS):
            @pl.when(meta_ref[2, e] > meta_ref[1, e])
            def _():
                tail_copy(e).start()

            @pl.when(meta_ref[3, 0] + e < n_blocks)
            def _():
                unused_copy(e).start()
        for e in range(N_EXPERTS):
            @pl.when(meta_ref[2, e] > meta_ref[1, e])
            def _():
                tail_copy(e).wait()

            @pl.when(meta_ref[3, 0] + e < n_blocks)
            def _():
                unused_copy(e).wait()

    def issue(r, carry):
        src = _token(h_ref, r)
        for k in range(TOP_K):
            pltpu.make_async_copy(src, _token(xs_ref, dest_ref[k, r]), sem_r).start(priority=k % 2)
        return carry

    lax.fori_loop(0, td, issue, 0)

    sh_ref[...] = _swiglu_packed(h_ref, w1_ref, w2_ref).astype(sh_ref.dtype)

    for k in range(TOP_K):
        pltpu.make_async_copy(h_ref, xs_ref.at[pl.ds(0, td * SUBLANES)], sem_r).wait()


def _dispatch_call(meta, dest, h2p, w_sh_in_bf, w_sh_out_bf, n_rows):
    t = h2p.shape[0] // SUBLANES
    d = 2 * SUBLANES * LANES
    td = min(256, t)
    return pl.pallas_call(
        functools.partial(_dispatch_kernel, td=td),
        grid=(t // td,),
        in_specs=[pl.BlockSpec(memory_space=pltpu.SMEM),
                  pl.BlockSpec((TOP_K, td), lambda i: (0, i), memory_space=pltpu.SMEM),
                  pl.BlockSpec((td * SUBLANES, LANES), lambda i: (i, 0)),
                  pl.BlockSpec(w_sh_in_bf.shape, lambda i: (0, 0)),
                  pl.BlockSpec(w_sh_out_bf.shape, lambda i: (0, 0))],
        out_specs=[pl.BlockSpec(memory_space=pl.ANY),
                   pl.BlockSpec((td, d), lambda i: (i, 0))],
        out_shape=[jax.ShapeDtypeStruct((n_rows * SUBLANES, LANES), U32),
                   jax.ShapeDtypeStruct((t, d), BF16)],
        scratch_shapes=[pltpu.VMEM((MOE_BLOCK * SUBLANES, LANES), U32),
                        pltpu.SemaphoreType.DMA, pltpu.SemaphoreType.DMA],
        compiler_params=_params(("arbitrary",)),
        name="dispatch_shared",
    )(meta, dest, h2p, w_sh_in_bf, w_sh_out_bf)


def _expert_kernel(meta_ref, xs_ref, w1_ref, w2_ref, ys_ref, w1b_ref, w2b_ref, xbuf, ybuf,
                   sem_in, sem_out, *, n_blocks):
    e = pl.program_id(0)
    blk_rows = MOE_BLOCK * SUBLANES
    first = meta_ref[0, e] // MOE_BLOCK
    nblk = meta_ref[2, e] // MOE_BLOCK
    nfull = nblk // 2

    def rows_of(b):
        return pl.ds(pl.multiple_of((first + b) * blk_rows, blk_rows), blk_rows)

    def fetch(b, s):
        return pltpu.make_async_copy(xs_ref.at[rows_of(b)], xbuf.at[s], sem_in.at[s])

    def store(b, s):
        return pltpu.make_async_copy(ybuf.at[s], ys_ref.at[rows_of(b)], sem_out.at[s])

    def compute(s):
        out = _swiglu_packed(xbuf.at[s], w1b_ref, w2b_ref)
        half = out.shape[1] // 2
        _store_rows(ybuf.at[s], _pack2(out[:, :half], out[:, half:]))

    @pl.when(nblk > 0)
    def _():
        @pl.when(nfull > 0)
        def _():
            fetch(0, 0).start()
            fetch(1, 1).start()

        w1b_ref[...] = w1_ref[...].astype(BF16)
        w2b_ref[...] = w2_ref[...].astype(BF16)

        def pair(p, carry):
            s0 = 2 * lax.rem(p, 2)
            n0 = 2 - s0
            fetch(2 * p, s0).wait()
            fetch(2 * p + 1, s0 + 1).wait()

            @pl.when(p + 1 < nfull)
            def _():
                fetch(2 * p + 2, n0).start()
                fetch(2 * p + 3, n0 + 1).start()

            @pl.when(p >= 2)
            def _():
                store(2 * p - 4, s0).wait()
                store(2 * p - 3, s0 + 1).wait()

            compute(s0)
            compute(s0 + 1)
            store(2 * p, s0).start()
            store(2 * p + 1, s0 + 1).start()
            return carry

        lax.fori_loop(0, nfull, pair, 0)

        for back in (2, 1):
            @pl.when(nfull >= back)
            def _():
                q = nfull - back
                s0 = 2 * lax.rem(q, 2)
                store(2 * q, s0).wait()
                store(2 * q + 1, s0 + 1).wait()

        @pl.when(lax.rem(nblk, 2) == 1)
        def _():
            b = nblk - 1
            fetch(b, 0).start()
            fetch(b, 0).wait()
            compute(0)
            store(b, 0).start()
            store(b, 0).wait()

    @pl.when(e == pl.num_programs(0) - 1)
    def _():
        _zero_rows(ybuf.at[0])

        def unused(j):
            start = pl.multiple_of((meta_ref[3, 0] + j) * blk_rows, blk_rows)
            return pltpu.make_async_copy(ybuf.at[0], ys_ref.at[pl.ds(start, blk_rows)],
                                         sem_out.at[0])

        for j in range(N_EXPERTS):
            @pl.when(meta_ref[3, 0] + j < n_blocks)
            def _():
                unused(j).start()
        for j in range(N_EXPERTS):
            @pl.when(meta_ref[3, 0] + j < n_blocks)
            def _():
                unused(j).wait()


def _expert_call(meta, xs, w_exp_in, w_exp_out):
    blk_rows = MOE_BLOCK * SUBLANES
    n_exp, d, h2 = w_exp_in.shape
    hid = w_exp_out.shape[1]
    n_blocks = xs.shape[0] // blk_rows
    return pl.pallas_call(
        functools.partial(_expert_kernel, n_blocks=n_blocks),
        grid_spec=pltpu.PrefetchScalarGridSpec(
            num_scalar_prefetch=1, grid=(n_exp,),
            in_specs=[pl.BlockSpec(memory_space=pl.ANY),
                      pl.BlockSpec((None, d, h2), lambda e, meta: (e, 0, 0)),
                      pl.BlockSpec((None, hid, d), lambda e, meta: (e, 0, 0))],
            out_specs=pl.BlockSpec(memory_space=pl.ANY),
            scratch_shapes=[pltpu.VMEM((d, h2), BF16), pltpu.VMEM((hid, d), BF16),
                            pltpu.VMEM((4, blk_rows, LANES), U32),
                            pltpu.VMEM((4, blk_rows, LANES), U32),
                            pltpu.SemaphoreType.DMA((4,)), pltpu.SemaphoreType.DMA((4,))]),
        out_shape=jax.ShapeDtypeStruct(xs.shape, U32),
        compiler_params=_params(("arbitrary",)),
        name="experts",
    )(meta, xs, w_exp_in, w_exp_out)


def _combine_kernel(dest_ref, w_ref, ys_ref, sh_ref, x1_ref, mod_ref, g_ref, o_ref, buf_ref, sem,
                    *, tc, final_norm):
    def issue(r, carry):
        for k in range(TOP_K):
            pltpu.make_async_copy(_token(ys_ref, dest_ref[k, r]), _token(buf_ref.at[k], r),
                                  sem).start(priority=k % 2)
        return carry

    lax.fori_loop(0, tc, issue, 0)
    for k in range(TOP_K):
        pltpu.make_async_copy(ys_ref.at[pl.ds(0, tc * SUBLANES)], buf_ref.at[k], sem).wait()

    acc_lo = acc_hi = None
    for k in range(TOP_K):
        wk = w_ref[:, k:k + 1]
        lo, hi = _unpack2(_load_rows(buf_ref.at[k]))
        acc_lo = wk * lo if acc_lo is None else acc_lo + wk * lo
        acc_hi = wk * hi if acc_hi is None else acc_hi + wk * hi
    routed = jnp.concatenate([acc_lo, acc_hi], axis=1)
    moe = routed + sh_ref[...].astype(F32)
    x2 = x1_ref[...] + mod_ref[5:6, :] * moe
    o_ref[...] = _rms(x2) * g_ref[...] if final_norm else x2


def _combine_call(dest, wsel_t, ys, shared, x1, mod3, g_final, seq, final_norm):
    t, d = x1.shape
    half = d // 2
    tc = min(256, seq)
    per_b = seq // tc
    return pl.pallas_call(
        functools.partial(_combine_kernel, tc=tc, final_norm=final_norm),
        grid=(t // tc,),
        in_specs=[pl.BlockSpec((TOP_K, tc), lambda i: (0, i), memory_space=pltpu.SMEM),
                  pl.BlockSpec((tc, TOP_K), lambda i: (i, 0)),
                  pl.BlockSpec(memory_space=pl.ANY),
                  pl.BlockSpec((tc, d), lambda i: (i, 0)),
                  pl.BlockSpec((tc, d), lambda i: (i, 0)),
                  pl.BlockSpec((None, 6, d), lambda i: (i // per_b, 0, 0)),
                  pl.BlockSpec((1, d), lambda i: (0, 0))],
        out_specs=pl.BlockSpec((tc, d), lambda i: (i, 0)),
        out_shape=jax.ShapeDtypeStruct((t, d), F32),
        scratch_shapes=[pltpu.VMEM((TOP_K, tc * SUBLANES, LANES), U32),
                        pltpu.SemaphoreType.DMA],
        compiler_params=_params(("arbitrary",)),
        name="combine_final",
    )(dest, wsel_t, ys, shared, x1, mod3, g_final.reshape(1, d))


def _moe_plan(counts):
    pcounts = (counts + MOE_BLOCK - 1) // MOE_BLOCK * MOE_BLOCK
    pends = jnp.cumsum(pcounts)
    pstarts = pends - pcounts
    return pstarts.astype(I32), pcounts.astype(I32), pends.astype(I32)


def kernel(x, c, positions, w_ada, b_ada, g_mix, w_in, w_conv, b_conv, w_rg_a, b_rg_a, w_rg_x, b_rg_x, lam, beta_ret, beta_lru, w_out, g_moe, w_router, b_router, w_exp_in, w_exp_out, w_sh_in, w_sh_out, g_final):
    batch, seq, d = x.shape
    t = batch * seq
    depth = w_ada.shape[0]
    xcur = x.reshape(t, d)
    cos, sin = _rope_call(positions)
    n_rows = (t * TOP_K + MOE_BLOCK - 1) // MOE_BLOCK * MOE_BLOCK + N_EXPERTS * MOE_BLOCK
    for l in range(depth):
        mod3 = _mod_call(c, w_ada[l], b_ada[l]).reshape(batch, 6, d)
        proj = _in_call(xcur, mod3, g_mix[l], w_in[l].astype(BF16), seq)
        ret = _ret_call(proj, cos, sin, beta_ret[l], batch, seq)
        lru = _lru_call(proj, w_conv[l], b_conv[l], w_rg_a[l], b_rg_a[l], w_rg_x[l], b_rg_x[l],
                        lam[l], beta_lru[l], batch, seq)
        x1, h2p = _out_call(ret, lru, xcur, mod3, g_moe[l], w_out[l].astype(BF16), seq)

        eidx, wsel, pos, cnt = _route_call(h2p, w_router[l], b_router[l])
        counts = cnt[:, 0].astype(I32)
        pstarts, pcounts, pends = _moe_plan(counts)
        dest = _dest_call(pstarts, eidx, pos)
        n_used = (pends[-1] // MOE_BLOCK).astype(I32)
        meta = jnp.stack([pstarts, counts, pcounts, jnp.full_like(counts, n_used)])

        xs, shared = _dispatch_call(meta, dest, h2p, w_sh_in[l].astype(BF16),
                                    w_sh_out[l].astype(BF16), n_rows)
        ys = _expert_call(meta, xs, w_exp_in[l], w_exp_out[l])
        xcur = _combine_call(dest, wsel.T, ys, shared, x1, mod3, g_final, seq,
                             final_norm=(l == depth - 1))
    return xcur.reshape(batch, seq, d)
```

```python
import functools

import numpy as np
import jax
import jax.numpy as jnp
from jax import lax
from jax.experimental import pallas as pl
from jax.experimental.pallas import tpu as pltpu

F32 = jnp.float32
BF16 = jnp.bfloat16
I32 = jnp.int32
U32 = jnp.uint32

RET_HEADS = 8
HEAD_DIM = 128
CHUNK = 128
ROPE_THETA = 10000.0
LRU_BLOCKS = 8
CONV_WIDTH = 4
LRU_C = 8.0
N_EXPERTS = 64
N_GROUPS = 8
GROUP_SIZE = N_EXPERTS // N_GROUPS
TOPK_GROUPS = 4
TOP_K = 8
ROUTED_SCALE = 2.5
EPS = 1e-6

LANES = 128
SUBLANES = 8
VMEM_LIMIT_BYTES = 56 * 1024 * 1024

MOE_BLOCK = 256
NEG_INF = float("-inf")


def _params(semantics, vmem=VMEM_LIMIT_BYTES):
    return pltpu.CompilerParams(dimension_semantics=semantics, vmem_limit_bytes=vmem)


def _silu(x):
    return x * jax.nn.sigmoid(x)


def _rms(x):
    return x * lax.rsqrt(jnp.mean(x * x, axis=-1, keepdims=True) + EPS)


def _pack2(lo, hi):
    return pltpu.pack_elementwise([lo, hi], packed_dtype=jnp.bfloat16)


def _unpack2(w):
    lo = pltpu.unpack_elementwise(w, index=0, packed_dtype=jnp.bfloat16, unpacked_dtype=F32)
    hi = pltpu.unpack_elementwise(w, index=1, packed_dtype=jnp.bfloat16, unpacked_dtype=F32)
    return lo, hi


def _load_rows(ref):
    rows = ref.shape[0] // SUBLANES
    return jnp.concatenate([ref[pl.ds(j, rows, stride=SUBLANES), :] for j in range(SUBLANES)],
                           axis=1)


def _store_rows(ref, v):
    rows = ref.shape[0] // SUBLANES
    for j in range(SUBLANES):
        ref[pl.ds(j, rows, stride=SUBLANES), :] = v[:, j * LANES:(j + 1) * LANES]


def _zero_rows(ref):
    rows = ref.shape[0] // SUBLANES
    z = jnp.zeros((rows, LANES), F32)
    w = _pack2(z, z)
    for j in range(SUBLANES):
        ref[pl.ds(j, rows, stride=SUBLANES), :] = w


def _token(ref, r):
    return ref.at[pl.ds(pl.multiple_of(r * SUBLANES, SUBLANES), SUBLANES)]


def _mod_kernel(c_ref, w_ref, b_ref, o_ref):
    cs = _silu(c_ref[...])
    o_ref[...] = jnp.dot(cs, w_ref[...], preferred_element_type=F32,
                         precision=lax.Precision.HIGHEST) + b_ref[...]


def _mod_call(c, w_ada, b_ada):
    b, d = c.shape
    n = w_ada.shape[1]
    tn = 1024
    return pl.pallas_call(
        _mod_kernel,
        grid=(n // tn,),
        in_specs=[pl.BlockSpec((b, d), lambda j: (0, 0)),
                  pl.BlockSpec((d, tn), lambda j: (0, j)),
                  pl.BlockSpec((1, tn), lambda j: (0, j))],
        out_specs=pl.BlockSpec((b, tn), lambda j: (0, j)),
        out_shape=jax.ShapeDtypeStruct((b, n), F32),
        compiler_params=_params(("parallel",)),
        name="adaln_mod",
    )(c, w_ada, b_ada.reshape(1, n))


def _in_kernel(x_ref, mod_ref, g_ref, w_ref, o_ref, h_ref):
    @pl.when(pl.program_id(1) == 0)
    def _():
        y = _rms(x_ref[...]) * g_ref[...]
        h_ref[...] = (y * (1.0 + mod_ref[1:2, :]) + mod_ref[0:1, :]).astype(BF16)

    o_ref[...] = jnp.dot(h_ref[...], w_ref[...], preferred_element_type=F32).astype(o_ref.dtype)


def _in_call(x2, mod3, g_mix, w_in_bf, seq):
    t, d = x2.shape
    n_out = w_in_bf.shape[1] // 1024
    tm = min(1024, seq)
    per_b = seq // tm
    return pl.pallas_call(
        _in_kernel,
        grid=(t // tm, n_out),
        in_specs=[pl.BlockSpec((tm, d), lambda i, j: (i, 0)),
                  pl.BlockSpec((None, 6, d), lambda i, j: (i // per_b, 0, 0)),
                  pl.BlockSpec((1, d), lambda i, j: (0, 0)),
                  pl.BlockSpec((d, 1024), lambda i, j: (0, j))],
        out_specs=pl.BlockSpec((None, tm, 1024), lambda i, j: (j, i, 0)),
        out_shape=jax.ShapeDtypeStruct((n_out, t, 1024), BF16),
        scratch_shapes=[pltpu.VMEM((tm, d), BF16)],
        compiler_params=_params(("parallel", "arbitrary")),
        name="in_proj",
    )(x2, mod3, g_mix.reshape(1, d), w_in_bf)


def _rope_kernel(pos_ref, invf_ref, sign_ref, cos_ref, sin_ref):
    ang = pos_ref[...].astype(F32) * invf_ref[...]
    cos_ref[...] = jnp.cos(ang)
    sin_ref[...] = jnp.sin(ang) * sign_ref[...]


def _rope_call(positions):
    t = positions.size
    tp = min(2048, t)
    half = HEAD_DIM // 2
    inv_freq = 1.0 / (ROPE_THETA ** (jnp.arange(0, HEAD_DIM, 2, dtype=F32) / HEAD_DIM))
    invf = jnp.concatenate([inv_freq, inv_freq]).reshape(1, HEAD_DIM)
    sign = jnp.concatenate([-jnp.ones((half,), F32), jnp.ones((half,), F32)]).reshape(1, HEAD_DIM)
    return pl.pallas_call(
        _rope_kernel,
        grid=(t // tp,),
        in_specs=[pl.BlockSpec((tp, 1), lambda i: (i, 0)),
                  pl.BlockSpec((1, HEAD_DIM), lambda i: (0, 0)),
                  pl.BlockSpec((1, HEAD_DIM), lambda i: (0, 0))],
        out_specs=[pl.BlockSpec((tp, HEAD_DIM), lambda i: (i, 0)),
                   pl.BlockSpec((tp, HEAD_DIM), lambda i: (i, 0))],
        out_shape=[jax.ShapeDtypeStruct((t, HEAD_DIM), F32)] * 2,
        compiler_params=_params(("parallel",)),
        name="rope_tables",
    )(positions.reshape(t, 1), invf, sign)


def _ret_tables():
    h = np.arange(RET_HEADS, dtype=np.float64)
    lg = np.log1p(-np.exp2(-5.0 - h))
    idx = np.arange(CHUNK, dtype=np.float64)
    diff = idx[:, None] - idx[None, :]
    intra = np.where(diff >= 0, np.exp(lg[:, None, None] * np.maximum(diff, 0.0)), 0.0)
    kdec = np.exp(lg[:, None] * (CHUNK - 1.0 - idx)[None, :])
    qdec = np.exp(lg[:, None] * (idx + 1.0)[None, :])
    cdec = np.exp(lg * CHUNK)
    bc = lambda a: np.ascontiguousarray(np.broadcast_to(a[:, :, None], (RET_HEADS, CHUNK, HEAD_DIM)))
    return (intra.astype(np.float32), bc(qdec).astype(np.float32), bc(kdec).astype(np.float32),
            [float(v) for v in cdec])


def _ret_kernel(q_ref, k_ref, v_ref, g_ref, cos_ref, sin_ref, intra_ref, qdec_ref, kdec_ref,
                beta_ref, o_ref, st_ref, *, cdec):
    @pl.when(pl.program_id(1) == 0)
    def _():
        st_ref[...] = jnp.zeros_like(st_ref)

    cos = cos_ref[...]
    sin = sin_ref[...]
    scale = HEAD_DIM ** -0.5
    nt = (((1,), (1,)), ((), ()))
    tn = (((0,), (0,)), ((), ()))
    outs = []
    for h in range(RET_HEADS):
        sl = slice(h * HEAD_DIM, (h + 1) * HEAD_DIM)
        q = q_ref[:, sl].astype(F32)
        k = k_ref[:, sl].astype(F32)
        v = v_ref[:, sl]
        qr = (q * cos + pltpu.roll(q, HEAD_DIM // 2, 1) * sin) * scale
        kr = k * cos + pltpu.roll(k, HEAD_DIM // 2, 1) * sin
        s = lax.dot_general(qr.astype(BF16), kr.astype(BF16), nt,
                            preferred_element_type=F32) * intra_ref[h]
        inner = jnp.dot(s.astype(BF16), v, preferred_element_type=F32)
        st = st_ref[h]
        cross = jnp.dot((qr * qdec_ref[h]).astype(BF16), st.astype(BF16),
                        preferred_element_type=F32)
        o = inner + cross
        kv = lax.dot_general((kr * kdec_ref[h]).astype(BF16), v, tn, preferred_element_type=F32)
        st_ref[h] = cdec[h] * st + kv
        oc = o - jnp.mean(o, axis=-1, keepdims=True)
        on = oc * lax.rsqrt(jnp.mean(oc * oc, axis=-1, keepdims=True) + EPS)
        outs.append(_silu(g_ref[:, sl].astype(F32)) * on)
    ret = jnp.concatenate(outs, axis=1)
    o_ref[...] = (_rms(ret) * beta_ref[...]).astype(o_ref.dtype)


def _ret_call(proj, cos, sin, beta_ret, batch, seq):
    t = batch * seq
    rw = RET_HEADS * HEAD_DIM
    nc = seq // CHUNK
    intra, qdec, kdec, cdec = _ret_tables()
    row = lambda which: pl.BlockSpec((None, CHUNK, rw), lambda b, n: (which, b * nc + n, 0))
    tab = lambda: pl.BlockSpec((RET_HEADS, CHUNK, HEAD_DIM), lambda b, n: (0, 0, 0))
    cs = lambda: pl.BlockSpec((CHUNK, HEAD_DIM), lambda b, n: (b * nc + n, 0))
    return pl.pallas_call(
        functools.partial(_ret_kernel, cdec=cdec),
        grid=(batch, nc),
        in_specs=[row(0), row(1), row(2), row(3), cs(), cs(), tab(), tab(), tab(),
                  pl.BlockSpec((1, rw), lambda b, n: (0, 0))],
        out_specs=pl.BlockSpec((CHUNK, rw), lambda b, n: (b * nc + n, 0)),
        out_shape=jax.ShapeDtypeStruct((t, rw), BF16),
        scratch_shapes=[pltpu.VMEM((RET_HEADS, HEAD_DIM, HEAD_DIM), F32)],
        compiler_params=_params(("parallel", "arbitrary")),
        name="retention",
    )(proj, proj, proj, proj, cos, sin, jnp.asarray(intra), jnp.asarray(qdec), jnp.asarray(kdec),
      beta_ret.reshape(1, rw))


def _lru_kernel(xr_ref, yr_ref, wc_ref, bc_ref, wg_ref, ba_ref, bx_ref, lam_ref, beta_ref,
                o_ref, xbuf_ref, h_ref, *, ts):
    @pl.when(pl.program_id(1) == 0)
    def _():
        xbuf_ref[0:SUBLANES, :] = jnp.zeros((SUBLANES, xbuf_ref.shape[1]), F32)
        h_ref[...] = jnp.zeros_like(h_ref)

    x = xr_ref[...].astype(F32)
    w = x.shape[1]
    xbuf_ref[SUBLANES:SUBLANES + ts, :] = x
    xc = bc_ref[...]
    for tap in range(CONV_WIDTH):
        back = CONV_WIDTH - 1 - tap
        xs = x if back == 0 else xbuf_ref[SUBLANES - back:SUBLANES - back + ts, :]
        xc = xc + xs * wc_ref[tap:tap + 1, :]
    xbuf_ref[0:SUBLANES, :] = x[ts - SUBLANES:ts, :]

    xcb = xc.astype(BF16)
    bd = w // LRU_BLOCKS
    rs, gs = [], []
    for g in range(LRU_BLOCKS):
        z = jnp.dot(xcb[:, g * bd:(g + 1) * bd], wg_ref[g], preferred_element_type=F32)
        rs.append(z[:, :bd])
        gs.append(z[:, bd:])
    r = jax.nn.sigmoid(jnp.concatenate(rs, axis=1) + ba_ref[...])
    i = jax.nn.sigmoid(jnp.concatenate(gs, axis=1) + bx_ref[...])
    log_a = -LRU_C * r * jax.nn.softplus(-lam_ref[...])
    a = jnp.exp(log_a)
    bt = jnp.sqrt(-jnp.tanh(log_a) * (a * a + 1.0)) * (i * xc)

    rows = lax.broadcasted_iota(I32, (ts, w), 0)
    acc_a, acc_b = a, bt
    sh = 1
    while sh < ts:
        prev_a = pltpu.roll(acc_a, sh, 0)
        prev_b = pltpu.roll(acc_b, sh, 0)
        m = rows >= sh
        acc_b = jnp.where(m, acc_a * prev_b + acc_b, acc_b)
        acc_a = jnp.where(m, acc_a * prev_a, acc_a)
        sh *= 2
    hs = acc_a * h_ref[...] + acc_b
    h_ref[...] = hs[ts - 1:ts, :]

    lru = hs * jax.nn.gelu(yr_ref[...].astype(F32), approximate=True)
    o_ref[...] = (_rms(lru) * beta_ref[...]).astype(o_ref.dtype)


def _lru_call(proj, w_conv, b_conv, w_rg_a, b_rg_a, w_rg_x, b_rg_x, lam, beta_lru, batch, seq):
    t = batch * seq
    w = w_conv.shape[1]
    ts = min(256, seq)
    nt = seq // ts
    wg = jnp.concatenate([w_rg_a, w_rg_x], axis=-1).astype(BF16)
    vec = lambda: pl.BlockSpec((1, w), lambda b, j: (0, 0))
    row = lambda which: pl.BlockSpec((None, ts, w), lambda b, j: (which, b * nt + j, 0))
    return pl.pallas_call(
        functools.partial(_lru_kernel, ts=ts),
        grid=(batch, nt),
        in_specs=[row(4), row(5),
                  pl.BlockSpec((CONV_WIDTH, w), lambda b, j: (0, 0)), vec(),
                  pl.BlockSpec(wg.shape, lambda b, j: (0, 0, 0)),
                  vec(), vec(), vec(), vec()],
        out_specs=pl.BlockSpec((ts, w), lambda b, j: (b * nt + j, 0)),
        out_shape=jax.ShapeDtypeStruct((t, w), BF16),
        scratch_shapes=[pltpu.VMEM((ts + SUBLANES, w), F32), pltpu.VMEM((1, w), F32)],
        compiler_params=_params(("parallel", "arbitrary")),
        name="rg_lru",
    )(proj, proj, w_conv, b_conv.reshape(1, w), wg, b_rg_a.reshape(1, w), b_rg_x.reshape(1, w),
      lam.reshape(1, w), beta_lru.reshape(1, w))


def _out_kernel(ret_ref, lru_ref, x_ref, mod_ref, g_ref, w_ref, x1_ref, h2_ref):
    rw = ret_ref.shape[1]
    mix = jnp.dot(ret_ref[...], w_ref[0:rw, :], preferred_element_type=F32)
    mix = mix + jnp.dot(lru_ref[...], w_ref[rw:, :], preferred_element_type=F32)
    x1 = x_ref[...] + mod_ref[2:3, :] * mix
    x1_ref[...] = x1
    h2 = _rms(x1) * g_ref[...] * (1.0 + mod_ref[4:5, :]) + mod_ref[3:4, :]
    half = h2.shape[1] // 2
    _store_rows(h2_ref, _pack2(h2[:, :half], h2[:, half:]))


def _out_call(ret, lru, x2, mod3, g_moe, w_out_bf, seq):
    t, d = x2.shape
    assert d // 2 == SUBLANES * LANES, "packed token rows are exactly one (8, 128) word tile"
    rw, lw = ret.shape[1], lru.shape[1]
    tm = min(512, seq)
    per_b = seq // tm
    return pl.pallas_call(
        _out_kernel,
        grid=(t // tm,),
        in_specs=[pl.BlockSpec((tm, rw), lambda i: (i, 0)),
                  pl.BlockSpec((tm, lw), lambda i: (i, 0)),
                  pl.BlockSpec((tm, d), lambda i: (i, 0)),
                  pl.BlockSpec((None, 6, d), lambda i: (i // per_b, 0, 0)),
                  pl.BlockSpec((1, d), lambda i: (0, 0)),
                  pl.BlockSpec((rw + lw, d), lambda i: (0, 0))],
        out_specs=[pl.BlockSpec((tm, d), lambda i: (i, 0)),
                   pl.BlockSpec((tm * SUBLANES, LANES), lambda i: (i, 0))],
        out_shape=[jax.ShapeDtypeStruct((t, d), F32),
                   jax.ShapeDtypeStruct((t * SUBLANES, LANES), U32)],
        compiler_params=_params(("parallel",)),
        name="out_proj",
    )(ret, lru, x2, mod3, g_moe.reshape(1, d), w_out_bf)


def _route_kernel(h_ref, wlo_ref, whi_ref, bias_ref, tri_ref, eidx_ref, wsel_ref, pos_ref,
                  cnt_ref, carry_ref):
    @pl.when(pl.program_id(0) == 0)
    def _():
        carry_ref[...] = jnp.zeros_like(carry_ref)

    tr = h_ref.shape[0] // SUBLANES
    lo, hi = _unpack2(_load_rows(h_ref))
    nt = (((1,), (1,)), ((), ()))
    logits = lax.dot_general(wlo_ref[...], lo.astype(BF16), nt, preferred_element_type=F32)
    logits = logits + lax.dot_general(whi_ref[...], hi.astype(BF16), nt,
                                      preferred_element_type=F32)
    scores = jax.nn.sigmoid(logits)
    biased = scores + bias_ref[:, 0:1]
    shape3 = (N_GROUPS, GROUP_SIZE, tr)
    s3 = scores.reshape(shape3)
    b3 = biased.reshape(shape3)
    member = lax.broadcasted_iota(I32, shape3, 1)
    group = lax.broadcasted_iota(I32, shape3, 0)
    expert = group * GROUP_SIZE + member

    m1 = jnp.max(b3, axis=1, keepdims=True)
    i1 = jnp.min(jnp.where(b3 == m1, member, GROUP_SIZE), axis=1, keepdims=True)
    m2 = jnp.max(jnp.where(member == i1, NEG_INF, b3), axis=1, keepdims=True)
    gscore = m1 + m2

    gid = lax.broadcasted_iota(I32, (N_GROUPS, 1, tr), 0)
    rank = jnp.zeros((N_GROUPS, 1, tr), I32)
    for j in range(N_GROUPS):
        gj = gscore[j:j + 1]
        ahead = (gj > gscore) | ((gj == gscore) & (gid > j))
        rank = rank + ahead.astype(I32)
    masked = jnp.where(rank < TOPK_GROUPS, b3, NEG_INF)

    idxs, vals = [], []
    picked = jnp.zeros(shape3, F32)
    for _ in range(TOP_K):
        m = jnp.max(jnp.max(masked, axis=0, keepdims=True), axis=1, keepdims=True)
        cand = jnp.where(masked == m, expert, N_EXPERTS)
        idx = jnp.min(jnp.min(cand, axis=0, keepdims=True), axis=1, keepdims=True)
        hit = expert == idx
        val = jnp.sum(jnp.sum(jnp.where(hit, s3, 0.0), axis=0, keepdims=True), axis=1, keepdims=True)
        masked = jnp.where(hit, NEG_INF, masked)
        picked = jnp.where(hit, 1.0, picked)
        idxs.append(idx)
        vals.append(val)
    total = vals[0]
    for v in vals[1:]:
        total = total + v

    before = jnp.dot(picked.reshape(N_EXPERTS, tr).astype(BF16), tri_ref[...],
                     preferred_element_type=F32)
    carry = carry_ref[...]
    posm = (before + carry[:, 0:1]).reshape(shape3)
    carry = carry + jnp.sum(picked.reshape(N_EXPERTS, tr), axis=1, keepdims=True)
    carry_ref[...] = carry
    cnt_ref[...] = carry

    for k in range(TOP_K):
        hit = expert == idxs[k]
        p = jnp.sum(jnp.sum(jnp.where(hit, posm, 0.0), axis=0, keepdims=True), axis=1, keepdims=True)
        eidx_ref[k:k + 1, :] = idxs[k].reshape(1, tr)
        wsel_ref[k:k + 1, :] = (vals[k] / total * ROUTED_SCALE).reshape(1, tr)
        pos_ref[k:k + 1, :] = p.reshape(1, tr).astype(I32)


def _route_call(h2p, w_router, b_router):
    t = h2p.shape[0] // SUBLANES
    half = SUBLANES * LANES
    tr = min(512, t)
    wt = w_router.T.astype(BF16)
    bias = jnp.broadcast_to(b_router.astype(F32)[:, None], (N_EXPERTS, LANES))
    tri = jnp.asarray(np.triu(np.ones((tr, tr), np.float32), k=1), dtype=BF16)
    kt = lambda dt: jax.ShapeDtypeStruct((TOP_K, t), dt)
    krow = lambda: pl.BlockSpec((TOP_K, tr), lambda i: (0, i))
    return pl.pallas_call(
        _route_kernel,
        grid=(t // tr,),
        in_specs=[pl.BlockSpec((tr * SUBLANES, LANES), lambda i: (i, 0)),
                  pl.BlockSpec((N_EXPERTS, half), lambda i: (0, 0)),
                  pl.BlockSpec((N_EXPERTS, half), lambda i: (0, 0)),
                  pl.BlockSpec((N_EXPERTS, LANES), lambda i: (0, 0)),
                  pl.BlockSpec((tr, tr), lambda i: (0, 0))],
        out_specs=[krow(), krow(), krow(),
                   pl.BlockSpec((N_EXPERTS, LANES), lambda i: (0, 0))],
        out_shape=[kt(I32), kt(F32), kt(I32), jax.ShapeDtypeStruct((N_EXPERTS, LANES), F32)],
        scratch_shapes=[pltpu.VMEM((N_EXPERTS, LANES), F32)],
        compiler_params=_params(("arbitrary",)),
        name="router",
    )(h2p, wt[:, :half], wt[:, half:], bias, tri)


def _dest_kernel(pst_ref, eidx_ref, pos_ref, o_ref):
    e = eidx_ref[...]
    acc = pos_ref[...]
    for j in range(N_EXPERTS):
        acc = acc + jnp.where(e == j, pst_ref[j], 0)
    o_ref[...] = acc


def _dest_call(pstarts, eidx, pos):
    k, t = eidx.shape
    tb = min(2048, t)
    blk = lambda: pl.BlockSpec((k, tb), lambda i, pst: (0, i))
    return pl.pallas_call(
        _dest_kernel,
        grid_spec=pltpu.PrefetchScalarGridSpec(
            num_scalar_prefetch=1, grid=(t // tb,),
            in_specs=[blk(), blk()], out_specs=blk()),
        out_shape=jax.ShapeDtypeStruct((k, t), I32),
        compiler_params=_params(("parallel",)),
        name="dest_rows",
    )(pstarts, eidx, pos)


def _swiglu_packed(x_ref, w1_ref, w2_ref):
    xw = _load_rows(x_ref)
    half = xw.shape[1]
    lo, hi = _unpack2(xw)
    gu = jnp.dot(lo.astype(BF16), w1_ref[0:half, :], preferred_element_type=F32)
    gu = gu + jnp.dot(hi.astype(BF16), w1_ref[half:, :], preferred_element_type=F32)
    hid = gu.shape[1] // 2
    act = _silu(gu[:, :hid]) * gu[:, hid:]
    return jnp.dot(act.astype(BF16), w2_ref[...], preferred_element_type=F32)


def _dispatch_kernel(meta_ref, dest_ref, h_ref, w1_ref, w2_ref, xs_ref, sh_ref, zero_ref,
                     sem_z, sem_r, *, td):

    @pl.when(pl.program_id(0) == 0)
    def _():
        _zero_rows(zero_ref)

        blk_rows = MOE_BLOCK * SUBLANES

        def block_copy(first_token):
            start = pl.multiple_of(first_token * SUBLANES, blk_rows)
            return pltpu.make_async_copy(zero_ref, xs_ref.at[pl.ds(start, blk_rows)], sem_z)

        def tail_copy(e):
            return block_copy(meta_ref[0, e] + meta_ref[2, e] - MOE_BLOCK)

        def unused_copy(j):
            return block_copy((meta_ref[3, 0] + j) * MOE_BLOCK)

        n_blocks = xs_ref.shape[0] // blk_rows
        for e in range(N_EXPERTS):
            @pl.when(meta_ref[2, e] > meta_ref[1, e])
            def _():
                tail_copy(e).start()

            @pl.when(meta_ref[3, 0] + e < n_blocks)
            def _():
                unused_copy(e).start()
        for e in range(N_EXPERTS):
            @pl.when(meta_ref[2, e] > meta_ref[1, e])
            def _():
                tail_copy(e).wait()

            @pl.when(meta_ref[3, 0] + e < n_blocks)
            def _():
                unused_copy(e).wait()

    def issue(r, carry):
        src = _token(h_ref, r)
        for k in range(TOP_K):
            pltpu.make_async_copy(src, _token(xs_ref, dest_ref[k, r]), sem_r).start(priority=k % 2)
        return carry

    lax.fori_loop(0, td, issue, 0)

    sh_ref[...] = _swiglu_packed(h_ref, w1_ref, w2_ref).astype(sh_ref.dtype)

    for k in range(TOP_K):
        pltpu.make_async_copy(h_ref, xs_ref.at[pl.ds(0, td * SUBLANES)], sem_r).wait()


def _dispatch_call(meta, dest, h2p, w_sh_in_bf, w_sh_out_bf, n_rows):
    t = h2p.shape[0] // SUBLANES
    d = 2 * SUBLANES * LANES
    td = min(256, t)
    return pl.pallas_call(
        functools.partial(_dispatch_kernel, td=td),
        grid=(t // td,),
        in_specs=[pl.BlockSpec(memory_space=pltpu.SMEM),
                  pl.BlockSpec((TOP_K, td), lambda i: (0, i), memory_space=pltpu.SMEM),
                  pl.BlockSpec((td * SUBLANES, LANES), lambda i: (i, 0)),
                  pl.BlockSpec(w_sh_in_bf.shape, lambda i: (0, 0)),
                  pl.BlockSpec(w_sh_out_bf.shape, lambda i: (0, 0))],
        out_specs=[pl.BlockSpec(memory_space=pl.ANY),
                   pl.BlockSpec((td, d), lambda i: (i, 0))],
        out_shape=[jax.ShapeDtypeStruct((n_rows * SUBLANES, LANES), U32),
                   jax.ShapeDtypeStruct((t, d), BF16)],
        scratch_shapes=[pltpu.VMEM((MOE_BLOCK * SUBLANES, LANES), U32),
                        pltpu.SemaphoreType.DMA, pltpu.SemaphoreType.DMA],
        compiler_params=_params(("arbitrary",)),
        name="dispatch_shared",
    )(meta, dest, h2p, w_sh_in_bf, w_sh_out_bf)


def _expert_kernel(meta_ref, xs_ref, w1_ref, w2_ref, ys_ref, w1b_ref, w2b_ref, xbuf, ybuf,
                   sem_in, sem_out, *, n_blocks):
    e = pl.program_id(0)
    blk_rows = MOE_BLOCK * SUBLANES
    first = meta_ref[0, e] // MOE_BLOCK
    nblk = meta_ref[2, e] // MOE_BLOCK
    nfull = nblk // 2

    def rows_of(b):
        return pl.ds(pl.multiple_of((first + b) * blk_rows, blk_rows), blk_rows)

    def fetch(b, s):
        return pltpu.make_async_copy(xs_ref.at[rows_of(b)], xbuf.at[s], sem_in.at[s])

    def store(b, s):
        return pltpu.make_async_copy(ybuf.at[s], ys_ref.at[rows_of(b)], sem_out.at[s])

    def compute(s):
        out = _swiglu_packed(xbuf.at[s], w1b_ref, w2b_ref)
        half = out.shape[1] // 2
        _store_rows(ybuf.at[s], _pack2(out[:, :half], out[:, half:]))

    odd = lax.rem(nblk, 2) == 1
    tail_slot = 4
    row_queue = 1

    @pl.when(nblk > 0)
    def _():
        @pl.when(nfull > 0)
        def _():
            fetch(0, 0).start(priority=row_queue)
            fetch(1, 1).start(priority=row_queue)

        @pl.when(odd)
        def _():
            fetch(nblk - 1, tail_slot).start(priority=row_queue)

        w1b_ref[...] = w1_ref[...].astype(BF16)
        w2b_ref[...] = w2_ref[...].astype(BF16)

        def pair(p, carry):
            s0 = 2 * lax.rem(p, 2)
            n0 = 2 - s0
            fetch(2 * p, s0).wait()
            fetch(2 * p + 1, s0 + 1).wait()

            @pl.when(p + 1 < nfull)
            def _():
                fetch(2 * p + 2, n0).start(priority=row_queue)
                fetch(2 * p + 3, n0 + 1).start(priority=row_queue)

            @pl.when(p >= 2)
            def _():
                store(2 * p - 4, s0).wait()
                store(2 * p - 3, s0 + 1).wait()

            compute(s0)
            compute(s0 + 1)
            store(2 * p, s0).start(priority=row_queue)
            store(2 * p + 1, s0 + 1).start(priority=row_queue)
            return carry

        lax.fori_loop(0, nfull, pair, 0)

        @pl.when(odd)
        def _():
            fetch(nblk - 1, tail_slot).wait()
            compute(tail_slot)
            store(nblk - 1, tail_slot).start(priority=row_queue)

        for back in (2, 1):
            @pl.when(nfull >= back)
            def _():
                q = nfull - back
                s0 = 2 * lax.rem(q, 2)
                store(2 * q, s0).wait()
                store(2 * q + 1, s0 + 1).wait()

        @pl.when(odd)
        def _():
            store(nblk - 1, tail_slot).wait()

    @pl.when(e == pl.num_programs(0) - 1)
    def _():
        _zero_rows(ybuf.at[0])

        def unused(j):
            start = pl.multiple_of((meta_ref[3, 0] + j) * blk_rows, blk_rows)
            return pltpu.make_async_copy(ybuf.at[0], ys_ref.at[pl.ds(start, blk_rows)],
                                         sem_out.at[0])

        for j in range(N_EXPERTS):
            @pl.when(meta_ref[3, 0] + j < n_blocks)
            def _():
                unused(j).start()
        for j in range(N_EXPERTS):
            @pl.when(meta_ref[3, 0] + j < n_blocks)
            def _():
                unused(j).wait()


def _expert_call(meta, xs, w_exp_in, w_exp_out):
    blk_rows = MOE_BLOCK * SUBLANES
    n_exp, d, h2 = w_exp_in.shape
    hid = w_exp_out.shape[1]
    n_blocks = xs.shape[0] // blk_rows
    return pl.pallas_call(
        functools.partial(_expert_kernel, n_blocks=n_blocks),
        grid_spec=pltpu.PrefetchScalarGridSpec(
            num_scalar_prefetch=1, grid=(n_exp,),
            in_specs=[pl.BlockSpec(memory_space=pl.ANY),
                      pl.BlockSpec((None, d, h2), lambda e, meta: (e, 0, 0)),
                      pl.BlockSpec((None, hid, d), lambda e, meta: (e, 0, 0))],
            out_specs=pl.BlockSpec(memory_space=pl.ANY),
            scratch_shapes=[pltpu.VMEM((d, h2), BF16), pltpu.VMEM((hid, d), BF16),
                            pltpu.VMEM((5, blk_rows, LANES), U32),
                            pltpu.VMEM((5, blk_rows, LANES), U32),
                            pltpu.SemaphoreType.DMA((5,)), pltpu.SemaphoreType.DMA((5,))]),
        out_shape=jax.ShapeDtypeStruct(xs.shape, U32),
        compiler_params=_params(("arbitrary",)),
        name="experts",
    )(meta, xs, w_exp_in, w_exp_out)


def _combine_kernel(dest_ref, w_ref, ys_ref, sh_ref, x1_ref, mod_ref, g_ref, o_ref, buf_ref, sem,
                    *, tc, final_norm):
    def issue(r, carry):
        for k in range(TOP_K):
            pltpu.make_async_copy(_token(ys_ref, dest_ref[k, r]), _token(buf_ref.at[k], r),
                                  sem).start(priority=k % 2)
        return carry

    lax.fori_loop(0, tc, issue, 0)
    for k in range(TOP_K):
        pltpu.make_async_copy(ys_ref.at[pl.ds(0, tc * SUBLANES)], buf_ref.at[k], sem).wait()

    acc_lo = acc_hi = None
    for k in range(TOP_K):
        wk = w_ref[:, k:k + 1]
        lo, hi = _unpack2(_load_rows(buf_ref.at[k]))
        acc_lo = wk * lo if acc_lo is None else acc_lo + wk * lo
        acc_hi = wk * hi if acc_hi is None else acc_hi + wk * hi
    routed = jnp.concatenate([acc_lo, acc_hi], axis=1)
    moe = routed + sh_ref[...].astype(F32)
    x2 = x1_ref[...] + mod_ref[5:6, :] * moe
    o_ref[...] = _rms(x2) * g_ref[...] if final_norm else x2


def _combine_call(dest, wsel_t, ys, shared, x1, mod3, g_final, seq, final_norm):
    t, d = x1.shape
    half = d // 2
    tc = min(256, seq)
    per_b = seq // tc
    return pl.pallas_call(
        functools.partial(_combine_kernel, tc=tc, final_norm=final_norm),
        grid=(t // tc,),
        in_specs=[pl.BlockSpec((TOP_K, tc), lambda i: (0, i), memory_space=pltpu.SMEM),
                  pl.BlockSpec((tc, TOP_K), lambda i: (i, 0)),
                  pl.BlockSpec(memory_space=pl.ANY),
                  pl.BlockSpec((tc, d), lambda i: (i, 0)),
                  pl.BlockSpec((tc, d), lambda i: (i, 0)),
                  pl.BlockSpec((None, 6, d), lambda i: (i // per_b, 0, 0)),
                  pl.BlockSpec((1, d), lambda i: (0, 0))],
        out_specs=pl.BlockSpec((tc, d), lambda i: (i, 0)),
        out_shape=jax.ShapeDtypeStruct((t, d), F32),
        scratch_shapes=[pltpu.VMEM((TOP_K, tc * SUBLANES, LANES), U32),
                        pltpu.SemaphoreType.DMA],
        compiler_params=_params(("arbitrary",)),
        name="combine_final",
    )(dest, wsel_t, ys, shared, x1, mod3, g_final.reshape(1, d))


def _moe_plan(counts):
    pcounts = (counts + MOE_BLOCK - 1) // MOE_BLOCK * MOE_BLOCK
    pends = jnp.cumsum(pcounts)
    pstarts = pends - pcounts
    return pstarts.astype(I32), pcounts.astype(I32), pends.astype(I32)


def kernel(x, c, positions, w_ada, b_ada, g_mix, w_in, w_conv, b_conv, w_rg_a, b_rg_a, w_rg_x, b_rg_x, lam, beta_ret, beta_lru, w_out, g_moe, w_router, b_router, w_exp_in, w_exp_out, w_sh_in, w_sh_out, g_final):
    batch, seq, d = x.shape
    t = batch * seq
    depth = w_ada.shape[0]
    xcur = x.reshape(t, d)
    cos, sin = _rope_call(positions)
    n_rows = (t * TOP_K + MOE_BLOCK - 1) // MOE_BLOCK * MOE_BLOCK + N_EXPERTS * MOE_BLOCK
    for l in range(depth):
        mod3 = _mod_call(c, w_ada[l], b_ada[l]).reshape(batch, 6, d)
        proj = _in_call(xcur, mod3, g_mix[l], w_in[l].astype(BF16), seq)
        ret = _ret_call(proj, cos, sin, beta_ret[l], batch, seq)
        lru = _lru_call(proj, w_conv[l], b_conv[l], w_rg_a[l], b_rg_a[l], w_rg_x[l], b_rg_x[l],
                        lam[l], beta_lru[l], batch, seq)
        x1, h2p = _out_call(ret, lru, xcur, mod3, g_moe[l], w_out[l].astype(BF16), seq)

        eidx, wsel, pos, cnt = _route_call(h2p, w_router[l], b_router[l])
        counts = cnt[:, 0].astype(I32)
        pstarts, pcounts, pends = _moe_plan(counts)
        dest = _dest_call(pstarts, eidx, pos)
        n_used = (pends[-1] // MOE_BLOCK).astype(I32)
        meta = jnp.stack([pstarts, counts, pcounts, jnp.full_like(counts, n_used)])

        xs, shared = _dispatch_call(meta, dest, h2p, w_sh_in[l].astype(BF16),
                                    w_sh_out[l].astype(BF16), n_rows)
        ys = _expert_call(meta, xs, w_exp_in[l], w_exp_out[l])
        xcur = _combine_call(dest, wsel.T, ys, shared, x1, mod3, g_final, seq,
                             final_norm=(l == depth - 1))
    return xcur.reshape(batch, seq, d)
```

```python
import functools

import numpy as np
import jax
import jax.numpy as jnp
from jax import lax
from jax.experimental import pallas as pl
from jax.experimental.pallas import tpu as pltpu

F32 = jnp.float32
BF16 = jnp.bfloat16
I32 = jnp.int32
U32 = jnp.uint32

RET_HEADS = 8
HEAD_DIM = 128
CHUNK = 128
ROPE_THETA = 10000.0
LRU_BLOCKS = 8
CONV_WIDTH = 4
LRU_C = 8.0
N_EXPERTS = 64
N_GROUPS = 8
GROUP_SIZE = N_EXPERTS // N_GROUPS
TOPK_GROUPS = 4
TOP_K = 8
ROUTED_SCALE = 2.5
EPS = 1e-6

LANES = 128
SUBLANES = 8
VMEM_LIMIT_BYTES = 56 * 1024 * 1024

MOE_BLOCK = 256
NEG_INF = float("-inf")


def _params(semantics, vmem=VMEM_LIMIT_BYTES):
    return pltpu.CompilerParams(dimension_semantics=semantics, vmem_limit_bytes=vmem)


def _silu(x):
    return x * jax.nn.sigmoid(x)


def _rms(x):
    return x * lax.rsqrt(jnp.mean(x * x, axis=-1, keepdims=True) + EPS)


def _pack2(lo, hi):
    return pltpu.pack_elementwise([lo, hi], packed_dtype=jnp.bfloat16)


def _unpack2(w):
    lo = pltpu.unpack_elementwise(w, index=0, packed_dtype=jnp.bfloat16, unpacked_dtype=F32)
    hi = pltpu.unpack_elementwise(w, index=1, packed_dtype=jnp.bfloat16, unpacked_dtype=F32)
    return lo, hi


def _load_rows(ref):
    rows = ref.shape[0] // SUBLANES
    return jnp.concatenate([ref[pl.ds(j, rows, stride=SUBLANES), :] for j in range(SUBLANES)],
                           axis=1)


def _store_rows(ref, v):
    rows = ref.shape[0] // SUBLANES
    for j in range(SUBLANES):
        ref[pl.ds(j, rows, stride=SUBLANES), :] = v[:, j * LANES:(j + 1) * LANES]


def _zero_rows(ref):
    rows = ref.shape[0] // SUBLANES
    z = jnp.zeros((rows, LANES), F32)
    w = _pack2(z, z)
    for j in range(SUBLANES):
        ref[pl.ds(j, rows, stride=SUBLANES), :] = w


def _token(ref, r):
    return ref.at[pl.ds(pl.multiple_of(r * SUBLANES, SUBLANES), SUBLANES)]


def _mod_kernel(c_ref, w_ref, b_ref, o_ref):
    cs = _silu(c_ref[...])
    o_ref[...] = jnp.dot(cs, w_ref[...], preferred_element_type=F32,
                         precision=lax.Precision.HIGHEST) + b_ref[...]


def _mod_call(c, w_ada, b_ada):
    b, d = c.shape
    n = w_ada.shape[1]
    tn = 1024
    return pl.pallas_call(
        _mod_kernel,
        grid=(n // tn,),
        in_specs=[pl.BlockSpec((b, d), lambda j: (0, 0)),
                  pl.BlockSpec((d, tn), lambda j: (0, j)),
                  pl.BlockSpec((1, tn), lambda j: (0, j))],
        out_specs=pl.BlockSpec((b, tn), lambda j: (0, j)),
        out_shape=jax.ShapeDtypeStruct((b, n), F32),
        compiler_params=_params(("parallel",)),
        name="adaln_mod",
    )(c, w_ada, b_ada.reshape(1, n))


def _in_kernel(x_ref, mod_ref, g_ref, w_ref, o_ref, h_ref):
    @pl.when(pl.program_id(1) == 0)
    def _():
        y = _rms(x_ref[...]) * g_ref[...]
        h_ref[...] = (y * (1.0 + mod_ref[1:2, :]) + mod_ref[0:1, :]).astype(BF16)

    o_ref[...] = jnp.dot(h_ref[...], w_ref[...], preferred_element_type=F32).astype(o_ref.dtype)


def _in_call(x2, mod3, g_mix, w_in_bf, seq):
    t, d = x2.shape
    n_out = w_in_bf.shape[1] // 1024
    tm = min(1024, seq)
    per_b = seq // tm
    return pl.pallas_call(
        _in_kernel,
        grid=(t // tm, n_out),
        in_specs=[pl.BlockSpec((tm, d), lambda i, j: (i, 0)),
                  pl.BlockSpec((None, 6, d), lambda i, j: (i // per_b, 0, 0)),
                  pl.BlockSpec((1, d), lambda i, j: (0, 0)),
                  pl.BlockSpec((d, 1024), lambda i, j: (0, j))],
        out_specs=pl.BlockSpec((None, tm, 1024), lambda i, j: (j, i, 0)),
        out_shape=jax.ShapeDtypeStruct((n_out, t, 1024), BF16),
        scratch_shapes=[pltpu.VMEM((tm, d), BF16)],
        compiler_params=_params(("parallel", "arbitrary")),
        name="in_proj",
    )(x2, mod3, g_mix.reshape(1, d), w_in_bf)


def _rope_kernel(pos_ref, invf_ref, sign_ref, cos_ref, sin_ref):
    ang = pos_ref[...].astype(F32) * invf_ref[...]
    cos_ref[...] = jnp.cos(ang)
    sin_ref[...] = jnp.sin(ang) * sign_ref[...]


def _rope_call(positions):
    t = positions.size
    tp = min(2048, t)
    half = HEAD_DIM // 2
    inv_freq = 1.0 / (ROPE_THETA ** (jnp.arange(0, HEAD_DIM, 2, dtype=F32) / HEAD_DIM))
    invf = jnp.concatenate([inv_freq, inv_freq]).reshape(1, HEAD_DIM)
    sign = jnp.concatenate([-jnp.ones((half,), F32), jnp.ones((half,), F32)]).reshape(1, HEAD_DIM)
    return pl.pallas_call(
        _rope_kernel,
        grid=(t // tp,),
        in_specs=[pl.BlockSpec((tp, 1), lambda i: (i, 0)),
                  pl.BlockSpec((1, HEAD_DIM), lambda i: (0, 0)),
                  pl.BlockSpec((1, HEAD_DIM), lambda i: (0, 0))],
        out_specs=[pl.BlockSpec((tp, HEAD_DIM), lambda i: (i, 0)),
                   pl.BlockSpec((tp, HEAD_DIM), lambda i: (i, 0))],
        out_shape=[jax.ShapeDtypeStruct((t, HEAD_DIM), F32)] * 2,
        compiler_params=_params(("parallel",)),
        name="rope_tables",
    )(positions.reshape(t, 1), invf, sign)


def _ret_tables():
    h = np.arange(RET_HEADS, dtype=np.float64)
    lg = np.log1p(-np.exp2(-5.0 - h))
    idx = np.arange(CHUNK, dtype=np.float64)
    diff = idx[:, None] - idx[None, :]
    intra = np.where(diff >= 0, np.exp(lg[:, None, None] * np.maximum(diff, 0.0)), 0.0)
    kdec = np.exp(lg[:, None] * (CHUNK - 1.0 - idx)[None, :])
    qdec = np.exp(lg[:, None] * (idx + 1.0)[None, :])
    cdec = np.exp(lg * CHUNK)
    bc = lambda a: np.ascontiguousarray(np.broadcast_to(a[:, :, None], (RET_HEADS, CHUNK, HEAD_DIM)))
    return (intra.astype(np.float32), bc(qdec).astype(np.float32), bc(kdec).astype(np.float32),
            [float(v) for v in cdec])


def _ret_kernel(q_ref, k_ref, v_ref, g_ref, cos_ref, sin_ref, intra_ref, qdec_ref, kdec_ref,
                beta_ref, o_ref, st_ref, *, cdec):
    @pl.when(pl.program_id(1) == 0)
    def _():
        st_ref[...] = jnp.zeros_like(st_ref)

    cos = cos_ref[...]
    sin = sin_ref[...]
    scale = HEAD_DIM ** -0.5
    nt = (((1,), (1,)), ((), ()))
    tn = (((0,), (0,)), ((), ()))
    outs = []
    for h in range(RET_HEADS):
        sl = slice(h * HEAD_DIM, (h + 1) * HEAD_DIM)
        q = q_ref[:, sl].astype(F32)
        k = k_ref[:, sl].astype(F32)
        v = v_ref[:, sl]
        qr = (q * cos + pltpu.roll(q, HEAD_DIM // 2, 1) * sin) * scale
        kr = k * cos + pltpu.roll(k, HEAD_DIM // 2, 1) * sin
        s = lax.dot_general(qr.astype(BF16), kr.astype(BF16), nt,
                            preferred_element_type=F32) * intra_ref[h]
        inner = jnp.dot(s.astype(BF16), v, preferred_element_type=F32)
        st = st_ref[h]
        cross = jnp.dot((qr * qdec_ref[h]).astype(BF16), st.astype(BF16),
                        preferred_element_type=F32)
        o = inner + cross
        kv = lax.dot_general((kr * kdec_ref[h]).astype(BF16), v, tn, preferred_element_type=F32)
        st_ref[h] = cdec[h] * st + kv
        oc = o - jnp.mean(o, axis=-1, keepdims=True)
        on = oc * lax.rsqrt(jnp.mean(oc * oc, axis=-1, keepdims=True) + EPS)
        outs.append(_silu(g_ref[:, sl].astype(F32)) * on)
    ret = jnp.concatenate(outs, axis=1)
    o_ref[...] = (_rms(ret) * beta_ref[...]).astype(o_ref.dtype)


def _ret_call(proj, cos, sin, beta_ret, batch, seq):
    t = batch * seq
    rw = RET_HEADS * HEAD_DIM
    nc = seq // CHUNK
    intra, qdec, kdec, cdec = _ret_tables()
    row = lambda which: pl.BlockSpec((None, CHUNK, rw), lambda b, n: (which, b * nc + n, 0))
    tab = lambda: pl.BlockSpec((RET_HEADS, CHUNK, HEAD_DIM), lambda b, n: (0, 0, 0))
    cs = lambda: pl.BlockSpec((CHUNK, HEAD_DIM), lambda b, n: (b * nc + n, 0))
    return pl.pallas_call(
        functools.partial(_ret_kernel, cdec=cdec),
        grid=(batch, nc),
        in_specs=[row(0), row(1), row(2), row(3), cs(), cs(), tab(), tab(), tab(),
                  pl.BlockSpec((1, rw), lambda b, n: (0, 0))],
        out_specs=pl.BlockSpec((CHUNK, rw), lambda b, n: (b * nc + n, 0)),
        out_shape=jax.ShapeDtypeStruct((t, rw), BF16),
        scratch_shapes=[pltpu.VMEM((RET_HEADS, HEAD_DIM, HEAD_DIM), F32)],
        compiler_params=_params(("parallel", "arbitrary")),
        name="retention",
    )(proj, proj, proj, proj, cos, sin, jnp.asarray(intra), jnp.asarray(qdec), jnp.asarray(kdec),
      beta_ret.reshape(1, rw))


def _lru_kernel(xr_ref, yr_ref, wc_ref, bc_ref, wg_ref, ba_ref, bx_ref, lam_ref, beta_ref,
                o_ref, xbuf_ref, h_ref, *, ts):
    @pl.when(pl.program_id(1) == 0)
    def _():
        xbuf_ref[0:SUBLANES, :] = jnp.zeros((SUBLANES, xbuf_ref.shape[1]), F32)
        h_ref[...] = jnp.zeros_like(h_ref)

    x = xr_ref[...].astype(F32)
    w = x.shape[1]
    xbuf_ref[SUBLANES:SUBLANES + ts, :] = x
    xc = bc_ref[...]
    for tap in range(CONV_WIDTH):
        back = CONV_WIDTH - 1 - tap
        xs = x if back == 0 else xbuf_ref[SUBLANES - back:SUBLANES - back + ts, :]
        xc = xc + xs * wc_ref[tap:tap + 1, :]
    xbuf_ref[0:SUBLANES, :] = x[ts - SUBLANES:ts, :]

    xcb = xc.astype(BF16)
    bd = w // LRU_BLOCKS
    rs, gs = [], []
    for g in range(LRU_BLOCKS):
        z = jnp.dot(xcb[:, g * bd:(g + 1) * bd], wg_ref[g], preferred_element_type=F32)
        rs.append(z[:, :bd])
        gs.append(z[:, bd:])
    r = jax.nn.sigmoid(jnp.concatenate(rs, axis=1) + ba_ref[...])
    i = jax.nn.sigmoid(jnp.concatenate(gs, axis=1) + bx_ref[...])
    log_a = -LRU_C * r * jax.nn.softplus(-lam_ref[...])
    a = jnp.exp(log_a)
    bt = jnp.sqrt(-jnp.tanh(log_a) * (a * a + 1.0)) * (i * xc)

    rows = lax.broadcasted_iota(I32, (ts, w), 0)
    acc_a, acc_b = a, bt
    sh = 1
    while sh < ts:
        prev_a = pltpu.roll(acc_a, sh, 0)
        prev_b = pltpu.roll(acc_b, sh, 0)
        m = rows >= sh
        acc_b = jnp.where(m, acc_a * prev_b + acc_b, acc_b)
        acc_a = jnp.where(m, acc_a * prev_a, acc_a)
        sh *= 2
    hs = acc_a * h_ref[...] + acc_b
    h_ref[...] = hs[ts - 1:ts, :]

    lru = hs * jax.nn.gelu(yr_ref[...].astype(F32), approximate=True)
    o_ref[...] = (_rms(lru) * beta_ref[...]).astype(o_ref.dtype)


def _lru_call(proj, w_conv, b_conv, w_rg_a, b_rg_a, w_rg_x, b_rg_x, lam, beta_lru, batch, seq):
    t = batch * seq
    w = w_conv.shape[1]
    ts = min(256, seq)
    nt = seq // ts
    wg = jnp.concatenate([w_rg_a, w_rg_x], axis=-1).astype(BF16)
    vec = lambda: pl.BlockSpec((1, w), lambda b, j: (0, 0))
    row = lambda which: pl.BlockSpec((None, ts, w), lambda b, j: (which, b * nt + j, 0))
    return pl.pallas_call(
        functools.partial(_lru_kernel, ts=ts),
        grid=(batch, nt),
        in_specs=[row(4), row(5),
                  pl.BlockSpec((CONV_WIDTH, w), lambda b, j: (0, 0)), vec(),
                  pl.BlockSpec(wg.shape, lambda b, j: (0, 0, 0)),
                  vec(), vec(), vec(), vec()],
        out_specs=pl.BlockSpec((ts, w), lambda b, j: (b * nt + j, 0)),
        out_shape=jax.ShapeDtypeStruct((t, w), BF16),
        scratch_shapes=[pltpu.VMEM((ts + SUBLANES, w), F32), pltpu.VMEM((1, w), F32)],
        compiler_params=_params(("parallel", "arbitrary")),
        name="rg_lru",
    )(proj, proj, w_conv, b_conv.reshape(1, w), wg, b_rg_a.reshape(1, w), b_rg_x.reshape(1, w),
      lam.reshape(1, w), beta_lru.reshape(1, w))


def _out_kernel(ret_ref, lru_ref, x_ref, mod_ref, g_ref, w_ref, x1_ref, h2_ref):
    rw = ret_ref.shape[1]
    mix = jnp.dot(ret_ref[...], w_ref[0:rw, :], preferred_element_type=F32)
    mix = mix + jnp.dot(lru_ref[...], w_ref[rw:, :], preferred_element_type=F32)
    x1 = x_ref[...] + mod_ref[2:3, :] * mix
    x1_ref[...] = x1
    h2 = _rms(x1) * g_ref[...] * (1.0 + mod_ref[4:5, :]) + mod_ref[3:4, :]
    half = h2.shape[1] // 2
    _store_rows(h2_ref, _pack2(h2[:, :half], h2[:, half:]))


def _out_call(ret, lru, x2, mod3, g_moe, w_out_bf, seq):
    t, d = x2.shape
    assert d // 2 == SUBLANES * LANES, "packed token rows are exactly one (8, 128) word tile"
    rw, lw = ret.shape[1], lru.shape[1]
    tm = min(512, seq)
    per_b = seq // tm
    return pl.pallas_call(
        _out_kernel,
        grid=(t // tm,),
        in_specs=[pl.BlockSpec((tm, rw), lambda i: (i, 0)),
                  pl.BlockSpec((tm, lw), lambda i: (i, 0)),
                  pl.BlockSpec((tm, d), lambda i: (i, 0)),
                  pl.BlockSpec((None, 6, d), lambda i: (i // per_b, 0, 0)),
                  pl.BlockSpec((1, d), lambda i: (0, 0)),
                  pl.BlockSpec((rw + lw, d), lambda i: (0, 0))],
        out_specs=[pl.BlockSpec((tm, d), lambda i: (i, 0)),
                   pl.BlockSpec((tm * SUBLANES, LANES), lambda i: (i, 0))],
        out_shape=[jax.ShapeDtypeStruct((t, d), F32),
                   jax.ShapeDtypeStruct((t * SUBLANES, LANES), U32)],
        compiler_params=_params(("parallel",)),
        name="out_proj",
    )(ret, lru, x2, mod3, g_moe.reshape(1, d), w_out_bf)


def _route_kernel(h_ref, wlo_ref, whi_ref, bias_ref, tri_ref, eidx_ref, wsel_ref, pos_ref,
                  cnt_ref, carry_ref):
    @pl.when(pl.program_id(0) == 0)
    def _():
        carry_ref[...] = jnp.zeros_like(carry_ref)

    tr = h_ref.shape[0] // SUBLANES
    lo, hi = _unpack2(_load_rows(h_ref))
    nt = (((1,), (1,)), ((), ()))
    logits = lax.dot_general(wlo_ref[...], lo.astype(BF16), nt, preferred_element_type=F32)
    logits = logits + lax.dot_general(whi_ref[...], hi.astype(BF16), nt,
                                      preferred_element_type=F32)
    scores = jax.nn.sigmoid(logits)
    biased = scores + bias_ref[:, 0:1]
    shape3 = (N_GROUPS, GROUP_SIZE, tr)
    s3 = scores.reshape(shape3)
    b3 = biased.reshape(shape3)
    member = lax.broadcasted_iota(I32, shape3, 1)
    group = lax.broadcasted_iota(I32, shape3, 0)
    expert = group * GROUP_SIZE + member

    m1 = jnp.max(b3, axis=1, keepdims=True)
    i1 = jnp.min(jnp.where(b3 == m1, member, GROUP_SIZE), axis=1, keepdims=True)
    m2 = jnp.max(jnp.where(member == i1, NEG_INF, b3), axis=1, keepdims=True)
    gscore = m1 + m2

    gid = lax.broadcasted_iota(I32, (N_GROUPS, 1, tr), 0)
    rank = jnp.zeros((N_GROUPS, 1, tr), I32)
    for j in range(N_GROUPS):
        gj = gscore[j:j + 1]
        ahead = (gj > gscore) | ((gj == gscore) & (gid > j))
        rank = rank + ahead.astype(I32)
    masked = jnp.where(rank < TOPK_GROUPS, b3, NEG_INF)

    idxs, vals = [], []
    picked = jnp.zeros(shape3, F32)
    for _ in range(TOP_K):
        m = jnp.max(jnp.max(masked, axis=0, keepdims=True), axis=1, keepdims=True)
        cand = jnp.where(masked == m, expert, N_EXPERTS)
        idx = jnp.min(jnp.min(cand, axis=0, keepdims=True), axis=1, keepdims=True)
        hit = expert == idx
        val = jnp.sum(jnp.sum(jnp.where(hit, s3, 0.0), axis=0, keepdims=True), axis=1, keepdims=True)
        masked = jnp.where(hit, NEG_INF, masked)
        picked = jnp.where(hit, 1.0, picked)
        idxs.append(idx)
        vals.append(val)
    total = vals[0]
    for v in vals[1:]:
        total = total + v

    before = jnp.dot(picked.reshape(N_EXPERTS, tr).astype(BF16), tri_ref[...],
                     preferred_element_type=F32)
    carry = carry_ref[...]
    posm = (before + carry[:, 0:1]).reshape(shape3)
    carry = carry + jnp.sum(picked.reshape(N_EXPERTS, tr), axis=1, keepdims=True)
    carry_ref[...] = carry
    cnt_ref[...] = carry

    for k in range(TOP_K):
        hit = expert == idxs[k]
        p = jnp.sum(jnp.sum(jnp.where(hit, posm, 0.0), axis=0, keepdims=True), axis=1, keepdims=True)
        eidx_ref[k:k + 1, :] = idxs[k].reshape(1, tr)
        wsel_ref[k:k + 1, :] = (vals[k] / total * ROUTED_SCALE).reshape(1, tr)
        pos_ref[k:k + 1, :] = p.reshape(1, tr).astype(I32)


def _route_call(h2p, w_router, b_router):
    t = h2p.shape[0] // SUBLANES
    half = SUBLANES * LANES
    tr = min(512, t)
    wt = w_router.T.astype(BF16)
    bias = jnp.broadcast_to(b_router.astype(F32)[:, None], (N_EXPERTS, LANES))
    tri = jnp.asarray(np.triu(np.ones((tr, tr), np.float32), k=1), dtype=BF16)
    kt = lambda dt: jax.ShapeDtypeStruct((TOP_K, t), dt)
    krow = lambda: pl.BlockSpec((TOP_K, tr), lambda i: (0, i))
    return pl.pallas_call(
        _route_kernel,
        grid=(t // tr,),
        in_specs=[pl.BlockSpec((tr * SUBLANES, LANES), lambda i: (i, 0)),
                  pl.BlockSpec((N_EXPERTS, half), lambda i: (0, 0)),
                  pl.BlockSpec((N_EXPERTS, half), lambda i: (0, 0)),
                  pl.BlockSpec((N_EXPERTS, LANES), lambda i: (0, 0)),
                  pl.BlockSpec((tr, tr), lambda i: (0, 0))],
        out_specs=[krow(), krow(), krow(),
                   pl.BlockSpec((N_EXPERTS, LANES), lambda i: (0, 0))],
        out_shape=[kt(I32), kt(F32), kt(I32), jax.ShapeDtypeStruct((N_EXPERTS, LANES), F32)],
        scratch_shapes=[pltpu.VMEM((N_EXPERTS, LANES), F32)],
        compiler_params=_params(("arbitrary",)),
        name="router",
    )(h2p, wt[:, :half], wt[:, half:], bias, tri)


def _dest_kernel(pst_ref, eidx_ref, pos_ref, o_ref):
    e = eidx_ref[...]
    acc = pos_ref[...]
    for j in range(N_EXPERTS):
        acc = acc + jnp.where(e == j, pst_ref[j], 0)
    o_ref[...] = acc


def _dest_call(pstarts, eidx, pos):
    k, t = eidx.shape
    tb = min(2048, t)
    blk = lambda: pl.BlockSpec((k, tb), lambda i, pst: (0, i))
    return pl.pallas_call(
        _dest_kernel,
        grid_spec=pltpu.PrefetchScalarGridSpec(
            num_scalar_prefetch=1, grid=(t // tb,),
            in_specs=[blk(), blk()], out_specs=blk()),
        out_shape=jax.ShapeDtypeStruct((k, t), I32),
        compiler_params=_params(("parallel",)),
        name="dest_rows",
    )(pstarts, eidx, pos)


def _swiglu_packed(x_ref, w1_ref, w2_ref):
    xw = _load_rows(x_ref)
    half = xw.shape[1]
    lo, hi = _unpack2(xw)
    gu = jnp.dot(lo.astype(BF16), w1_ref[0:half, :], preferred_element_type=F32)
    gu = gu + jnp.dot(hi.astype(BF16), w1_ref[half:, :], preferred_element_type=F32)
    hid = gu.shape[1] // 2
    act = _silu(gu[:, :hid]) * gu[:, hid:]
    return jnp.dot(act.astype(BF16), w2_ref[...], preferred_element_type=F32)


def _dispatch_kernel(meta_ref, dest_ref, h_ref, w1_ref, w2_ref, xs_ref, sh_ref, zero_ref,
                     sem_z, sem_r, *, td):

    @pl.when(pl.program_id(0) == 0)
    def _():
        _zero_rows(zero_ref)

        blk_rows = MOE_BLOCK * SUBLANES

        def block_copy(first_token):
            start = pl.multiple_of(first_token * SUBLANES, blk_rows)
            return pltpu.make_async_copy(zero_ref, xs_ref.at[pl.ds(start, blk_rows)], sem_z)

        def tail_copy(e):
            return block_copy(meta_ref[0, e] + meta_ref[2, e] - MOE_BLOCK)

        def unused_copy(j):
            return block_copy((meta_ref[3, 0] + j) * MOE_BLOCK)

        n_blocks = xs_ref.shape[0] // blk_rows
        for e in range(N_EXPERTS):
            @pl.when(meta_ref[2, e] > meta_ref[1, e])
            def _():
                tail_copy(e).start()

            @pl.when(meta_ref[3, 0] + e < n_blocks)
            def _():
                unused_copy(e).start()
        for e in range(N_EXPERTS):
            @pl.when(meta_ref[2, e] > meta_ref[1, e])
            def _():
                tail_copy(e).wait()

            @pl.when(meta_ref[3, 0] + e < n_blocks)
            def _():
                unused_copy(e).wait()

    def issue(r, carry):
        src = _token(h_ref, r)
        for k in range(TOP_K):
            pltpu.make_async_copy(src, _token(xs_ref, dest_ref[k, r]), sem_r).start(priority=k % 2)
        return carry

    lax.fori_loop(0, td, issue, 0)

    sh_ref[...] = _swiglu_packed(h_ref, w1_ref, w2_ref).astype(sh_ref.dtype)

    for k in range(TOP_K):
        pltpu.make_async_copy(h_ref, xs_ref.at[pl.ds(0, td * SUBLANES)], sem_r).wait()


def _dispatch_call(meta, dest, h2p, w_sh_in_bf, w_sh_out_bf, n_rows):
    t = h2p.shape[0] // SUBLANES
    d = 2 * SUBLANES * LANES
    td = min(256, t)
    return pl.pallas_call(
        functools.partial(_dispatch_kernel, td=td),
        grid=(t // td,),
        in_specs=[pl.BlockSpec(memory_space=pltpu.SMEM),
                  pl.BlockSpec((TOP_K, td), lambda i: (0, i), memory_space=pltpu.SMEM),
                  pl.BlockSpec((td * SUBLANES, LANES), lambda i: (i, 0)),
                  pl.BlockSpec(w_sh_in_bf.shape, lambda i: (0, 0)),
                  pl.BlockSpec(w_sh_out_bf.shape, lambda i: (0, 0))],
        out_specs=[pl.BlockSpec(memory_space=pl.ANY),
                   pl.BlockSpec((td, d), lambda i: (i, 0))],
        out_shape=[jax.ShapeDtypeStruct((n_rows * SUBLANES, LANES), U32),
                   jax.ShapeDtypeStruct((t, d), BF16)],
        scratch_shapes=[pltpu.VMEM((MOE_BLOCK * SUBLANES, LANES), U32),
                        pltpu.SemaphoreType.DMA, pltpu.SemaphoreType.DMA],
        compiler_params=_params(("arbitrary",)),
        name="dispatch_shared",
    )(meta, dest, h2p, w_sh_in_bf, w_sh_out_bf)


def _expert_kernel(meta_ref, xs_ref, w1_hbm, w2_hbm, ys_ref, w1f, w2f, w1b_ref, w2b_ref, xbuf, ybuf,
                   sem_w, sem_in, sem_out, *, n_blocks):
    e = pl.program_id(0)
    n_exp = pl.num_programs(0)
    blk_rows = MOE_BLOCK * SUBLANES
    wslot = lax.rem(e, 2)
    low = 1

    def weights(ex, s):
        return (pltpu.make_async_copy(w1_hbm.at[ex], w1f.at[s], sem_w.at[0, s]),
                pltpu.make_async_copy(w2_hbm.at[ex], w2f.at[s], sem_w.at[1, s]))

    @pl.when(e == 0)
    def _():
        for cp in weights(0, 0):
            cp.start()

    for cp in weights(e, wslot):
        cp.wait()

    @pl.when(e + 1 < n_exp)
    def _():
        for cp in weights(e + 1, 1 - wslot):
            cp.start(priority=low)
    first = meta_ref[0, e] // MOE_BLOCK
    nblk = meta_ref[2, e] // MOE_BLOCK
    nfull = nblk // 2

    def rows_of(b):
        return pl.ds(pl.multiple_of((first + b) * blk_rows, blk_rows), blk_rows)

    def fetch(b, s):
        return pltpu.make_async_copy(xs_ref.at[rows_of(b)], xbuf.at[s], sem_in.at[s])

    def store(b, s):
        return pltpu.make_async_copy(ybuf.at[s], ys_ref.at[rows_of(b)], sem_out.at[s])

    def compute(s):
        out = _swiglu_packed(xbuf.at[s], w1b_ref, w2b_ref)
        half = out.shape[1] // 2
        _store_rows(ybuf.at[s], _pack2(out[:, :half], out[:, half:]))

    odd = lax.rem(nblk, 2) == 1
    tail_slot = 4
    row_queue = 0

    @pl.when(nblk > 0)
    def _():
        @pl.when(nfull > 0)
        def _():
            fetch(0, 0).start(priority=row_queue)
            fetch(1, 1).start(priority=row_queue)

        @pl.when(odd)
        def _():
            fetch(nblk - 1, tail_slot).start(priority=row_queue)

        w1b_ref[...] = w1f[wslot].astype(BF16)
        w2b_ref[...] = w2f[wslot].astype(BF16)

        def pair(p, carry):
            s0 = 2 * lax.rem(p, 2)
            n0 = 2 - s0
            fetch(2 * p, s0).wait()
            fetch(2 * p + 1, s0 + 1).wait()

            @pl.when(p + 1 < nfull)
            def _():
                fetch(2 * p + 2, n0).start(priority=row_queue)
                fetch(2 * p + 3, n0 + 1).start(priority=row_queue)

            @pl.when(p >= 2)
            def _():
                store(2 * p - 4, s0).wait()
                store(2 * p - 3, s0 + 1).wait()

            compute(s0)
            compute(s0 + 1)
            store(2 * p, s0).start(priority=row_queue)
            store(2 * p + 1, s0 + 1).start(priority=row_queue)
            return carry

        lax.fori_loop(0, nfull, pair, 0)

        @pl.when(odd)
        def _():
            fetch(nblk - 1, tail_slot).wait()
            compute(tail_slot)
            store(nblk - 1, tail_slot).start(priority=row_queue)

        for back in (2, 1):
            @pl.when(nfull >= back)
            def _():
                q = nfull - back
                s0 = 2 * lax.rem(q, 2)
                store(2 * q, s0).wait()
                store(2 * q + 1, s0 + 1).wait()

        @pl.when(odd)
        def _():
            store(nblk - 1, tail_slot).wait()

    @pl.when(e == pl.num_programs(0) - 1)
    def _():
        _zero_rows(ybuf.at[0])

        def unused(j):
            start = pl.multiple_of((meta_ref[3, 0] + j) * blk_rows, blk_rows)
            return pltpu.make_async_copy(ybuf.at[0], ys_ref.at[pl.ds(start, blk_rows)],
                                         sem_out.at[0])

        for j in range(N_EXPERTS):
            @pl.when(meta_ref[3, 0] + j < n_blocks)
            def _():
                unused(j).start()
        for j in range(N_EXPERTS):
            @pl.when(meta_ref[3, 0] + j < n_blocks)
            def _():
                unused(j).wait()


def _expert_call(meta, xs, w_exp_in, w_exp_out):
    blk_rows = MOE_BLOCK * SUBLANES
    n_exp, d, h2 = w_exp_in.shape
    hid = w_exp_out.shape[1]
    n_blocks = xs.shape[0] // blk_rows
    return pl.pallas_call(
        functools.partial(_expert_kernel, n_blocks=n_blocks),
        grid_spec=pltpu.PrefetchScalarGridSpec(
            num_scalar_prefetch=1, grid=(n_exp,),
            in_specs=[pl.BlockSpec(memory_space=pl.ANY),
                      pl.BlockSpec(memory_space=pl.ANY),
                      pl.BlockSpec(memory_space=pl.ANY)],
            out_specs=pl.BlockSpec(memory_space=pl.ANY),
            scratch_shapes=[pltpu.VMEM((2, d, h2), F32), pltpu.VMEM((2, hid, d), F32),
                            pltpu.VMEM((d, h2), BF16), pltpu.VMEM((hid, d), BF16),
                            pltpu.VMEM((5, blk_rows, LANES), U32),
                            pltpu.VMEM((5, blk_rows, LANES), U32),
                            pltpu.SemaphoreType.DMA((2, 2)),
                            pltpu.SemaphoreType.DMA((5,)), pltpu.SemaphoreType.DMA((5,))]),
        out_shape=jax.ShapeDtypeStruct(xs.shape, U32),
        compiler_params=_params(("arbitrary",)),
        name="experts",
    )(meta, xs, w_exp_in, w_exp_out)


def _combine_kernel(dest_ref, w_ref, ys_ref, sh_ref, x1_ref, mod_ref, g_ref, o_ref, buf_ref, sem,
                    *, tc, final_norm):
    def issue(r, carry):
        for k in range(TOP_K):
            pltpu.make_async_copy(_token(ys_ref, dest_ref[k, r]), _token(buf_ref.at[k], r),
                                  sem).start(priority=k % 2)
        return carry

    lax.fori_loop(0, tc, issue, 0)
    for k in range(TOP_K):
        pltpu.make_async_copy(ys_ref.at[pl.ds(0, tc * SUBLANES)], buf_ref.at[k], sem).wait()

    acc_lo = acc_hi = None
    for k in range(TOP_K):
        wk = w_ref[:, k:k + 1]
        lo, hi = _unpack2(_load_rows(buf_ref.at[k]))
        acc_lo = wk * lo if acc_lo is None else acc_lo + wk * lo
        acc_hi = wk * hi if acc_hi is None else acc_hi + wk * hi
    routed = jnp.concatenate([acc_lo, acc_hi], axis=1)
    moe = routed + sh_ref[...].astype(F32)
    x2 = x1_ref[...] + mod_ref[5:6, :] * moe
    o_ref[...] = _rms(x2) * g_ref[...] if final_norm else x2


def _combine_call(dest, wsel_t, ys, shared, x1, mod3, g_final, seq, final_norm):
    t, d = x1.shape
    half = d // 2
    tc = min(256, seq)
    per_b = seq // tc
    return pl.pallas_call(
        functools.partial(_combine_kernel, tc=tc, final_norm=final_norm),
        grid=(t // tc,),
        in_specs=[pl.BlockSpec((TOP_K, tc), lambda i: (0, i), memory_space=pltpu.SMEM),
                  pl.BlockSpec((tc, TOP_K), lambda i: (i, 0)),
                  pl.BlockSpec(memory_space=pl.ANY),
                  pl.BlockSpec((tc, d), lambda i: (i, 0)),
                  pl.BlockSpec((tc, d), lambda i: (i, 0)),
                  pl.BlockSpec((None, 6, d), lambda i: (i // per_b, 0, 0)),
                  pl.BlockSpec((1, d), lambda i: (0, 0))],
        out_specs=pl.BlockSpec((tc, d), lambda i: (i, 0)),
        out_shape=jax.ShapeDtypeStruct((t, d), F32),
        scratch_shapes=[pltpu.VMEM((TOP_K, tc * SUBLANES, LANES), U32),
                        pltpu.SemaphoreType.DMA],
        compiler_params=_params(("arbitrary",)),
        name="combine_final",
    )(dest, wsel_t, ys, shared, x1, mod3, g_final.reshape(1, d))


def _moe_plan(counts):
    pcounts = (counts + MOE_BLOCK - 1) // MOE_BLOCK * MOE_BLOCK
    pends = jnp.cumsum(pcounts)
    pstarts = pends - pcounts
    return pstarts.astype(I32), pcounts.astype(I32), pends.astype(I32)


def kernel(x, c, positions, w_ada, b_ada, g_mix, w_in, w_conv, b_conv, w_rg_a, b_rg_a, w_rg_x, b_rg_x, lam, beta_ret, beta_lru, w_out, g_moe, w_router, b_router, w_exp_in, w_exp_out, w_sh_in, w_sh_out, g_final):
    batch, seq, d = x.shape
    t = batch * seq
    depth = w_ada.shape[0]
    xcur = x.reshape(t, d)
    cos, sin = _rope_call(positions)
    n_rows = (t * TOP_K + MOE_BLOCK - 1) // MOE_BLOCK * MOE_BLOCK + N_EXPERTS * MOE_BLOCK
    for l in range(depth):
        mod3 = _mod_call(c, w_ada[l], b_ada[l]).reshape(batch, 6, d)
        proj = _in_call(xcur, mod3, g_mix[l], w_in[l].astype(BF16), seq)
        ret = _ret_call(proj, cos, sin, beta_ret[l], batch, seq)
        lru = _lru_call(proj, w_conv[l], b_conv[l], w_rg_a[l], b_rg_a[l], w_rg_x[l], b_rg_x[l],
                        lam[l], beta_lru[l], batch, seq)
        x1, h2p = _out_call(ret, lru, xcur, mod3, g_moe[l], w_out[l].astype(BF16), seq)

        eidx, wsel, pos, cnt = _route_call(h2p, w_router[l], b_router[l])
        counts = cnt[:, 0].astype(I32)
        pstarts, pcounts, pends = _moe_plan(counts)
        dest = _dest_call(pstarts, eidx, pos)
        n_used = (pends[-1] // MOE_BLOCK).astype(I32)
        meta = jnp.stack([pstarts, counts, pcounts, jnp.full_like(counts, n_used)])

        xs, shared = _dispatch_call(meta, dest, h2p, w_sh_in[l].astype(BF16),
                                    w_sh_out[l].astype(BF16), n_rows)
        ys = _expert_call(meta, xs, w_exp_in[l], w_exp_out[l])
        xcur = _combine_call(dest, wsel.T, ys, shared, x1, mod3, g_final, seq,
                             final_norm=(l == depth - 1))
    return xcur.reshape(batch, seq, d)
```

```python
import functools

import numpy as np
import jax
import jax.numpy as jnp
from jax import lax
from jax.experimental import pallas as pl
from jax.experimental.pallas import tpu as pltpu

F32 = jnp.float32
BF16 = jnp.bfloat16
I32 = jnp.int32
U32 = jnp.uint32

RET_HEADS = 8
HEAD_DIM = 128
CHUNK = 128
ROPE_THETA = 10000.0
LRU_BLOCKS = 8
CONV_WIDTH = 4
LRU_C = 8.0
N_EXPERTS = 64
N_GROUPS = 8
GROUP_SIZE = N_EXPERTS // N_GROUPS
TOPK_GROUPS = 4
TOP_K = 8
ROUTED_SCALE = 2.5
EPS = 1e-6

LANES = 128
SUBLANES = 8
VMEM_LIMIT_BYTES = 56 * 1024 * 1024

MOE_BLOCK = 256
NEG_INF = float("-inf")


def _params(semantics, vmem=VMEM_LIMIT_BYTES):
    return pltpu.CompilerParams(dimension_semantics=semantics, vmem_limit_bytes=vmem)


def _silu(x):
    return x * jax.nn.sigmoid(x)


def _rms(x):
    return x * lax.rsqrt(jnp.mean(x * x, axis=-1, keepdims=True) + EPS)


def _pack2(lo, hi):
    return pltpu.pack_elementwise([lo, hi], packed_dtype=jnp.bfloat16)


def _unpack2(w):
    lo = pltpu.unpack_elementwise(w, index=0, packed_dtype=jnp.bfloat16, unpacked_dtype=F32)
    hi = pltpu.unpack_elementwise(w, index=1, packed_dtype=jnp.bfloat16, unpacked_dtype=F32)
    return lo, hi


def _load_rows(ref):
    rows = ref.shape[0] // SUBLANES
    return jnp.concatenate([ref[pl.ds(j, rows, stride=SUBLANES), :] for j in range(SUBLANES)],
                           axis=1)


def _store_rows(ref, v):
    rows = ref.shape[0] // SUBLANES
    for j in range(SUBLANES):
        ref[pl.ds(j, rows, stride=SUBLANES), :] = v[:, j * LANES:(j + 1) * LANES]


def _zero_rows(ref):
    rows = ref.shape[0] // SUBLANES
    z = jnp.zeros((rows, LANES), F32)
    w = _pack2(z, z)
    for j in range(SUBLANES):
        ref[pl.ds(j, rows, stride=SUBLANES), :] = w


def _token(ref, r):
    return ref.at[pl.ds(pl.multiple_of(r * SUBLANES, SUBLANES), SUBLANES)]


def _mod_kernel(c_ref, w_ref, b_ref, o_ref):
    cs = _silu(c_ref[...])
    o_ref[...] = jnp.dot(cs, w_ref[...], preferred_element_type=F32,
                         precision=lax.Precision.HIGHEST) + b_ref[...]


def _mod_call(c, w_ada, b_ada):
    b, d = c.shape
    n = w_ada.shape[1]
    tn = 1024
    return pl.pallas_call(
        _mod_kernel,
        grid=(n // tn,),
        in_specs=[pl.BlockSpec((b, d), lambda j: (0, 0)),
                  pl.BlockSpec((d, tn), lambda j: (0, j)),
                  pl.BlockSpec((1, tn), lambda j: (0, j))],
        out_specs=pl.BlockSpec((b, tn), lambda j: (0, j)),
        out_shape=jax.ShapeDtypeStruct((b, n), F32),
        compiler_params=_params(("parallel",)),
        name="adaln_mod",
    )(c, w_ada, b_ada.reshape(1, n))


def _in_kernel(x_ref, mod_ref, g_ref, w_ref, o_ref, h_ref):
    @pl.when(pl.program_id(1) == 0)
    def _():
        y = _rms(x_ref[...]) * g_ref[...]
        h_ref[...] = (y * (1.0 + mod_ref[1:2, :]) + mod_ref[0:1, :]).astype(BF16)

    o_ref[...] = jnp.dot(h_ref[...], w_ref[...], preferred_element_type=F32).astype(o_ref.dtype)


def _in_call(x2, mod3, g_mix, w_in_bf, seq):
    t, d = x2.shape
    n_out = w_in_bf.shape[1] // 1024
    tm = min(1024, seq)
    per_b = seq // tm
    return pl.pallas_call(
        _in_kernel,
        grid=(t // tm, n_out),
        in_specs=[pl.BlockSpec((tm, d), lambda i, j: (i, 0)),
                  pl.BlockSpec((None, 6, d), lambda i, j: (i // per_b, 0, 0)),
                  pl.BlockSpec((1, d), lambda i, j: (0, 0)),
                  pl.BlockSpec((d, 1024), lambda i, j: (0, j))],
        out_specs=pl.BlockSpec((None, tm, 1024), lambda i, j: (j, i, 0)),
        out_shape=jax.ShapeDtypeStruct((n_out, t, 1024), BF16),
        scratch_shapes=[pltpu.VMEM((tm, d), BF16)],
        compiler_params=_params(("parallel", "arbitrary")),
        name="in_proj",
    )(x2, mod3, g_mix.reshape(1, d), w_in_bf)


def _rope_kernel(pos_ref, invf_ref, sign_ref, cos_ref, sin_ref):
    ang = pos_ref[...].astype(F32) * invf_ref[...]
    cos_ref[...] = jnp.cos(ang)
    sin_ref[...] = jnp.sin(ang) * sign_ref[...]


def _rope_call(positions):
    t = positions.size
    tp = min(2048, t)
    half = HEAD_DIM // 2
    inv_freq = 1.0 / (ROPE_THETA ** (jnp.arange(0, HEAD_DIM, 2, dtype=F32) / HEAD_DIM))
    invf = jnp.concatenate([inv_freq, inv_freq]).reshape(1, HEAD_DIM)
    sign = jnp.concatenate([-jnp.ones((half,), F32), jnp.ones((half,), F32)]).reshape(1, HEAD_DIM)
    return pl.pallas_call(
        _rope_kernel,
        grid=(t // tp,),
        in_specs=[pl.BlockSpec((tp, 1), lambda i: (i, 0)),
                  pl.BlockSpec((1, HEAD_DIM), lambda i: (0, 0)),
                  pl.BlockSpec((1, HEAD_DIM), lambda i: (0, 0))],
        out_specs=[pl.BlockSpec((tp, HEAD_DIM), lambda i: (i, 0)),
                   pl.BlockSpec((tp, HEAD_DIM), lambda i: (i, 0))],
        out_shape=[jax.ShapeDtypeStruct((t, HEAD_DIM), F32)] * 2,
        compiler_params=_params(("parallel",)),
        name="rope_tables",
    )(positions.reshape(t, 1), invf, sign)


def _ret_tables():
    h = np.arange(RET_HEADS, dtype=np.float64)
    lg = np.log1p(-np.exp2(-5.0 - h))
    idx = np.arange(CHUNK, dtype=np.float64)
    diff = idx[:, None] - idx[None, :]
    intra = np.where(diff >= 0, np.exp(lg[:, None, None] * np.maximum(diff, 0.0)), 0.0)
    kdec = np.exp(lg[:, None] * (CHUNK - 1.0 - idx)[None, :])
    qdec = np.exp(lg[:, None] * (idx + 1.0)[None, :])
    cdec = np.exp(lg * CHUNK)
    bc = lambda a: np.ascontiguousarray(np.broadcast_to(a[:, :, None], (RET_HEADS, CHUNK, HEAD_DIM)))
    return (intra.astype(np.float32), bc(qdec).astype(np.float32), bc(kdec).astype(np.float32),
            [float(v) for v in cdec])


def _ret_kernel(q_ref, k_ref, v_ref, g_ref, cos_ref, sin_ref, intra_ref, qdec_ref, kdec_ref,
                beta_ref, o_ref, st_ref, *, cdec):
    @pl.when(pl.program_id(1) == 0)
    def _():
        st_ref[...] = jnp.zeros_like(st_ref)

    scale = HEAD_DIM ** -0.5
    nt = (((1,), (1,)), ((), ()))
    tn = (((0,), (0,)), ((), ()))
    n_chunks = q_ref.shape[0] // CHUNK
    outs = [[] for _ in range(n_chunks)]
    for h in range(RET_HEADS):
        sl = slice(h * HEAD_DIM, (h + 1) * HEAD_DIM)
        st = st_ref[h]
        for c in range(n_chunks):
            rows = slice(c * CHUNK, (c + 1) * CHUNK)
            cos = cos_ref[rows, :]
            sin = sin_ref[rows, :]
            q = q_ref[rows, sl].astype(F32)
            k = k_ref[rows, sl].astype(F32)
            v = v_ref[rows, sl]
            qr = (q * cos + pltpu.roll(q, HEAD_DIM // 2, 1) * sin) * scale
            kr = k * cos + pltpu.roll(k, HEAD_DIM // 2, 1) * sin
            s = lax.dot_general(qr.astype(BF16), kr.astype(BF16), nt,
                                preferred_element_type=F32) * intra_ref[h]
            inner = jnp.dot(s.astype(BF16), v, preferred_element_type=F32)
            cross = jnp.dot((qr * qdec_ref[h]).astype(BF16), st.astype(BF16),
                            preferred_element_type=F32)
            o = inner + cross
            kv = lax.dot_general((kr * kdec_ref[h]).astype(BF16), v, tn,
                                 preferred_element_type=F32)
            st = cdec[h] * st + kv
            oc = o - jnp.mean(o, axis=-1, keepdims=True)
            on = oc * lax.rsqrt(jnp.mean(oc * oc, axis=-1, keepdims=True) + EPS)
            outs[c].append(_silu(g_ref[rows, sl].astype(F32)) * on)
        st_ref[h] = st
    for c in range(n_chunks):
        ret = jnp.concatenate(outs[c], axis=1)
        o_ref[c * CHUNK:(c + 1) * CHUNK, :] = (_rms(ret) * beta_ref[...]).astype(o_ref.dtype)


def _ret_call(proj, cos, sin, beta_ret, batch, seq):
    t = batch * seq
    rw = RET_HEADS * HEAD_DIM
    tr = min(2 * CHUNK, seq)
    nc = seq // tr
    intra, qdec, kdec, cdec = _ret_tables()
    row = lambda which: pl.BlockSpec((None, tr, rw), lambda b, n: (which, b * nc + n, 0))
    tab = lambda: pl.BlockSpec((RET_HEADS, CHUNK, HEAD_DIM), lambda b, n: (0, 0, 0))
    cs = lambda: pl.BlockSpec((tr, HEAD_DIM), lambda b, n: (b * nc + n, 0))
    return pl.pallas_call(
        functools.partial(_ret_kernel, cdec=cdec),
        grid=(batch, nc),
        in_specs=[row(0), row(1), row(2), row(3), cs(), cs(), tab(), tab(), tab(),
                  pl.BlockSpec((1, rw), lambda b, n: (0, 0))],
        out_specs=pl.BlockSpec((tr, rw), lambda b, n: (b * nc + n, 0)),
        out_shape=jax.ShapeDtypeStruct((t, rw), BF16),
        scratch_shapes=[pltpu.VMEM((RET_HEADS, HEAD_DIM, HEAD_DIM), F32)],
        compiler_params=_params(("parallel", "arbitrary")),
        name="retention",
    )(proj, proj, proj, proj, cos, sin, jnp.asarray(intra), jnp.asarray(qdec), jnp.asarray(kdec),
      beta_ret.reshape(1, rw))


def _lru_kernel(xr_ref, yr_ref, wc_ref, bc_ref, wg_ref, ba_ref, bx_ref, lam_ref, beta_ref,
                o_ref, xtail_ref, h_ref, *, ts):
    @pl.when(pl.program_id(1) == 0)
    def _():
        xtail_ref[...] = jnp.zeros_like(xtail_ref)
        h_ref[...] = jnp.zeros_like(h_ref)

    x = xr_ref[...].astype(F32)
    w = x.shape[1]
    rows = lax.broadcasted_iota(I32, (ts, w), 0)
    head_rows = lax.broadcasted_iota(I32, (SUBLANES, w), 0)
    tail = xtail_ref[...]
    xc = bc_ref[...]
    for tap in range(CONV_WIDTH):
        back = CONV_WIDTH - 1 - tap
        if back == 0:
            xs = x
        else:
            rolled = pltpu.roll(x, back, 0)
            head = jnp.where(head_rows < back, pltpu.roll(tail, back, 0), rolled[:SUBLANES])
            xs = jnp.concatenate([head, rolled[SUBLANES:]], axis=0)
        xc = xc + xs * wc_ref[tap:tap + 1, :]
    xtail_ref[...] = x[ts - SUBLANES:ts, :]

    xcb = xc.astype(BF16)
    bd = w // LRU_BLOCKS
    rs, gs = [], []
    for g in range(LRU_BLOCKS):
        z = jnp.dot(xcb[:, g * bd:(g + 1) * bd], wg_ref[g], preferred_element_type=F32)
        rs.append(z[:, :bd])
        gs.append(z[:, bd:])
    r = jax.nn.sigmoid(jnp.concatenate(rs, axis=1) + ba_ref[...])
    i = jax.nn.sigmoid(jnp.concatenate(gs, axis=1) + bx_ref[...])
    log_a = -LRU_C * r * jax.nn.softplus(-lam_ref[...])
    a = jnp.exp(log_a)
    bt = jnp.sqrt(-jnp.tanh(log_a) * (a * a + 1.0)) * (i * xc)

    acc_a, acc_b = a, bt
    sh = 1
    while sh < ts:
        if sh < SUBLANES:
            m = rows >= sh
            acc_b = jnp.where(m, acc_a * pltpu.roll(acc_b, sh, 0) + acc_b, acc_b)
            acc_a = jnp.where(m, acc_a * pltpu.roll(acc_a, sh, 0), acc_a)
        else:
            new_b = acc_a[sh:] * acc_b[:ts - sh] + acc_b[sh:]
            acc_a = jnp.concatenate([acc_a[:sh], acc_a[sh:] * acc_a[:ts - sh]], axis=0)
            acc_b = jnp.concatenate([acc_b[:sh], new_b], axis=0)
        sh *= 2
    hs = acc_a * h_ref[...] + acc_b
    h_ref[...] = hs[ts - 1:ts, :]

    lru = hs * jax.nn.gelu(yr_ref[...].astype(F32), approximate=True)
    o_ref[...] = (_rms(lru) * beta_ref[...]).astype(o_ref.dtype)


def _lru_call(proj, w_conv, b_conv, w_rg_a, b_rg_a, w_rg_x, b_rg_x, lam, beta_lru, batch, seq):
    t = batch * seq
    w = w_conv.shape[1]
    ts = min(256, seq)
    nt = seq // ts
    wg = jnp.concatenate([w_rg_a, w_rg_x], axis=-1).astype(BF16)
    vec = lambda: pl.BlockSpec((1, w), lambda b, j: (0, 0))
    row = lambda which: pl.BlockSpec((None, ts, w), lambda b, j: (which, b * nt + j, 0))
    return pl.pallas_call(
        functools.partial(_lru_kernel, ts=ts),
        grid=(batch, nt),
        in_specs=[row(4), row(5),
                  pl.BlockSpec((CONV_WIDTH, w), lambda b, j: (0, 0)), vec(),
                  pl.BlockSpec(wg.shape, lambda b, j: (0, 0, 0)),
                  vec(), vec(), vec(), vec()],
        out_specs=pl.BlockSpec((ts, w), lambda b, j: (b * nt + j, 0)),
        out_shape=jax.ShapeDtypeStruct((t, w), BF16),
        scratch_shapes=[pltpu.VMEM((SUBLANES, w), F32), pltpu.VMEM((1, w), F32)],
        compiler_params=_params(("parallel", "arbitrary")),
        name="rg_lru",
    )(proj, proj, w_conv, b_conv.reshape(1, w), wg, b_rg_a.reshape(1, w), b_rg_x.reshape(1, w),
      lam.reshape(1, w), beta_lru.reshape(1, w))


def _out_kernel(ret_ref, lru_ref, x_ref, mod_ref, g_ref, w_ref, x1_ref, h2_ref):
    rw = ret_ref.shape[1]
    mix = jnp.dot(ret_ref[...], w_ref[0:rw, :], preferred_element_type=F32)
    mix = mix + jnp.dot(lru_ref[...], w_ref[rw:, :], preferred_element_type=F32)
    x1 = x_ref[...] + mod_ref[2:3, :] * mix
    x1_ref[...] = x1
    h2 = _rms(x1) * g_ref[...] * (1.0 + mod_ref[4:5, :]) + mod_ref[3:4, :]
    half = h2.shape[1] // 2
    _store_rows(h2_ref, _pack2(h2[:, :half], h2[:, half:]))


def _out_call(ret, lru, x2, mod3, g_moe, w_out_bf, seq):
    t, d = x2.shape
    assert d // 2 == SUBLANES * LANES, "packed token rows are exactly one (8, 128) word tile"
    rw, lw = ret.shape[1], lru.shape[1]
    tm = min(512, seq)
    per_b = seq // tm
    return pl.pallas_call(
        _out_kernel,
        grid=(t // tm,),
        in_specs=[pl.BlockSpec((tm, rw), lambda i: (i, 0)),
                  pl.BlockSpec((tm, lw), lambda i: (i, 0)),
                  pl.BlockSpec((tm, d), lambda i: (i, 0)),
                  pl.BlockSpec((None, 6, d), lambda i: (i // per_b, 0, 0)),
                  pl.BlockSpec((1, d), lambda i: (0, 0)),
                  pl.BlockSpec((rw + lw, d), lambda i: (0, 0))],
        out_specs=[pl.BlockSpec((tm, d), lambda i: (i, 0)),
                   pl.BlockSpec((tm * SUBLANES, LANES), lambda i: (i, 0))],
        out_shape=[jax.ShapeDtypeStruct((t, d), F32),
                   jax.ShapeDtypeStruct((t * SUBLANES, LANES), U32)],
        compiler_params=_params(("parallel",)),
        name="out_proj",
    )(ret, lru, x2, mod3, g_moe.reshape(1, d), w_out_bf)


def _route_kernel(h_ref, wlo_ref, whi_ref, bias_ref, tri_ref, eidx_ref, wsel_ref, pos_ref,
                  cnt_ref, carry_ref):
    @pl.when(pl.program_id(0) == 0)
    def _():
        carry_ref[...] = jnp.zeros_like(carry_ref)

    tr = h_ref.shape[0] // SUBLANES
    lo, hi = _unpack2(_load_rows(h_ref))
    nt = (((1,), (1,)), ((), ()))
    logits = lax.dot_general(wlo_ref[...], lo.astype(BF16), nt, preferred_element_type=F32)
    logits = logits + lax.dot_general(whi_ref[...], hi.astype(BF16), nt,
                                      preferred_element_type=F32)
    scores = jax.nn.sigmoid(logits)
    biased = scores + bias_ref[:, 0:1]
    shape3 = (N_GROUPS, GROUP_SIZE, tr)
    s3 = scores.reshape(shape3)
    b3 = biased.reshape(shape3)
    member = lax.broadcasted_iota(I32, shape3, 1)
    group = lax.broadcasted_iota(I32, shape3, 0)
    expert = group * GROUP_SIZE + member

    m1 = jnp.max(b3, axis=1, keepdims=True)
    i1 = jnp.min(jnp.where(b3 == m1, member, GROUP_SIZE), axis=1, keepdims=True)
    m2 = jnp.max(jnp.where(member == i1, NEG_INF, b3), axis=1, keepdims=True)
    gscore = m1 + m2

    gid = lax.broadcasted_iota(I32, (N_GROUPS, 1, tr), 0)
    rank = jnp.zeros((N_GROUPS, 1, tr), I32)
    for j in range(N_GROUPS):
        gj = gscore[j:j + 1]
        ahead = (gj > gscore) | ((gj == gscore) & (gid > j))
        rank = rank + ahead.astype(I32)
    masked = jnp.where(rank < TOPK_GROUPS, b3, NEG_INF)

    idxs, vals = [], []
    picked = jnp.zeros(shape3, F32)
    for _ in range(TOP_K):
        m = jnp.max(jnp.max(masked, axis=0, keepdims=True), axis=1, keepdims=True)
        cand = jnp.where(masked == m, expert, N_EXPERTS)
        idx = jnp.min(jnp.min(cand, axis=0, keepdims=True), axis=1, keepdims=True)
        hit = expert == idx
        val = jnp.sum(jnp.sum(jnp.where(hit, s3, 0.0), axis=0, keepdims=True), axis=1, keepdims=True)
        masked = jnp.where(hit, NEG_INF, masked)
        picked = jnp.where(hit, 1.0, picked)
        idxs.append(idx)
        vals.append(val)
    total = vals[0]
    for v in vals[1:]:
        total = total + v

    before = jnp.dot(picked.reshape(N_EXPERTS, tr).astype(BF16), tri_ref[...],
                     preferred_element_type=F32)
    carry = carry_ref[...]
    posm = (before + carry[:, 0:1]).reshape(shape3)
    carry = carry + jnp.sum(picked.reshape(N_EXPERTS, tr), axis=1, keepdims=True)
    carry_ref[...] = carry
    cnt_ref[...] = carry

    for k in range(TOP_K):
        hit = expert == idxs[k]
        p = jnp.sum(jnp.sum(jnp.where(hit, posm, 0.0), axis=0, keepdims=True), axis=1, keepdims=True)
        eidx_ref[k:k + 1, :] = idxs[k].reshape(1, tr)
        wsel_ref[k:k + 1, :] = (vals[k] / total * ROUTED_SCALE).reshape(1, tr)
        pos_ref[k:k + 1, :] = p.reshape(1, tr).astype(I32)


def _route_call(h2p, w_router, b_router):
    t = h2p.shape[0] // SUBLANES
    half = SUBLANES * LANES
    tr = min(512, t)
    wt = w_router.T.astype(BF16)
    bias = jnp.broadcast_to(b_router.astype(F32)[:, None], (N_EXPERTS, LANES))
    tri = jnp.asarray(np.triu(np.ones((tr, tr), np.float32), k=1), dtype=BF16)
    kt = lambda dt: jax.ShapeDtypeStruct((TOP_K, t), dt)
    krow = lambda: pl.BlockSpec((TOP_K, tr), lambda i: (0, i))
    return pl.pallas_call(
        _route_kernel,
        grid=(t // tr,),
        in_specs=[pl.BlockSpec((tr * SUBLANES, LANES), lambda i: (i, 0)),
                  pl.BlockSpec((N_EXPERTS, half), lambda i: (0, 0)),
                  pl.BlockSpec((N_EXPERTS, half), lambda i: (0, 0)),
                  pl.BlockSpec((N_EXPERTS, LANES), lambda i: (0, 0)),
                  pl.BlockSpec((tr, tr), lambda i: (0, 0))],
        out_specs=[krow(), krow(), krow(),
                   pl.BlockSpec((N_EXPERTS, LANES), lambda i: (0, 0))],
        out_shape=[kt(I32), kt(F32), kt(I32), jax.ShapeDtypeStruct((N_EXPERTS, LANES), F32)],
        scratch_shapes=[pltpu.VMEM((N_EXPERTS, LANES), F32)],
        compiler_params=_params(("arbitrary",)),
        name="router",
    )(h2p, wt[:, :half], wt[:, half:], bias, tri)


def _dest_kernel(pst_ref, eidx_ref, pos_ref, o_ref):
    e = eidx_ref[...]
    acc = pos_ref[...]
    for j in range(N_EXPERTS):
        acc = acc + jnp.where(e == j, pst_ref[j], 0)
    o_ref[...] = acc


def _dest_call(pstarts, eidx, pos):
    k, t = eidx.shape
    tb = min(2048, t)
    blk = lambda: pl.BlockSpec((k, tb), lambda i, pst: (0, i))
    return pl.pallas_call(
        _dest_kernel,
        grid_spec=pltpu.PrefetchScalarGridSpec(
            num_scalar_prefetch=1, grid=(t // tb,),
            in_specs=[blk(), blk()], out_specs=blk()),
        out_shape=jax.ShapeDtypeStruct((k, t), I32),
        compiler_params=_params(("parallel",)),
        name="dest_rows",
    )(pstarts, eidx, pos)


def _swiglu_packed(x_ref, w1_ref, w2_ref):
    xw = _load_rows(x_ref)
    half = xw.shape[1]
    lo, hi = _unpack2(xw)
    gu = jnp.dot(lo.astype(BF16), w1_ref[0:half, :], preferred_element_type=F32)
    gu = gu + jnp.dot(hi.astype(BF16), w1_ref[half:, :], preferred_element_type=F32)
    hid = gu.shape[1] // 2
    act = _silu(gu[:, :hid]) * gu[:, hid:]
    return jnp.dot(act.astype(BF16), w2_ref[...], preferred_element_type=F32)


def _dispatch_kernel(meta_ref, dest_ref, h_ref, w1_ref, w2_ref, xs_ref, sh_ref, zero_ref,
                     sem_z, sem_r, *, td):

    @pl.when(pl.program_id(0) == 0)
    def _():
        _zero_rows(zero_ref)

        blk_rows = MOE_BLOCK * SUBLANES

        def block_copy(first_token):
            start = pl.multiple_of(first_token * SUBLANES, blk_rows)
            return pltpu.make_async_copy(zero_ref, xs_ref.at[pl.ds(start, blk_rows)], sem_z)

        def tail_copy(e):
            return block_copy(meta_ref[0, e] + meta_ref[2, e] - MOE_BLOCK)

        def unused_copy(j):
            return block_copy((meta_ref[3, 0] + j) * MOE_BLOCK)

        n_blocks = xs_ref.shape[0] // blk_rows
        for e in range(N_EXPERTS):
            @pl.when(meta_ref[2, e] > meta_ref[1, e])
            def _():
                tail_copy(e).start()

            @pl.when(meta_ref[3, 0] + e < n_blocks)
            def _():
                unused_copy(e).start()
        for e in range(N_EXPERTS):
            @pl.when(meta_ref[2, e] > meta_ref[1, e])
            def _():
                tail_copy(e).wait()

            @pl.when(meta_ref[3, 0] + e < n_blocks)
            def _():
                unused_copy(e).wait()

    def issue(r, carry):
        src = _token(h_ref, r)
        for k in range(TOP_K):
            pltpu.make_async_copy(src, _token(xs_ref, dest_ref[k, r]), sem_r).start(priority=k % 2)
        return carry

    lax.fori_loop(0, td, issue, 0)

    sh_ref[...] = _swiglu_packed(h_ref, w1_ref, w2_ref).astype(sh_ref.dtype)

    for k in range(TOP_K):
        pltpu.make_async_copy(h_ref, xs_ref.at[pl.ds(0, td * SUBLANES)], sem_r).wait()


def _dispatch_call(meta, dest, h2p, w_sh_in_bf, w_sh_out_bf, n_rows):
    t = h2p.shape[0] // SUBLANES
    d = 2 * SUBLANES * LANES
    td = min(256, t)
    return pl.pallas_call(
        functools.partial(_dispatch_kernel, td=td),
        grid=(t // td,),
        in_specs=[pl.BlockSpec(memory_space=pltpu.SMEM),
                  pl.BlockSpec((TOP_K, td), lambda i: (0, i), memory_space=pltpu.SMEM),
                  pl.BlockSpec((td * SUBLANES, LANES), lambda i: (i, 0)),
                  pl.BlockSpec(w_sh_in_bf.shape, lambda i: (0, 0)),
                  pl.BlockSpec(w_sh_out_bf.shape, lambda i: (0, 0))],
        out_specs=[pl.BlockSpec(memory_space=pl.ANY),
                   pl.BlockSpec((td, d), lambda i: (i, 0))],
        out_shape=[jax.ShapeDtypeStruct((n_rows * SUBLANES, LANES), U32),
                   jax.ShapeDtypeStruct((t, d), BF16)],
        scratch_shapes=[pltpu.VMEM((MOE_BLOCK * SUBLANES, LANES), U32),
                        pltpu.SemaphoreType.DMA, pltpu.SemaphoreType.DMA],
        compiler_params=_params(("arbitrary",)),
        name="dispatch_shared",
    )(meta, dest, h2p, w_sh_in_bf, w_sh_out_bf)


def _expert_kernel(meta_ref, xs_ref, w1_hbm, w2_hbm, ys_ref, w1f, w2f, w1b_ref, w2b_ref, xbuf, ybuf,
                   sem_w, sem_in, sem_out, *, n_blocks):
    e = pl.program_id(0)
    n_exp = pl.num_programs(0)
    blk_rows = MOE_BLOCK * SUBLANES
    wslot = lax.rem(e, 2)
    low = 1

    def weights(ex, s):
        return (pltpu.make_async_copy(w1_hbm.at[ex], w1f.at[s], sem_w.at[0, s]),
                pltpu.make_async_copy(w2_hbm.at[ex], w2f.at[s], sem_w.at[1, s]))

    @pl.when(e == 0)
    def _():
        for cp in weights(0, 0):
            cp.start()

    for cp in weights(e, wslot):
        cp.wait()

    @pl.when(e + 1 < n_exp)
    def _():
        for cp in weights(e + 1, 1 - wslot):
            cp.start(priority=low)
    first = meta_ref[0, e] // MOE_BLOCK
    nblk = meta_ref[2, e] // MOE_BLOCK
    nfull = nblk // 2

    def rows_of(b):
        return pl.ds(pl.multiple_of((first + b) * blk_rows, blk_rows), blk_rows)

    def fetch(b, s):
        return pltpu.make_async_copy(xs_ref.at[rows_of(b)], xbuf.at[s], sem_in.at[s])

    def store(b, s):
        return pltpu.make_async_copy(ybuf.at[s], ys_ref.at[rows_of(b)], sem_out.at[s])

    def compute(s):
        out = _swiglu_packed(xbuf.at[s], w1b_ref, w2b_ref)
        half = out.shape[1] // 2
        _store_rows(ybuf.at[s], _pack2(out[:, :half], out[:, half:]))

    odd = lax.rem(nblk, 2) == 1
    tail_slot = 4
    row_queue = 0

    @pl.when(nblk > 0)
    def _():
        @pl.when(nfull > 0)
        def _():
            fetch(0, 0).start(priority=row_queue)
            fetch(1, 1).start(priority=row_queue)

        @pl.when(odd)
        def _():
            fetch(nblk - 1, tail_slot).start(priority=row_queue)

        w1b_ref[...] = w1f[wslot].astype(BF16)
        w2b_ref[...] = w2f[wslot].astype(BF16)

        def pair(p, carry):
            s0 = 2 * lax.rem(p, 2)
            n0 = 2 - s0
            fetch(2 * p, s0).wait()
            fetch(2 * p + 1, s0 + 1).wait()

            @pl.when(p + 1 < nfull)
            def _():
                fetch(2 * p + 2, n0).start(priority=row_queue)
                fetch(2 * p + 3, n0 + 1).start(priority=row_queue)

            @pl.when(p >= 2)
            def _():
                store(2 * p - 4, s0).wait()
                store(2 * p - 3, s0 + 1).wait()

            compute(s0)
            compute(s0 + 1)
            store(2 * p, s0).start(priority=row_queue)
            store(2 * p + 1, s0 + 1).start(priority=row_queue)
            return carry

        lax.fori_loop(0, nfull, pair, 0)

        @pl.when(odd)
        def _():
            fetch(nblk - 1, tail_slot).wait()
            compute(tail_slot)
            store(nblk - 1, tail_slot).start(priority=row_queue)

        for back in (2, 1):
            @pl.when(nfull >= back)
            def _():
                q = nfull - back
                s0 = 2 * lax.rem(q, 2)
                store(2 * q, s0).wait()
                store(2 * q + 1, s0 + 1).wait()

        @pl.when(odd)
        def _():
            store(nblk - 1, tail_slot).wait()

    @pl.when(e == pl.num_programs(0) - 1)
    def _():
        _zero_rows(ybuf.at[0])

        def unused(j):
            start = pl.multiple_of((meta_ref[3, 0] + j) * blk_rows, blk_rows)
            return pltpu.make_async_copy(ybuf.at[0], ys_ref.at[pl.ds(start, blk_rows)],
                                         sem_out.at[0])

        for j in range(N_EXPERTS):
            @pl.when(meta_ref[3, 0] + j < n_blocks)
            def _():
                unused(j).start()
        for j in range(N_EXPERTS):
            @pl.when(meta_ref[3, 0] + j < n_blocks)
            def _():
                unused(j).wait()


def _expert_call(meta, xs, w_exp_in, w_exp_out):
    blk_rows = MOE_BLOCK * SUBLANES
    n_exp, d, h2 = w_exp_in.shape
    hid = w_exp_out.shape[1]
    n_blocks = xs.shape[0] // blk_rows
    return pl.pallas_call(
        functools.partial(_expert_kernel, n_blocks=n_blocks),
        grid_spec=pltpu.PrefetchScalarGridSpec(
            num_scalar_prefetch=1, grid=(n_exp,),
            in_specs=[pl.BlockSpec(memory_space=pl.ANY),
                      pl.BlockSpec(memory_space=pl.ANY),
                      pl.BlockSpec(memory_space=pl.ANY)],
            out_specs=pl.BlockSpec(memory_space=pl.ANY),
            scratch_shapes=[pltpu.VMEM((2, d, h2), F32), pltpu.VMEM((2, hid, d), F32),
                            pltpu.VMEM((d, h2), BF16), pltpu.VMEM((hid, d), BF16),
                            pltpu.VMEM((5, blk_rows, LANES), U32),
                            pltpu.VMEM((5, blk_rows, LANES), U32),
                            pltpu.SemaphoreType.DMA((2, 2)),
                            pltpu.SemaphoreType.DMA((5,)), pltpu.SemaphoreType.DMA((5,))]),
        out_shape=jax.ShapeDtypeStruct(xs.shape, U32),
        compiler_params=_params(("arbitrary",)),
        name="experts",
    )(meta, xs, w_exp_in, w_exp_out)


def _combine_kernel(dest_ref, dnext_ref, w_ref, ys_ref, sh_ref, x1_ref, mod_ref, g_ref, o_ref,
                    buf_ref, sem, *, tc, final_norm):
    i = pl.program_id(0)
    slot = lax.rem(i, 2)

    def gather(d_ref, s):
        def issue(r, carry):
            for k in range(TOP_K):
                pltpu.make_async_copy(_token(ys_ref, d_ref[k, r]), _token(buf_ref.at[s, k], r),
                                      sem.at[s]).start(priority=k % 2)
            return carry

        lax.fori_loop(0, tc, issue, 0)

    @pl.when(i == 0)
    def _():
        gather(dest_ref, 0)

    @pl.when(i + 1 < pl.num_programs(0))
    def _():
        gather(dnext_ref, 1 - slot)

    for k in range(TOP_K):
        pltpu.make_async_copy(ys_ref.at[pl.ds(0, tc * SUBLANES)], buf_ref.at[slot, k],
                              sem.at[slot]).wait()

    acc_lo = acc_hi = None
    for k in range(TOP_K):
        wk = w_ref[:, k:k + 1]
        lo, hi = _unpack2(_load_rows(buf_ref.at[slot, k]))
        acc_lo = wk * lo if acc_lo is None else acc_lo + wk * lo
        acc_hi = wk * hi if acc_hi is None else acc_hi + wk * hi
    routed = jnp.concatenate([acc_lo, acc_hi], axis=1)
    moe = routed + sh_ref[...].astype(F32)
    x2 = x1_ref[...] + mod_ref[5:6, :] * moe
    o_ref[...] = _rms(x2) * g_ref[...] if final_norm else x2


def _combine_call(dest, wsel_t, ys, shared, x1, mod3, g_final, seq, final_norm):
    t, d = x1.shape
    tc = min(256, seq)
    per_b = seq // tc
    n_tiles = t // tc
    return pl.pallas_call(
        functools.partial(_combine_kernel, tc=tc, final_norm=final_norm),
        grid=(n_tiles,),
        in_specs=[pl.BlockSpec((TOP_K, tc), lambda i: (0, i), memory_space=pltpu.SMEM),
                  pl.BlockSpec((TOP_K, tc), lambda i: (0, jnp.minimum(i + 1, n_tiles - 1)),
                               memory_space=pltpu.SMEM),
                  pl.BlockSpec((tc, TOP_K), lambda i: (i, 0)),
                  pl.BlockSpec(memory_space=pl.ANY),
                  pl.BlockSpec((tc, d), lambda i: (i, 0)),
                  pl.BlockSpec((tc, d), lambda i: (i, 0)),
                  pl.BlockSpec((None, 6, d), lambda i: (i // per_b, 0, 0)),
                  pl.BlockSpec((1, d), lambda i: (0, 0))],
        out_specs=pl.BlockSpec((tc, d), lambda i: (i, 0)),
        out_shape=jax.ShapeDtypeStruct((t, d), F32),
        scratch_shapes=[pltpu.VMEM((2, TOP_K, tc * SUBLANES, LANES), U32),
                        pltpu.SemaphoreType.DMA((2,))],
        compiler_params=_params(("arbitrary",)),
        name="combine_final",
    )(dest, dest, wsel_t, ys, shared, x1, mod3, g_final.reshape(1, d))


def _moe_plan(counts):
    pcounts = (counts + MOE_BLOCK - 1) // MOE_BLOCK * MOE_BLOCK
    pends = jnp.cumsum(pcounts)
    pstarts = pends - pcounts
    return pstarts.astype(I32), pcounts.astype(I32), pends.astype(I32)


def kernel(x, c, positions, w_ada, b_ada, g_mix, w_in, w_conv, b_conv, w_rg_a, b_rg_a, w_rg_x, b_rg_x, lam, beta_ret, beta_lru, w_out, g_moe, w_router, b_router, w_exp_in, w_exp_out, w_sh_in, w_sh_out, g_final):
    batch, seq, d = x.shape
    t = batch * seq
    depth = w_ada.shape[0]
    xcur = x.reshape(t, d)
    cos, sin = _rope_call(positions)
    n_rows = (t * TOP_K + MOE_BLOCK - 1) // MOE_BLOCK * MOE_BLOCK + N_EXPERTS * MOE_BLOCK
    for l in range(depth):
        mod3 = _mod_call(c, w_ada[l], b_ada[l]).reshape(batch, 6, d)
        proj = _in_call(xcur, mod3, g_mix[l], w_in[l].astype(BF16), seq)
        ret = _ret_call(proj, cos, sin, beta_ret[l], batch, seq)
        lru = _lru_call(proj, w_conv[l], b_conv[l], w_rg_a[l], b_rg_a[l], w_rg_x[l], b_rg_x[l],
                        lam[l], beta_lru[l], batch, seq)
        x1, h2p = _out_call(ret, lru, xcur, mod3, g_moe[l], w_out[l].astype(BF16), seq)

        eidx, wsel, pos, cnt = _route_call(h2p, w_router[l], b_router[l])
        counts = cnt[:, 0].astype(I32)
        pstarts, pcounts, pends = _moe_plan(counts)
        dest = _dest_call(pstarts, eidx, pos)
        n_used = (pends[-1] // MOE_BLOCK).astype(I32)
        meta = jnp.stack([pstarts, counts, pcounts, jnp.full_like(counts, n_used)])

        xs, shared = _dispatch_call(meta, dest, h2p, w_sh_in[l].astype(BF16),
                                    w_sh_out[l].astype(BF16), n_rows)
        ys = _expert_call(meta, xs, w_exp_in[l], w_exp_out[l])
        xcur = _combine_call(dest, wsel.T, ys, shared, x1, mod3, g_final, seq,
                             final_norm=(l == depth - 1))
    return xcur.reshape(batch, seq, d)
```

```python
import functools

import numpy as np
import jax
import jax.numpy as jnp
from jax import lax
from jax.experimental import pallas as pl
from jax.experimental.pallas import tpu as pltpu

F32 = jnp.float32
BF16 = jnp.bfloat16
I32 = jnp.int32
U32 = jnp.uint32

RET_HEADS = 8
HEAD_DIM = 128
CHUNK = 128
ROPE_THETA = 10000.0
LRU_BLOCKS = 8
CONV_WIDTH = 4
LRU_C = 8.0
N_EXPERTS = 64
N_GROUPS = 8
GROUP_SIZE = N_EXPERTS // N_GROUPS
TOPK_GROUPS = 4
TOP_K = 8
ROUTED_SCALE = 2.5
EPS = 1e-6

LANES = 128
SUBLANES = 8
VMEM_LIMIT_BYTES = 56 * 1024 * 1024

MOE_BLOCK = 256
NEG_INF = float("-inf")


def _params(semantics, vmem=VMEM_LIMIT_BYTES):
    return pltpu.CompilerParams(dimension_semantics=semantics, vmem_limit_bytes=vmem)


def _silu(x):
    return x * jax.nn.sigmoid(x)


def _rms(x):
    return x * lax.rsqrt(jnp.mean(x * x, axis=-1, keepdims=True) + EPS)


def _pack2(lo, hi):
    return pltpu.pack_elementwise([lo, hi], packed_dtype=jnp.bfloat16)


def _unpack2(w):
    lo = pltpu.unpack_elementwise(w, index=0, packed_dtype=jnp.bfloat16, unpacked_dtype=F32)
    hi = pltpu.unpack_elementwise(w, index=1, packed_dtype=jnp.bfloat16, unpacked_dtype=F32)
    return lo, hi


def _load_rows(ref):
    rows = ref.shape[0] // SUBLANES
    return jnp.concatenate([ref[pl.ds(j, rows, stride=SUBLANES), :] for j in range(SUBLANES)],
                           axis=1)


def _store_rows(ref, v):
    rows = ref.shape[0] // SUBLANES
    for j in range(SUBLANES):
        ref[pl.ds(j, rows, stride=SUBLANES), :] = v[:, j * LANES:(j + 1) * LANES]


def _zero_rows(ref):
    rows = ref.shape[0] // SUBLANES
    z = jnp.zeros((rows, LANES), F32)
    w = _pack2(z, z)
    for j in range(SUBLANES):
        ref[pl.ds(j, rows, stride=SUBLANES), :] = w


def _token(ref, r):
    return ref.at[pl.ds(pl.multiple_of(r * SUBLANES, SUBLANES), SUBLANES)]


def _mod_kernel(c_ref, w_ref, b_ref, o_ref):
    cs = _silu(c_ref[...])
    o_ref[...] = jnp.dot(cs, w_ref[...], preferred_element_type=F32,
                         precision=lax.Precision.HIGHEST) + b_ref[...]


def _mod_call(c, w_ada, b_ada):
    b, d = c.shape
    n = w_ada.shape[1]
    tn = 1024
    return pl.pallas_call(
        _mod_kernel,
        grid=(n // tn,),
        in_specs=[pl.BlockSpec((b, d), lambda j: (0, 0)),
                  pl.BlockSpec((d, tn), lambda j: (0, j)),
                  pl.BlockSpec((1, tn), lambda j: (0, j))],
        out_specs=pl.BlockSpec((b, tn), lambda j: (0, j)),
        out_shape=jax.ShapeDtypeStruct((b, n), F32),
        compiler_params=_params(("parallel",)),
        name="adaln_mod",
    )(c, w_ada, b_ada.reshape(1, n))


def _in_kernel(x_ref, mod_ref, g_ref, w_ref, o_ref, h_ref):
    @pl.when(pl.program_id(1) == 0)
    def _():
        y = _rms(x_ref[...]) * g_ref[...]
        h_ref[...] = (y * (1.0 + mod_ref[1:2, :]) + mod_ref[0:1, :]).astype(BF16)

    o_ref[...] = jnp.dot(h_ref[...], w_ref[...], preferred_element_type=F32).astype(o_ref.dtype)


def _in_call(x2, mod3, g_mix, w_in_bf, seq):
    t, d = x2.shape
    n_out = w_in_bf.shape[1] // 1024
    tm = min(1024, seq)
    per_b = seq // tm
    return pl.pallas_call(
        _in_kernel,
        grid=(t // tm, n_out),
        in_specs=[pl.BlockSpec((tm, d), lambda i, j: (i, 0)),
                  pl.BlockSpec((None, 6, d), lambda i, j: (i // per_b, 0, 0)),
                  pl.BlockSpec((1, d), lambda i, j: (0, 0)),
                  pl.BlockSpec((d, 1024), lambda i, j: (0, j))],
        out_specs=pl.BlockSpec((None, tm, 1024), lambda i, j: (j, i, 0)),
        out_shape=jax.ShapeDtypeStruct((n_out, t, 1024), BF16),
        scratch_shapes=[pltpu.VMEM((tm, d), BF16)],
        compiler_params=_params(("parallel", "arbitrary")),
        name="in_proj",
    )(x2, mod3, g_mix.reshape(1, d), w_in_bf)


def _rope_kernel(pos_ref, invf_ref, sign_ref, cos_ref, sin_ref):
    ang = pos_ref[...].astype(F32) * invf_ref[...]
    cos_ref[...] = jnp.cos(ang)
    sin_ref[...] = jnp.sin(ang) * sign_ref[...]


def _rope_call(positions):
    t = positions.size
    tp = min(2048, t)
    half = HEAD_DIM // 2
    inv_freq = 1.0 / (ROPE_THETA ** (jnp.arange(0, HEAD_DIM, 2, dtype=F32) / HEAD_DIM))
    invf = jnp.concatenate([inv_freq, inv_freq]).reshape(1, HEAD_DIM)
    sign = jnp.concatenate([-jnp.ones((half,), F32), jnp.ones((half,), F32)]).reshape(1, HEAD_DIM)
    return pl.pallas_call(
        _rope_kernel,
        grid=(t // tp,),
        in_specs=[pl.BlockSpec((tp, 1), lambda i: (i, 0)),
                  pl.BlockSpec((1, HEAD_DIM), lambda i: (0, 0)),
                  pl.BlockSpec((1, HEAD_DIM), lambda i: (0, 0))],
        out_specs=[pl.BlockSpec((tp, HEAD_DIM), lambda i: (i, 0)),
                   pl.BlockSpec((tp, HEAD_DIM), lambda i: (i, 0))],
        out_shape=[jax.ShapeDtypeStruct((t, HEAD_DIM), F32)] * 2,
        compiler_params=_params(("parallel",)),
        name="rope_tables",
    )(positions.reshape(t, 1), invf, sign)


def _ret_tables():
    h = np.arange(RET_HEADS, dtype=np.float64)
    lg = np.log1p(-np.exp2(-5.0 - h))
    idx = np.arange(CHUNK, dtype=np.float64)
    diff = idx[:, None] - idx[None, :]
    intra = np.where(diff >= 0, np.exp(lg[:, None, None] * np.maximum(diff, 0.0)), 0.0)
    kdec = np.exp(lg[:, None] * (CHUNK - 1.0 - idx)[None, :])
    qdec = np.exp(lg[:, None] * (idx + 1.0)[None, :])
    cdec = np.exp(lg * CHUNK)
    bc = lambda a: np.ascontiguousarray(np.broadcast_to(a[:, :, None], (RET_HEADS, CHUNK, HEAD_DIM)))
    return (intra.astype(np.float32), bc(qdec).astype(np.float32), bc(kdec).astype(np.float32),
            [float(v) for v in cdec])


def _ret_kernel(q_ref, k_ref, v_ref, g_ref, cos_ref, sin_ref, intra_ref, qdec_ref, kdec_ref,
                beta_ref, o_ref, st_ref, *, cdec):
    @pl.when(pl.program_id(1) == 0)
    def _():
        st_ref[...] = jnp.zeros_like(st_ref)

    scale = HEAD_DIM ** -0.5
    nt = (((1,), (1,)), ((), ()))
    tn = (((0,), (0,)), ((), ()))
    n_chunks = q_ref.shape[0] // CHUNK
    outs = [[] for _ in range(n_chunks)]
    for h in range(RET_HEADS):
        sl = slice(h * HEAD_DIM, (h + 1) * HEAD_DIM)
        st = st_ref[h]
        for c in range(n_chunks):
            rows = slice(c * CHUNK, (c + 1) * CHUNK)
            cos = cos_ref[rows, :]
            sin = sin_ref[rows, :]
            q = q_ref[rows, sl].astype(F32)
            k = k_ref[rows, sl].astype(F32)
            v = v_ref[rows, sl]
            qr = (q * cos + pltpu.roll(q, HEAD_DIM // 2, 1) * sin) * scale
            kr = k * cos + pltpu.roll(k, HEAD_DIM // 2, 1) * sin
            s = lax.dot_general(qr.astype(BF16), kr.astype(BF16), nt,
                                preferred_element_type=F32) * intra_ref[h]
            inner = jnp.dot(s.astype(BF16), v, preferred_element_type=F32)
            cross = jnp.dot((qr * qdec_ref[h]).astype(BF16), st.astype(BF16),
                            preferred_element_type=F32)
            o = inner + cross
            kv = lax.dot_general((kr * kdec_ref[h]).astype(BF16), v, tn,
                                 preferred_element_type=F32)
            st = cdec[h] * st + kv
            oc = o - jnp.mean(o, axis=-1, keepdims=True)
            on = oc * lax.rsqrt(jnp.mean(oc * oc, axis=-1, keepdims=True) + EPS)
            outs[c].append(_silu(g_ref[rows, sl].astype(F32)) * on)
        st_ref[h] = st
    for c in range(n_chunks):
        ret = jnp.concatenate(outs[c], axis=1)
        o_ref[c * CHUNK:(c + 1) * CHUNK, :] = (_rms(ret) * beta_ref[...]).astype(o_ref.dtype)


def _ret_call(proj, cos, sin, beta_ret, batch, seq):
    t = batch * seq
    rw = RET_HEADS * HEAD_DIM
    tr = min(4 * CHUNK, seq)
    nc = seq // tr
    intra, qdec, kdec, cdec = _ret_tables()
    row = lambda which: pl.BlockSpec((None, tr, rw), lambda b, n: (which, b * nc + n, 0))
    tab = lambda: pl.BlockSpec((RET_HEADS, CHUNK, HEAD_DIM), lambda b, n: (0, 0, 0))
    cs = lambda: pl.BlockSpec((tr, HEAD_DIM), lambda b, n: (b * nc + n, 0))
    return pl.pallas_call(
        functools.partial(_ret_kernel, cdec=cdec),
        grid=(batch, nc),
        in_specs=[row(0), row(1), row(2), row(3), cs(), cs(), tab(), tab(), tab(),
                  pl.BlockSpec((1, rw), lambda b, n: (0, 0))],
        out_specs=pl.BlockSpec((tr, rw), lambda b, n: (b * nc + n, 0)),
        out_shape=jax.ShapeDtypeStruct((t, rw), BF16),
        scratch_shapes=[pltpu.VMEM((RET_HEADS, HEAD_DIM, HEAD_DIM), F32)],
        compiler_params=_params(("parallel", "arbitrary")),
        name="retention",
    )(proj, proj, proj, proj, cos, sin, jnp.asarray(intra), jnp.asarray(qdec), jnp.asarray(kdec),
      beta_ret.reshape(1, rw))


def _lru_kernel(xr_ref, yr_ref, wc_ref, bc_ref, wg_ref, ba_ref, bx_ref, lam_ref, beta_ref,
                o_ref, xtail_ref, h_ref, *, ts):
    @pl.when(pl.program_id(1) == 0)
    def _():
        xtail_ref[...] = jnp.zeros_like(xtail_ref)
        h_ref[...] = jnp.zeros_like(h_ref)

    x = xr_ref[...].astype(F32)
    w = x.shape[1]
    rows = lax.broadcasted_iota(I32, (ts, w), 0)
    head_rows = lax.broadcasted_iota(I32, (SUBLANES, w), 0)
    tail = xtail_ref[...]
    xc = bc_ref[...]
    for tap in range(CONV_WIDTH):
        back = CONV_WIDTH - 1 - tap
        if back == 0:
            xs = x
        else:
            rolled = pltpu.roll(x, back, 0)
            head = jnp.where(head_rows < back, pltpu.roll(tail, back, 0), rolled[:SUBLANES])
            xs = jnp.concatenate([head, rolled[SUBLANES:]], axis=0)
        xc = xc + xs * wc_ref[tap:tap + 1, :]
    xtail_ref[...] = x[ts - SUBLANES:ts, :]

    xcb = xc.astype(BF16)
    bd = w // LRU_BLOCKS
    rs, gs = [], []
    for g in range(LRU_BLOCKS):
        z = jnp.dot(xcb[:, g * bd:(g + 1) * bd], wg_ref[g], preferred_element_type=F32)
        rs.append(z[:, :bd])
        gs.append(z[:, bd:])
    r = jax.nn.sigmoid(jnp.concatenate(rs, axis=1) + ba_ref[...])
    i = jax.nn.sigmoid(jnp.concatenate(gs, axis=1) + bx_ref[...])
    log_a = -LRU_C * r * jax.nn.softplus(-lam_ref[...])
    a = jnp.exp(log_a)
    bt = jnp.sqrt(-jnp.tanh(log_a) * (a * a + 1.0)) * (i * xc)

    acc_a, acc_b = a, bt
    sh = 1
    while sh < ts:
        if sh < SUBLANES:
            m = rows >= sh
            acc_b = jnp.where(m, acc_a * pltpu.roll(acc_b, sh, 0) + acc_b, acc_b)
            acc_a = jnp.where(m, acc_a * pltpu.roll(acc_a, sh, 0), acc_a)
        else:
            new_b = acc_a[sh:] * acc_b[:ts - sh] + acc_b[sh:]
            acc_a = jnp.concatenate([acc_a[:sh], acc_a[sh:] * acc_a[:ts - sh]], axis=0)
            acc_b = jnp.concatenate([acc_b[:sh], new_b], axis=0)
        sh *= 2
    hs = acc_a * h_ref[...] + acc_b
    h_ref[...] = hs[ts - 1:ts, :]

    lru = hs * jax.nn.gelu(yr_ref[...].astype(F32), approximate=True)
    o_ref[...] = (_rms(lru) * beta_ref[...]).astype(o_ref.dtype)


def _lru_call(proj, w_conv, b_conv, w_rg_a, b_rg_a, w_rg_x, b_rg_x, lam, beta_lru, batch, seq):
    t = batch * seq
    w = w_conv.shape[1]
    ts = min(256, seq)
    nt = seq // ts
    wg = jnp.concatenate([w_rg_a, w_rg_x], axis=-1).astype(BF16)
    vec = lambda: pl.BlockSpec((1, w), lambda b, j: (0, 0))
    row = lambda which: pl.BlockSpec((None, ts, w), lambda b, j: (which, b * nt + j, 0))
    return pl.pallas_call(
        functools.partial(_lru_kernel, ts=ts),
        grid=(batch, nt),
        in_specs=[row(4), row(5),
                  pl.BlockSpec((CONV_WIDTH, w), lambda b, j: (0, 0)), vec(),
                  pl.BlockSpec(wg.shape, lambda b, j: (0, 0, 0)),
                  vec(), vec(), vec(), vec()],
        out_specs=pl.BlockSpec((ts, w), lambda b, j: (b * nt + j, 0)),
        out_shape=jax.ShapeDtypeStruct((t, w), BF16),
        scratch_shapes=[pltpu.VMEM((SUBLANES, w), F32), pltpu.VMEM((1, w), F32)],
        compiler_params=_params(("parallel", "arbitrary")),
        name="rg_lru",
    )(proj, proj, w_conv, b_conv.reshape(1, w), wg, b_rg_a.reshape(1, w), b_rg_x.reshape(1, w),
      lam.reshape(1, w), beta_lru.reshape(1, w))


def _out_kernel(ret_ref, lru_ref, x_ref, mod_ref, g_ref, w_ref, x1_ref, h2_ref):
    rw = ret_ref.shape[1]
    mix = jnp.dot(ret_ref[...], w_ref[0:rw, :], preferred_element_type=F32)
    mix = mix + jnp.dot(lru_ref[...], w_ref[rw:, :], preferred_element_type=F32)
    x1 = x_ref[...] + mod_ref[2:3, :] * mix
    x1_ref[...] = x1
    h2 = _rms(x1) * g_ref[...] * (1.0 + mod_ref[4:5, :]) + mod_ref[3:4, :]
    half = h2.shape[1] // 2
    _store_rows(h2_ref, _pack2(h2[:, :half], h2[:, half:]))


def _out_call(ret, lru, x2, mod3, g_moe, w_out_bf, seq):
    t, d = x2.shape
    assert d // 2 == SUBLANES * LANES, "packed token rows are exactly one (8, 128) word tile"
    rw, lw = ret.shape[1], lru.shape[1]
    tm = min(512, seq)
    per_b = seq // tm
    return pl.pallas_call(
        _out_kernel,
        grid=(t // tm,),
        in_specs=[pl.BlockSpec((tm, rw), lambda i: (i, 0)),
                  pl.BlockSpec((tm, lw), lambda i: (i, 0)),
                  pl.BlockSpec((tm, d), lambda i: (i, 0)),
                  pl.BlockSpec((None, 6, d), lambda i: (i // per_b, 0, 0)),
                  pl.BlockSpec((1, d), lambda i: (0, 0)),
                  pl.BlockSpec((rw + lw, d), lambda i: (0, 0))],
        out_specs=[pl.BlockSpec((tm, d), lambda i: (i, 0)),
                   pl.BlockSpec((tm * SUBLANES, LANES), lambda i: (i, 0))],
        out_shape=[jax.ShapeDtypeStruct((t, d), F32),
                   jax.ShapeDtypeStruct((t * SUBLANES, LANES), U32)],
        compiler_params=_params(("parallel",)),
        name="out_proj",
    )(ret, lru, x2, mod3, g_moe.reshape(1, d), w_out_bf)


def _route_kernel(h_ref, wlo_ref, whi_ref, bias_ref, tri_ref, eidx_ref, wsel_ref, pos_ref,
                  cnt_ref, carry_ref):
    @pl.when(pl.program_id(0) == 0)
    def _():
        carry_ref[...] = jnp.zeros_like(carry_ref)

    tr = h_ref.shape[0] // SUBLANES
    lo, hi = _unpack2(_load_rows(h_ref))
    nt = (((1,), (1,)), ((), ()))
    logits = lax.dot_general(wlo_ref[...], lo.astype(BF16), nt, preferred_element_type=F32)
    logits = logits + lax.dot_general(whi_ref[...], hi.astype(BF16), nt,
                                      preferred_element_type=F32)
    scores = jax.nn.sigmoid(logits)
    biased = scores + bias_ref[:, 0:1]
    shape3 = (N_GROUPS, GROUP_SIZE, tr)
    s3 = scores.reshape(shape3)
    b3 = biased.reshape(shape3)
    member = lax.broadcasted_iota(I32, shape3, 1)
    group = lax.broadcasted_iota(I32, shape3, 0)
    expert = group * GROUP_SIZE + member

    m1 = jnp.max(b3, axis=1, keepdims=True)
    i1 = jnp.min(jnp.where(b3 == m1, member, GROUP_SIZE), axis=1, keepdims=True)
    m2 = jnp.max(jnp.where(member == i1, NEG_INF, b3), axis=1, keepdims=True)
    gscore = m1 + m2

    gid = lax.broadcasted_iota(I32, (N_GROUPS, 1, tr), 0)
    rank = jnp.zeros((N_GROUPS, 1, tr), I32)
    for j in range(N_GROUPS):
        gj = gscore[j:j + 1]
        ahead = (gj > gscore) | ((gj == gscore) & (gid > j))
        rank = rank + ahead.astype(I32)
    masked = jnp.where(rank < TOPK_GROUPS, b3, NEG_INF)

    idxs, vals = [], []
    picked = jnp.zeros(shape3, F32)
    for _ in range(TOP_K):
        m = jnp.max(jnp.max(masked, axis=0, keepdims=True), axis=1, keepdims=True)
        cand = jnp.where(masked == m, expert, N_EXPERTS)
        idx = jnp.min(jnp.min(cand, axis=0, keepdims=True), axis=1, keepdims=True)
        hit = expert == idx
        val = jnp.sum(jnp.sum(jnp.where(hit, s3, 0.0), axis=0, keepdims=True), axis=1, keepdims=True)
        masked = jnp.where(hit, NEG_INF, masked)
        picked = jnp.where(hit, 1.0, picked)
        idxs.append(idx)
        vals.append(val)
    total = vals[0]
    for v in vals[1:]:
        total = total + v

    before = jnp.dot(picked.reshape(N_EXPERTS, tr).astype(BF16), tri_ref[...],
                     preferred_element_type=F32)
    carry = carry_ref[...]
    posm = (before + carry[:, 0:1]).reshape(shape3)
    carry = carry + jnp.sum(picked.reshape(N_EXPERTS, tr), axis=1, keepdims=True)
    carry_ref[...] = carry
    cnt_ref[...] = carry

    for k in range(TOP_K):
        hit = expert == idxs[k]
        p = jnp.sum(jnp.sum(jnp.where(hit, posm, 0.0), axis=0, keepdims=True), axis=1, keepdims=True)
        eidx_ref[k:k + 1, :] = idxs[k].reshape(1, tr)
        wsel_ref[k:k + 1, :] = (vals[k] / total * ROUTED_SCALE).reshape(1, tr)
        pos_ref[k:k + 1, :] = p.reshape(1, tr).astype(I32)


def _route_call(h2p, w_router, b_router):
    t = h2p.shape[0] // SUBLANES
    half = SUBLANES * LANES
    tr = min(512, t)
    wt = w_router.T.astype(BF16)
    bias = jnp.broadcast_to(b_router.astype(F32)[:, None], (N_EXPERTS, LANES))
    tri = jnp.asarray(np.triu(np.ones((tr, tr), np.float32), k=1), dtype=BF16)
    kt = lambda dt: jax.ShapeDtypeStruct((TOP_K, t), dt)
    krow = lambda: pl.BlockSpec((TOP_K, tr), lambda i: (0, i))
    return pl.pallas_call(
        _route_kernel,
        grid=(t // tr,),
        in_specs=[pl.BlockSpec((tr * SUBLANES, LANES), lambda i: (i, 0)),
                  pl.BlockSpec((N_EXPERTS, half), lambda i: (0, 0)),
                  pl.BlockSpec((N_EXPERTS, half), lambda i: (0, 0)),
                  pl.BlockSpec((N_EXPERTS, LANES), lambda i: (0, 0)),
                  pl.BlockSpec((tr, tr), lambda i: (0, 0))],
        out_specs=[krow(), krow(), krow(),
                   pl.BlockSpec((N_EXPERTS, LANES), lambda i: (0, 0))],
        out_shape=[kt(I32), kt(F32), kt(I32), jax.ShapeDtypeStruct((N_EXPERTS, LANES), F32)],
        scratch_shapes=[pltpu.VMEM((N_EXPERTS, LANES), F32)],
        compiler_params=_params(("arbitrary",)),
        name="router",
    )(h2p, wt[:, :half], wt[:, half:], bias, tri)


def _dest_kernel(pst_ref, eidx_ref, pos_ref, o_ref):
    e = eidx_ref[...]
    acc = pos_ref[...]
    for j in range(N_EXPERTS):
        acc = acc + jnp.where(e == j, pst_ref[j], 0)
    o_ref[...] = acc


def _dest_call(pstarts, eidx, pos):
    k, t = eidx.shape
    tb = min(2048, t)
    blk = lambda: pl.BlockSpec((k, tb), lambda i, pst: (0, i))
    return pl.pallas_call(
        _dest_kernel,
        grid_spec=pltpu.PrefetchScalarGridSpec(
            num_scalar_prefetch=1, grid=(t // tb,),
            in_specs=[blk(), blk()], out_specs=blk()),
        out_shape=jax.ShapeDtypeStruct((k, t), I32),
        compiler_params=_params(("parallel",)),
        name="dest_rows",
    )(pstarts, eidx, pos)


def _swiglu_packed(x_ref, w1_ref, w2_ref):
    xw = _load_rows(x_ref)
    half = xw.shape[1]
    lo, hi = _unpack2(xw)
    gu = jnp.dot(lo.astype(BF16), w1_ref[0:half, :], preferred_element_type=F32)
    gu = gu + jnp.dot(hi.astype(BF16), w1_ref[half:, :], preferred_element_type=F32)
    hid = gu.shape[1] // 2
    act = _silu(gu[:, :hid]) * gu[:, hid:]
    return jnp.dot(act.astype(BF16), w2_ref[...], preferred_element_type=F32)


def _dispatch_kernel(meta_ref, dest_ref, h_ref, w1_ref, w2_ref, xs_ref, sh_ref, zero_ref,
                     sem_z, sem_r, *, td):

    @pl.when(pl.program_id(0) == 0)
    def _():
        _zero_rows(zero_ref)

        blk_rows = MOE_BLOCK * SUBLANES

        def block_copy(first_token):
            start = pl.multiple_of(first_token * SUBLANES, blk_rows)
            return pltpu.make_async_copy(zero_ref, xs_ref.at[pl.ds(start, blk_rows)], sem_z)

        def tail_copy(e):
            return block_copy(meta_ref[0, e] + meta_ref[2, e] - MOE_BLOCK)

        def unused_copy(j):
            return block_copy((meta_ref[3, 0] + j) * MOE_BLOCK)

        n_blocks = xs_ref.shape[0] // blk_rows
        for e in range(N_EXPERTS):
            @pl.when(meta_ref[2, e] > meta_ref[1, e])
            def _():
                tail_copy(e).start()

            @pl.when(meta_ref[3, 0] + e < n_blocks)
            def _():
                unused_copy(e).start()
        for e in range(N_EXPERTS):
            @pl.when(meta_ref[2, e] > meta_ref[1, e])
            def _():
                tail_copy(e).wait()

            @pl.when(meta_ref[3, 0] + e < n_blocks)
            def _():
                unused_copy(e).wait()

    def issue(r, carry):
        src = _token(h_ref, r)
        for k in range(TOP_K):
            pltpu.make_async_copy(src, _token(xs_ref, dest_ref[k, r]), sem_r).start(priority=k % 2)
        return carry

    lax.fori_loop(0, td, issue, 0)

    sh_ref[...] = _swiglu_packed(h_ref, w1_ref, w2_ref).astype(sh_ref.dtype)

    for k in range(TOP_K):
        pltpu.make_async_copy(h_ref, xs_ref.at[pl.ds(0, td * SUBLANES)], sem_r).wait()


def _dispatch_call(meta, dest, h2p, w_sh_in_bf, w_sh_out_bf, n_rows):
    t = h2p.shape[0] // SUBLANES
    d = 2 * SUBLANES * LANES
    td = min(256, t)
    return pl.pallas_call(
        functools.partial(_dispatch_kernel, td=td),
        grid=(t // td,),
        in_specs=[pl.BlockSpec(memory_space=pltpu.SMEM),
                  pl.BlockSpec((TOP_K, td), lambda i: (0, i), memory_space=pltpu.SMEM),
                  pl.BlockSpec((td * SUBLANES, LANES), lambda i: (i, 0)),
                  pl.BlockSpec(w_sh_in_bf.shape, lambda i: (0, 0)),
                  pl.BlockSpec(w_sh_out_bf.shape, lambda i: (0, 0))],
        out_specs=[pl.BlockSpec(memory_space=pl.ANY),
                   pl.BlockSpec((td, d), lambda i: (i, 0))],
        out_shape=[jax.ShapeDtypeStruct((n_rows * SUBLANES, LANES), U32),
                   jax.ShapeDtypeStruct((t, d), BF16)],
        scratch_shapes=[pltpu.VMEM((MOE_BLOCK * SUBLANES, LANES), U32),
                        pltpu.SemaphoreType.DMA, pltpu.SemaphoreType.DMA],
        compiler_params=_params(("arbitrary",)),
        name="dispatch_shared",
    )(meta, dest, h2p, w_sh_in_bf, w_sh_out_bf)


def _expert_kernel(meta_ref, xs_ref, w1_hbm, w2_hbm, ys_ref, w1f, w2f, w1b_ref, w2b_ref, xbuf, ybuf,
                   sem_w, sem_in, sem_out, *, n_blocks):
    e = pl.program_id(0)
    n_exp = pl.num_programs(0)
    blk_rows = MOE_BLOCK * SUBLANES
    wslot = lax.rem(e, 2)
    low = 1

    def weights(ex, s):
        return (pltpu.make_async_copy(w1_hbm.at[ex], w1f.at[s], sem_w.at[0, s]),
                pltpu.make_async_copy(w2_hbm.at[ex], w2f.at[s], sem_w.at[1, s]))

    @pl.when(e == 0)
    def _():
        for cp in weights(0, 0):
            cp.start()

    for cp in weights(e, wslot):
        cp.wait()

    @pl.when(e + 1 < n_exp)
    def _():
        for cp in weights(e + 1, 1 - wslot):
            cp.start(priority=low)
    first = meta_ref[0, e] // MOE_BLOCK
    nblk = meta_ref[2, e] // MOE_BLOCK
    nfull = nblk // 2

    def rows_of(b):
        return pl.ds(pl.multiple_of((first + b) * blk_rows, blk_rows), blk_rows)

    def fetch(b, s, base=None):
        src = rows_of(b) if base is None else pl.ds(
            pl.multiple_of((base + b) * blk_rows, blk_rows), blk_rows)
        return pltpu.make_async_copy(xs_ref.at[src], xbuf.at[s], sem_in.at[s])

    def store(b, s):
        return pltpu.make_async_copy(ybuf.at[s], ys_ref.at[rows_of(b)], sem_out.at[s])

    def compute(s):
        out = _swiglu_packed(xbuf.at[s], w1b_ref, w2b_ref)
        half = out.shape[1] // 2
        _store_rows(ybuf.at[s], _pack2(out[:, :half], out[:, half:]))

    odd = lax.rem(nblk, 2) == 1
    tail_slot = 4
    row_queue = 0

    @pl.when(nblk > 0)
    def _():
        @pl.when(jnp.logical_and(nfull > 0, e == 0))
        def _():
            fetch(0, 0).start(priority=row_queue)
            fetch(1, 1).start(priority=row_queue)

        @pl.when(odd)
        def _():
            fetch(nblk - 1, tail_slot).start(priority=row_queue)

        w1b_ref[...] = w1f[wslot].astype(BF16)
        w2b_ref[...] = w2f[wslot].astype(BF16)

        def pair(p, carry):
            s0 = 2 * lax.rem(p, 2)
            n0 = 2 - s0
            fetch(2 * p, s0).wait()
            fetch(2 * p + 1, s0 + 1).wait()

            @pl.when(p + 1 < nfull)
            def _():
                fetch(2 * p + 2, n0).start(priority=row_queue)
                fetch(2 * p + 3, n0 + 1).start(priority=row_queue)

            @pl.when(p >= 2)
            def _():
                store(2 * p - 4, s0).wait()
                store(2 * p - 3, s0 + 1).wait()

            compute(s0)
            compute(s0 + 1)
            store(2 * p, s0).start(priority=row_queue)
            store(2 * p + 1, s0 + 1).start(priority=row_queue)
            return carry

        lax.fori_loop(0, nfull, pair, 0)

        @pl.when(odd)
        def _():
            fetch(nblk - 1, tail_slot).wait()
            compute(tail_slot)
            store(nblk - 1, tail_slot).start(priority=row_queue)

        for back in (2, 1):
            @pl.when(nfull >= back)
            def _():
                q = nfull - back
                s0 = 2 * lax.rem(q, 2)
                store(2 * q, s0).wait()
                store(2 * q + 1, s0 + 1).wait()

        @pl.when(odd)
        def _():
            store(nblk - 1, tail_slot).wait()

    @pl.when(e + 1 < n_exp)
    def _():
        nxt = jnp.minimum(e + 1, n_exp - 1)
        nxt_first = meta_ref[0, nxt] // MOE_BLOCK

        @pl.when(meta_ref[2, nxt] // MOE_BLOCK >= 2)
        def _():
            fetch(0, 0, base=nxt_first).start(priority=row_queue)
            fetch(1, 1, base=nxt_first).start(priority=row_queue)

    @pl.when(e == pl.num_programs(0) - 1)
    def _():
        _zero_rows(ybuf.at[0])

        def unused(j):
            start = pl.multiple_of((meta_ref[3, 0] + j) * blk_rows, blk_rows)
            return pltpu.make_async_copy(ybuf.at[0], ys_ref.at[pl.ds(start, blk_rows)],
                                         sem_out.at[0])

        for j in range(N_EXPERTS):
            @pl.when(meta_ref[3, 0] + j < n_blocks)
            def _():
                unused(j).start()
        for j in range(N_EXPERTS):
            @pl.when(meta_ref[3, 0] + j < n_blocks)
            def _():
                unused(j).wait()


def _expert_call(meta, xs, w_exp_in, w_exp_out):
    blk_rows = MOE_BLOCK * SUBLANES
    n_exp, d, h2 = w_exp_in.shape
    hid = w_exp_out.shape[1]
    n_blocks = xs.shape[0] // blk_rows
    return pl.pallas_call(
        functools.partial(_expert_kernel, n_blocks=n_blocks),
        grid_spec=pltpu.PrefetchScalarGridSpec(
            num_scalar_prefetch=1, grid=(n_exp,),
            in_specs=[pl.BlockSpec(memory_space=pl.ANY),
                      pl.BlockSpec(memory_space=pl.ANY),
                      pl.BlockSpec(memory_space=pl.ANY)],
            out_specs=pl.BlockSpec(memory_space=pl.ANY),
            scratch_shapes=[pltpu.VMEM((2, d, h2), F32), pltpu.VMEM((2, hid, d), F32),
                            pltpu.VMEM((d, h2), BF16), pltpu.VMEM((hid, d), BF16),
                            pltpu.VMEM((5, blk_rows, LANES), U32),
                            pltpu.VMEM((5, blk_rows, LANES), U32),
                            pltpu.SemaphoreType.DMA((2, 2)),
                            pltpu.SemaphoreType.DMA((5,)), pltpu.SemaphoreType.DMA((5,))]),
        out_shape=jax.ShapeDtypeStruct(xs.shape, U32),
        compiler_params=_params(("arbitrary",)),
        name="experts",
    )(meta, xs, w_exp_in, w_exp_out)


def _combine_kernel(dest_ref, dnext_ref, w_ref, ys_ref, sh_ref, x1_ref, mod_ref, g_ref, o_ref,
                    buf_ref, sem, *, tc, final_norm):
    i = pl.program_id(0)
    slot = lax.rem(i, 2)

    def gather(d_ref, s):
        def issue(r, carry):
            for k in range(TOP_K):
                pltpu.make_async_copy(_token(ys_ref, d_ref[k, r]), _token(buf_ref.at[s, k], r),
                                      sem.at[s]).start(priority=k % 2)
            return carry

        lax.fori_loop(0, tc, issue, 0)

    @pl.when(i == 0)
    def _():
        gather(dest_ref, 0)

    @pl.when(i + 1 < pl.num_programs(0))
    def _():
        gather(dnext_ref, 1 - slot)

    for k in range(TOP_K):
        pltpu.make_async_copy(ys_ref.at[pl.ds(0, tc * SUBLANES)], buf_ref.at[slot, k],
                              sem.at[slot]).wait()

    acc_lo = acc_hi = None
    for k in range(TOP_K):
        wk = w_ref[:, k:k + 1]
        lo, hi = _unpack2(_load_rows(buf_ref.at[slot, k]))
        acc_lo = wk * lo if acc_lo is None else acc_lo + wk * lo
        acc_hi = wk * hi if acc_hi is None else acc_hi + wk * hi
    routed = jnp.concatenate([acc_lo, acc_hi], axis=1)
    moe = routed + sh_ref[...].astype(F32)
    x2 = x1_ref[...] + mod_ref[5:6, :] * moe
    o_ref[...] = _rms(x2) * g_ref[...] if final_norm else x2


def _combine_call(dest, wsel_t, ys, shared, x1, mod3, g_final, seq, final_norm):
    t, d = x1.shape
    tc = min(256, seq)
    per_b = seq // tc
    n_tiles = t // tc
    return pl.pallas_call(
        functools.partial(_combine_kernel, tc=tc, final_norm=final_norm),
        grid=(n_tiles,),
        in_specs=[pl.BlockSpec((TOP_K, tc), lambda i: (0, i), memory_space=pltpu.SMEM),
                  pl.BlockSpec((TOP_K, tc), lambda i: (0, jnp.minimum(i + 1, n_tiles - 1)),
                               memory_space=pltpu.SMEM),
                  pl.BlockSpec((tc, TOP_K), lambda i: (i, 0)),
                  pl.BlockSpec(memory_space=pl.ANY),
                  pl.BlockSpec((tc, d), lambda i: (i, 0)),
                  pl.BlockSpec((tc, d), lambda i: (i, 0)),
                  pl.BlockSpec((None, 6, d), lambda i: (i // per_b, 0, 0)),
                  pl.BlockSpec((1, d), lambda i: (0, 0))],
        out_specs=pl.BlockSpec((tc, d), lambda i: (i, 0)),
        out_shape=jax.ShapeDtypeStruct((t, d), F32),
        scratch_shapes=[pltpu.VMEM((2, TOP_K, tc * SUBLANES, LANES), U32),
                        pltpu.SemaphoreType.DMA((2,))],
        compiler_params=_params(("arbitrary",)),
        name="combine_final",
    )(dest, dest, wsel_t, ys, shared, x1, mod3, g_final.reshape(1, d))


def _moe_plan(counts):
    pcounts = (counts + MOE_BLOCK - 1) // MOE_BLOCK * MOE_BLOCK
    pends = jnp.cumsum(pcounts)
    pstarts = pends - pcounts
    return pstarts.astype(I32), pcounts.astype(I32), pends.astype(I32)


def kernel(x, c, positions, w_ada, b_ada, g_mix, w_in, w_conv, b_conv, w_rg_a, b_rg_a, w_rg_x, b_rg_x, lam, beta_ret, beta_lru, w_out, g_moe, w_router, b_router, w_exp_in, w_exp_out, w_sh_in, w_sh_out, g_final):
    batch, seq, d = x.shape
    t = batch * seq
    depth = w_ada.shape[0]
    xcur = x.reshape(t, d)
    cos, sin = _rope_call(positions)
    n_rows = (t * TOP_K + MOE_BLOCK - 1) // MOE_BLOCK * MOE_BLOCK + N_EXPERTS * MOE_BLOCK
    for l in range(depth):
        mod3 = _mod_call(c, w_ada[l], b_ada[l]).reshape(batch, 6, d)
        proj = _in_call(xcur, mod3, g_mix[l], w_in[l].astype(BF16), seq)
        ret = _ret_call(proj, cos, sin, beta_ret[l], batch, seq)
        lru = _lru_call(proj, w_conv[l], b_conv[l], w_rg_a[l], b_rg_a[l], w_rg_x[l], b_rg_x[l],
                        lam[l], beta_lru[l], batch, seq)
        x1, h2p = _out_call(ret, lru, xcur, mod3, g_moe[l], w_out[l].astype(BF16), seq)

        eidx, wsel, pos, cnt = _route_call(h2p, w_router[l], b_router[l])
        counts = cnt[:, 0].astype(I32)
        pstarts, pcounts, pends = _moe_plan(counts)
        dest = _dest_call(pstarts, eidx, pos)
        n_used = (pends[-1] // MOE_BLOCK).astype(I32)
        meta = jnp.stack([pstarts, counts, pcounts, jnp.full_like(counts, n_used)])

        xs, shared = _dispatch_call(meta, dest, h2p, w_sh_in[l].astype(BF16),
                                    w_sh_out[l].astype(BF16), n_rows)
        ys = _expert_call(meta, xs, w_exp_in[l], w_exp_out[l])
        xcur = _combine_call(dest, wsel.T, ys, shared, x1, mod3, g_final, seq,
                             final_norm=(l == depth - 1))
    return xcur.reshape(batch, seq, d)
```

```python
import functools

import numpy as np
import jax
import jax.numpy as jnp
from jax import lax
from jax.experimental import pallas as pl
from jax.experimental.pallas import tpu as pltpu

F32 = jnp.float32
BF16 = jnp.bfloat16
I32 = jnp.int32
U32 = jnp.uint32

RET_HEADS = 8
HEAD_DIM = 128
CHUNK = 128
ROPE_THETA = 10000.0
LRU_BLOCKS = 8
CONV_WIDTH = 4
LRU_C = 8.0
N_EXPERTS = 64
N_GROUPS = 8
GROUP_SIZE = N_EXPERTS // N_GROUPS
TOPK_GROUPS = 4
TOP_K = 8
ROUTED_SCALE = 2.5
EPS = 1e-6

LANES = 128
SUBLANES = 8
VMEM_LIMIT_BYTES = 56 * 1024 * 1024

MOE_BLOCK = 256
NEG_INF = float("-inf")


def _params(semantics, vmem=VMEM_LIMIT_BYTES):
    return pltpu.CompilerParams(dimension_semantics=semantics, vmem_limit_bytes=vmem)


def _silu(x):
    return x * jax.nn.sigmoid(x)


def _rms(x):
    return x * lax.rsqrt(jnp.mean(x * x, axis=-1, keepdims=True) + EPS)


def _pack2(lo, hi):
    return pltpu.pack_elementwise([lo, hi], packed_dtype=jnp.bfloat16)


def _unpack2(w):
    lo = pltpu.unpack_elementwise(w, index=0, packed_dtype=jnp.bfloat16, unpacked_dtype=F32)
    hi = pltpu.unpack_elementwise(w, index=1, packed_dtype=jnp.bfloat16, unpacked_dtype=F32)
    return lo, hi


def _load_rows(ref):
    rows = ref.shape[0] // SUBLANES
    return jnp.concatenate([ref[pl.ds(j, rows, stride=SUBLANES), :] for j in range(SUBLANES)],
                           axis=1)


def _store_rows(ref, v):
    rows = ref.shape[0] // SUBLANES
    for j in range(SUBLANES):
        ref[pl.ds(j, rows, stride=SUBLANES), :] = v[:, j * LANES:(j + 1) * LANES]


def _zero_rows(ref):
    rows = ref.shape[0] // SUBLANES
    z = jnp.zeros((rows, LANES), F32)
    w = _pack2(z, z)
    for j in range(SUBLANES):
        ref[pl.ds(j, rows, stride=SUBLANES), :] = w


def _token(ref, r):
    return ref.at[pl.ds(pl.multiple_of(r * SUBLANES, SUBLANES), SUBLANES)]


def _mod_kernel(c_ref, w_ref, b_ref, o_ref):
    cs = _silu(c_ref[...])
    o_ref[...] = jnp.dot(cs.astype(BF16), w_ref[...].astype(BF16),
                         preferred_element_type=F32) + b_ref[...]


def _mod_call(c, w_ada, b_ada):
    b, d = c.shape
    n = w_ada.shape[1]
    tn = 1024
    return pl.pallas_call(
        _mod_kernel,
        grid=(n // tn,),
        in_specs=[pl.BlockSpec((b, d), lambda j: (0, 0)),
                  pl.BlockSpec((d, tn), lambda j: (0, j)),
                  pl.BlockSpec((1, tn), lambda j: (0, j))],
        out_specs=pl.BlockSpec((b, tn), lambda j: (0, j)),
        out_shape=jax.ShapeDtypeStruct((b, n), F32),
        compiler_params=_params(("parallel",)),
        name="adaln_mod",
    )(c, w_ada, b_ada.reshape(1, n))


def _in_kernel(x_ref, mod_ref, g_ref, w_ref, o_ref, h_ref):
    @pl.when(pl.program_id(1) == 0)
    def _():
        y = _rms(x_ref[...]) * g_ref[...]
        h_ref[...] = (y * (1.0 + mod_ref[1:2, :]) + mod_ref[0:1, :]).astype(BF16)

    o_ref[...] = jnp.dot(h_ref[...], w_ref[...], preferred_element_type=F32).astype(o_ref.dtype)


def _in_call(x2, mod3, g_mix, w_in_bf, seq):
    t, d = x2.shape
    n_out = w_in_bf.shape[1] // 1024
    tm = min(1024, seq)
    per_b = seq // tm
    return pl.pallas_call(
        _in_kernel,
        grid=(t // tm, n_out),
        in_specs=[pl.BlockSpec((tm, d), lambda i, j: (i, 0)),
                  pl.BlockSpec((None, 6, d), lambda i, j: (i // per_b, 0, 0)),
                  pl.BlockSpec((1, d), lambda i, j: (0, 0)),
                  pl.BlockSpec((d, 1024), lambda i, j: (0, j))],
        out_specs=pl.BlockSpec((None, tm, 1024), lambda i, j: (j, i, 0)),
        out_shape=jax.ShapeDtypeStruct((n_out, t, 1024), BF16),
        scratch_shapes=[pltpu.VMEM((tm, d), BF16)],
        compiler_params=_params(("parallel", "arbitrary")),
        name="in_proj",
    )(x2, mod3, g_mix.reshape(1, d), w_in_bf)


def _rope_kernel(pos_ref, invf_ref, sign_ref, cos_ref, sin_ref):
    ang = pos_ref[...].astype(F32) * invf_ref[...]
    cos_ref[...] = jnp.cos(ang)
    sin_ref[...] = jnp.sin(ang) * sign_ref[...]


def _rope_call(positions):
    t = positions.size
    tp = min(2048, t)
    half = HEAD_DIM // 2
    inv_freq = 1.0 / (ROPE_THETA ** (jnp.arange(0, HEAD_DIM, 2, dtype=F32) / HEAD_DIM))
    invf = jnp.concatenate([inv_freq, inv_freq]).reshape(1, HEAD_DIM)
    sign = jnp.concatenate([-jnp.ones((half,), F32), jnp.ones((half,), F32)]).reshape(1, HEAD_DIM)
    return pl.pallas_call(
        _rope_kernel,
        grid=(t // tp,),
        in_specs=[pl.BlockSpec((tp, 1), lambda i: (i, 0)),
                  pl.BlockSpec((1, HEAD_DIM), lambda i: (0, 0)),
                  pl.BlockSpec((1, HEAD_DIM), lambda i: (0, 0))],
        out_specs=[pl.BlockSpec((tp, HEAD_DIM), lambda i: (i, 0)),
                   pl.BlockSpec((tp, HEAD_DIM), lambda i: (i, 0))],
        out_shape=[jax.ShapeDtypeStruct((t, HEAD_DIM), F32)] * 2,
        compiler_params=_params(("parallel",)),
        name="rope_tables",
    )(positions.reshape(t, 1), invf, sign)


def _ret_tables():
    h = np.arange(RET_HEADS, dtype=np.float64)
    lg = np.log1p(-np.exp2(-5.0 - h))
    idx = np.arange(CHUNK, dtype=np.float64)
    diff = idx[:, None] - idx[None, :]
    intra = np.where(diff >= 0, np.exp(lg[:, None, None] * np.maximum(diff, 0.0)), 0.0)
    kdec = np.exp(lg[:, None] * (CHUNK - 1.0 - idx)[None, :])
    qdec = np.exp(lg[:, None] * (idx + 1.0)[None, :])
    cdec = np.exp(lg * CHUNK)
    bc = lambda a: np.ascontiguousarray(np.broadcast_to(a[:, :, None], (RET_HEADS, CHUNK, HEAD_DIM)))
    return (intra.astype(np.float32), bc(qdec).astype(np.float32), bc(kdec).astype(np.float32),
            [float(v) for v in cdec])


def _ret_kernel(q_ref, k_ref, v_ref, g_ref, cos_ref, sin_ref, intra_ref, qdec_ref, kdec_ref,
                beta_ref, o_ref, st_ref, *, cdec):
    @pl.when(pl.program_id(1) == 0)
    def _():
        st_ref[...] = jnp.zeros_like(st_ref)

    scale = HEAD_DIM ** -0.5
    nt = (((1,), (1,)), ((), ()))
    tn = (((0,), (0,)), ((), ()))
    n_chunks = q_ref.shape[0] // CHUNK
    outs = [[] for _ in range(n_chunks)]
    for h in range(RET_HEADS):
        sl = slice(h * HEAD_DIM, (h + 1) * HEAD_DIM)
        st = st_ref[h]
        for c in range(n_chunks):
            rows = slice(c * CHUNK, (c + 1) * CHUNK)
            cos = cos_ref[rows, :]
            sin = sin_ref[rows, :]
            q = q_ref[rows, sl].astype(F32)
            k = k_ref[rows, sl].astype(F32)
            v = v_ref[rows, sl]
            qr = (q * cos + pltpu.roll(q, HEAD_DIM // 2, 1) * sin) * scale
            kr = k * cos + pltpu.roll(k, HEAD_DIM // 2, 1) * sin
            s = lax.dot_general(qr.astype(BF16), kr.astype(BF16), nt,
                                preferred_element_type=F32) * intra_ref[h]
            inner = jnp.dot(s.astype(BF16), v, preferred_element_type=F32)
            cross = jnp.dot((qr * qdec_ref[h]).astype(BF16), st.astype(BF16),
                            preferred_element_type=F32)
            o = inner + cross
            kv = lax.dot_general((kr * kdec_ref[h]).astype(BF16), v, tn,
                                 preferred_element_type=F32)
            st = cdec[h] * st + kv
            oc = o - jnp.mean(o, axis=-1, keepdims=True)
            on = oc * lax.rsqrt(jnp.mean(oc * oc, axis=-1, keepdims=True) + EPS)
            outs[c].append(_silu(g_ref[rows, sl].astype(F32)) * on)
        st_ref[h] = st
    for c in range(n_chunks):
        ret = jnp.concatenate(outs[c], axis=1)
        o_ref[c * CHUNK:(c + 1) * CHUNK, :] = (_rms(ret) * beta_ref[...]).astype(o_ref.dtype)


def _ret_call(proj, cos, sin, beta_ret, batch, seq):
    t = batch * seq
    rw = RET_HEADS * HEAD_DIM
    tr = min(4 * CHUNK, seq)
    nc = seq // tr
    intra, qdec, kdec, cdec = _ret_tables()
    row = lambda which: pl.BlockSpec((None, tr, rw), lambda b, n: (which, b * nc + n, 0))
    tab = lambda: pl.BlockSpec((RET_HEADS, CHUNK, HEAD_DIM), lambda b, n: (0, 0, 0))
    cs = lambda: pl.BlockSpec((tr, HEAD_DIM), lambda b, n: (b * nc + n, 0))
    return pl.pallas_call(
        functools.partial(_ret_kernel, cdec=cdec),
        grid=(batch, nc),
        in_specs=[row(0), row(1), row(2), row(3), cs(), cs(), tab(), tab(), tab(),
                  pl.BlockSpec((1, rw), lambda b, n: (0, 0))],
        out_specs=pl.BlockSpec((tr, rw), lambda b, n: (b * nc + n, 0)),
        out_shape=jax.ShapeDtypeStruct((t, rw), BF16),
        scratch_shapes=[pltpu.VMEM((RET_HEADS, HEAD_DIM, HEAD_DIM), F32)],
        compiler_params=_params(("parallel", "arbitrary")),
        name="retention",
    )(proj, proj, proj, proj, cos, sin, jnp.asarray(intra), jnp.asarray(qdec), jnp.asarray(kdec),
      beta_ret.reshape(1, rw))


def _lru_kernel(xr_ref, yr_ref, wc_ref, bc_ref, wg_ref, ba_ref, bx_ref, lam_ref, beta_ref,
                o_ref, xtail_ref, h_ref, *, ts):
    @pl.when(pl.program_id(1) == 0)
    def _():
        xtail_ref[...] = jnp.zeros_like(xtail_ref)
        h_ref[...] = jnp.zeros_like(h_ref)

    x = xr_ref[...].astype(F32)
    w = x.shape[1]
    rows = lax.broadcasted_iota(I32, (ts, w), 0)
    head_rows = lax.broadcasted_iota(I32, (SUBLANES, w), 0)
    tail = xtail_ref[...]
    xc = bc_ref[...]
    for tap in range(CONV_WIDTH):
        back = CONV_WIDTH - 1 - tap
        if back == 0:
            xs = x
        else:
            rolled = pltpu.roll(x, back, 0)
            head = jnp.where(head_rows < back, pltpu.roll(tail, back, 0), rolled[:SUBLANES])
            xs = jnp.concatenate([head, rolled[SUBLANES:]], axis=0)
        xc = xc + xs * wc_ref[tap:tap + 1, :]
    xtail_ref[...] = x[ts - SUBLANES:ts, :]

    xcb = xc.astype(BF16)
    bd = w // LRU_BLOCKS
    rs, gs = [], []
    for g in range(LRU_BLOCKS):
        z = jnp.dot(xcb[:, g * bd:(g + 1) * bd], wg_ref[g], preferred_element_type=F32)
        rs.append(z[:, :bd])
        gs.append(z[:, bd:])
    r = jax.nn.sigmoid(jnp.concatenate(rs, axis=1) + ba_ref[...])
    i = jax.nn.sigmoid(jnp.concatenate(gs, axis=1) + bx_ref[...])
    log_a = -LRU_C * r * jax.nn.softplus(-lam_ref[...])
    a = jnp.exp(log_a)
    bt = jnp.sqrt(-jnp.tanh(log_a) * (a * a + 1.0)) * (i * xc)

    acc_a, acc_b = a, bt
    sh = 1
    while sh < ts:
        if sh < SUBLANES:
            m = rows >= sh
            acc_b = jnp.where(m, acc_a * pltpu.roll(acc_b, sh, 0) + acc_b, acc_b)
            acc_a = jnp.where(m, acc_a * pltpu.roll(acc_a, sh, 0), acc_a)
        else:
            new_b = acc_a[sh:] * acc_b[:ts - sh] + acc_b[sh:]
            acc_a = jnp.concatenate([acc_a[:sh], acc_a[sh:] * acc_a[:ts - sh]], axis=0)
            acc_b = jnp.concatenate([acc_b[:sh], new_b], axis=0)
        sh *= 2
    hs = acc_a * h_ref[...] + acc_b
    h_ref[...] = hs[ts - 1:ts, :]

    lru = hs * jax.nn.gelu(yr_ref[...].astype(F32), approximate=True)
    o_ref[...] = (_rms(lru) * beta_ref[...]).astype(o_ref.dtype)


def _lru_call(proj, w_conv, b_conv, w_rg_a, b_rg_a, w_rg_x, b_rg_x, lam, beta_lru, batch, seq):
    t = batch * seq
    w = w_conv.shape[1]
    ts = min(256, seq)
    nt = seq // ts
    wg = jnp.concatenate([w_rg_a, w_rg_x], axis=-1).astype(BF16)
    vec = lambda: pl.BlockSpec((1, w), lambda b, j: (0, 0))
    row = lambda which: pl.BlockSpec((None, ts, w), lambda b, j: (which, b * nt + j, 0))
    return pl.pallas_call(
        functools.partial(_lru_kernel, ts=ts),
        grid=(batch, nt),
        in_specs=[row(4), row(5),
                  pl.BlockSpec((CONV_WIDTH, w), lambda b, j: (0, 0)), vec(),
                  pl.BlockSpec(wg.shape, lambda b, j: (0, 0, 0)),
                  vec(), vec(), vec(), vec()],
        out_specs=pl.BlockSpec((ts, w), lambda b, j: (b * nt + j, 0)),
        out_shape=jax.ShapeDtypeStruct((t, w), BF16),
        scratch_shapes=[pltpu.VMEM((SUBLANES, w), F32), pltpu.VMEM((1, w), F32)],
        compiler_params=_params(("parallel", "arbitrary")),
        name="rg_lru",
    )(proj, proj, w_conv, b_conv.reshape(1, w), wg, b_rg_a.reshape(1, w), b_rg_x.reshape(1, w),
      lam.reshape(1, w), beta_lru.reshape(1, w))


def _out_kernel(ret_ref, lru_ref, x_ref, mod_ref, g_ref, w_ref, x1_ref, h2_ref):
    rw = ret_ref.shape[1]
    mix = jnp.dot(ret_ref[...], w_ref[0:rw, :], preferred_element_type=F32)
    mix = mix + jnp.dot(lru_ref[...], w_ref[rw:, :], preferred_element_type=F32)
    x1 = x_ref[...] + mod_ref[2:3, :] * mix
    x1_ref[...] = x1
    h2 = _rms(x1) * g_ref[...] * (1.0 + mod_ref[4:5, :]) + mod_ref[3:4, :]
    half = h2.shape[1] // 2
    _store_rows(h2_ref, _pack2(h2[:, :half], h2[:, half:]))


def _out_call(ret, lru, x2, mod3, g_moe, w_out_bf, seq):
    t, d = x2.shape
    assert d // 2 == SUBLANES * LANES, "packed token rows are exactly one (8, 128) word tile"
    rw, lw = ret.shape[1], lru.shape[1]
    tm = min(512, seq)
    per_b = seq // tm
    return pl.pallas_call(
        _out_kernel,
        grid=(t // tm,),
        in_specs=[pl.BlockSpec((tm, rw), lambda i: (i, 0)),
                  pl.BlockSpec((tm, lw), lambda i: (i, 0)),
                  pl.BlockSpec((tm, d), lambda i: (i, 0)),
                  pl.BlockSpec((None, 6, d), lambda i: (i // per_b, 0, 0)),
                  pl.BlockSpec((1, d), lambda i: (0, 0)),
                  pl.BlockSpec((rw + lw, d), lambda i: (0, 0))],
        out_specs=[pl.BlockSpec((tm, d), lambda i: (i, 0)),
                   pl.BlockSpec((tm * SUBLANES, LANES), lambda i: (i, 0))],
        out_shape=[jax.ShapeDtypeStruct((t, d), F32),
                   jax.ShapeDtypeStruct((t * SUBLANES, LANES), U32)],
        compiler_params=_params(("parallel",)),
        name="out_proj",
    )(ret, lru, x2, mod3, g_moe.reshape(1, d), w_out_bf)


def _route_kernel(h_ref, wlo_ref, whi_ref, bias_ref, tri_ref, eidx_ref, wsel_ref, pos_ref,
                  cnt_ref, carry_ref):
    @pl.when(pl.program_id(0) == 0)
    def _():
        carry_ref[...] = jnp.zeros_like(carry_ref)

    tr = h_ref.shape[0] // SUBLANES
    lo, hi = _unpack2(_load_rows(h_ref))
    nt = (((1,), (1,)), ((), ()))
    logits = lax.dot_general(wlo_ref[...], lo.astype(BF16), nt, preferred_element_type=F32)
    logits = logits + lax.dot_general(whi_ref[...], hi.astype(BF16), nt,
                                      preferred_element_type=F32)
    scores = jax.nn.sigmoid(logits)
    biased = scores + bias_ref[:, 0:1]
    shape3 = (N_GROUPS, GROUP_SIZE, tr)
    s3 = scores.reshape(shape3)
    b3 = biased.reshape(shape3)
    member = lax.broadcasted_iota(I32, shape3, 1)
    group = lax.broadcasted_iota(I32, shape3, 0)
    expert = group * GROUP_SIZE + member

    m1 = jnp.max(b3, axis=1, keepdims=True)
    i1 = jnp.min(jnp.where(b3 == m1, member, GROUP_SIZE), axis=1, keepdims=True)
    m2 = jnp.max(jnp.where(member == i1, NEG_INF, b3), axis=1, keepdims=True)
    gscore = m1 + m2

    gid = lax.broadcasted_iota(I32, (N_GROUPS, 1, tr), 0)
    rank = jnp.zeros((N_GROUPS, 1, tr), I32)
    for j in range(N_GROUPS):
        gj = gscore[j:j + 1]
        ahead = (gj > gscore) | ((gj == gscore) & (gid > j))
        rank = rank + ahead.astype(I32)
    masked = jnp.where(rank < TOPK_GROUPS, b3, NEG_INF)

    idxs, vals = [], []
    picked = jnp.zeros(shape3, F32)
    for _ in range(TOP_K):
        m = jnp.max(jnp.max(masked, axis=0, keepdims=True), axis=1, keepdims=True)
        cand = jnp.where(masked == m, expert, N_EXPERTS)
        idx = jnp.min(jnp.min(cand, axis=0, keepdims=True), axis=1, keepdims=True)
        hit = expert == idx
        val = jnp.sum(jnp.sum(jnp.where(hit, s3, 0.0), axis=0, keepdims=True), axis=1, keepdims=True)
        masked = jnp.where(hit, NEG_INF, masked)
        picked = jnp.where(hit, 1.0, picked)
        idxs.append(idx)
        vals.append(val)
    total = vals[0]
    for v in vals[1:]:
        total = total + v

    before = jnp.dot(picked.reshape(N_EXPERTS, tr).astype(BF16), tri_ref[...],
                     preferred_element_type=F32)
    carry = carry_ref[...]
    posm = (before + carry[:, 0:1]).reshape(shape3)
    carry = carry + jnp.sum(picked.reshape(N_EXPERTS, tr), axis=1, keepdims=True)
    carry_ref[...] = carry
    cnt_ref[...] = carry

    for k in range(TOP_K):
        hit = expert == idxs[k]
        p = jnp.sum(jnp.sum(jnp.where(hit, posm, 0.0), axis=0, keepdims=True), axis=1, keepdims=True)
        eidx_ref[k:k + 1, :] = idxs[k].reshape(1, tr)
        wsel_ref[k:k + 1, :] = (vals[k] / total * ROUTED_SCALE).reshape(1, tr)
        pos_ref[k:k + 1, :] = p.reshape(1, tr).astype(I32)


def _route_call(h2p, w_router, b_router):
    t = h2p.shape[0] // SUBLANES
    half = SUBLANES * LANES
    tr = min(512, t)
    wt = w_router.T.astype(BF16)
    bias = jnp.broadcast_to(b_router.astype(F32)[:, None], (N_EXPERTS, LANES))
    tri = jnp.asarray(np.triu(np.ones((tr, tr), np.float32), k=1), dtype=BF16)
    kt = lambda dt: jax.ShapeDtypeStruct((TOP_K, t), dt)
    krow = lambda: pl.BlockSpec((TOP_K, tr), lambda i: (0, i))
    return pl.pallas_call(
        _route_kernel,
        grid=(t // tr,),
        in_specs=[pl.BlockSpec((tr * SUBLANES, LANES), lambda i: (i, 0)),
                  pl.BlockSpec((N_EXPERTS, half), lambda i: (0, 0)),
                  pl.BlockSpec((N_EXPERTS, half), lambda i: (0, 0)),
                  pl.BlockSpec((N_EXPERTS, LANES), lambda i: (0, 0)),
                  pl.BlockSpec((tr, tr), lambda i: (0, 0))],
        out_specs=[krow(), krow(), krow(),
                   pl.BlockSpec((N_EXPERTS, LANES), lambda i: (0, 0))],
        out_shape=[kt(I32), kt(F32), kt(I32), jax.ShapeDtypeStruct((N_EXPERTS, LANES), F32)],
        scratch_shapes=[pltpu.VMEM((N_EXPERTS, LANES), F32)],
        compiler_params=_params(("arbitrary",)),
        name="router",
    )(h2p, wt[:, :half], wt[:, half:], bias, tri)


def _dest_kernel(pst_ref, eidx_ref, pos_ref, o_ref):
    e = eidx_ref[...]
    acc = pos_ref[...]
    for j in range(N_EXPERTS):
        acc = acc + jnp.where(e == j, pst_ref[j], 0)
    o_ref[...] = acc


def _dest_call(pstarts, eidx, pos):
    k, t = eidx.shape
    tb = min(2048, t)
    blk = lambda: pl.BlockSpec((k, tb), lambda i, pst: (0, i))
    return pl.pallas_call(
        _dest_kernel,
        grid_spec=pltpu.PrefetchScalarGridSpec(
            num_scalar_prefetch=1, grid=(t // tb,),
            in_specs=[blk(), blk()], out_specs=blk()),
        out_shape=jax.ShapeDtypeStruct((k, t), I32),
        compiler_params=_params(("parallel",)),
        name="dest_rows",
    )(pstarts, eidx, pos)


def _swiglu_packed(x_ref, w1_ref, w2_ref):
    xw = _load_rows(x_ref)
    half = xw.shape[1]
    lo, hi = _unpack2(xw)
    gu = jnp.dot(lo.astype(BF16), w1_ref[0:half, :], preferred_element_type=F32)
    gu = gu + jnp.dot(hi.astype(BF16), w1_ref[half:, :], preferred_element_type=F32)
    hid = gu.shape[1] // 2
    act = _silu(gu[:, :hid]) * gu[:, hid:]
    return jnp.dot(act.astype(BF16), w2_ref[...], preferred_element_type=F32)


def _dispatch_kernel(meta_ref, dest_ref, h_ref, w1_ref, w2_ref, xs_ref, sh_ref, zero_ref,
                     sem_z, sem_r, *, td):

    @pl.when(pl.program_id(0) == 0)
    def _():
        _zero_rows(zero_ref)

        blk_rows = MOE_BLOCK * SUBLANES

        def block_copy(first_token):
            start = pl.multiple_of(first_token * SUBLANES, blk_rows)
            return pltpu.make_async_copy(zero_ref, xs_ref.at[pl.ds(start, blk_rows)], sem_z)

        def tail_copy(e):
            return block_copy(meta_ref[0, e] + meta_ref[2, e] - MOE_BLOCK)

        def unused_copy(j):
            return block_copy((meta_ref[3, 0] + j) * MOE_BLOCK)

        n_blocks = xs_ref.shape[0] // blk_rows
        for e in range(N_EXPERTS):
            @pl.when(meta_ref[2, e] > meta_ref[1, e])
            def _():
                tail_copy(e).start()

            @pl.when(meta_ref[3, 0] + e < n_blocks)
            def _():
                unused_copy(e).start()
        for e in range(N_EXPERTS):
            @pl.when(meta_ref[2, e] > meta_ref[1, e])
            def _():
                tail_copy(e).wait()

            @pl.when(meta_ref[3, 0] + e < n_blocks)
            def _():
                unused_copy(e).wait()

    def issue(r, carry):
        src = _token(h_ref, r)
        for k in range(TOP_K):
            pltpu.make_async_copy(src, _token(xs_ref, dest_ref[k, r]), sem_r).start(priority=k % 2)
        return carry

    lax.fori_loop(0, td, issue, 0)

    sh_ref[...] = _swiglu_packed(h_ref, w1_ref, w2_ref).astype(sh_ref.dtype)

    for k in range(TOP_K):
        pltpu.make_async_copy(h_ref, xs_ref.at[pl.ds(0, td * SUBLANES)], sem_r).wait()


def _dispatch_call(meta, dest, h2p, w_sh_in_bf, w_sh_out_bf, n_rows):
    t = h2p.shape[0] // SUBLANES
    d = 2 * SUBLANES * LANES
    td = min(512, t)
    return pl.pallas_call(
        functools.partial(_dispatch_kernel, td=td),
        grid=(t // td,),
        in_specs=[pl.BlockSpec(memory_space=pltpu.SMEM),
                  pl.BlockSpec((TOP_K, td), lambda i: (0, i), memory_space=pltpu.SMEM),
                  pl.BlockSpec((td * SUBLANES, LANES), lambda i: (i, 0)),
                  pl.BlockSpec(w_sh_in_bf.shape, lambda i: (0, 0)),
                  pl.BlockSpec(w_sh_out_bf.shape, lambda i: (0, 0))],
        out_specs=[pl.BlockSpec(memory_space=pl.ANY),
                   pl.BlockSpec((td, d), lambda i: (i, 0))],
        out_shape=[jax.ShapeDtypeStruct((n_rows * SUBLANES, LANES), U32),
                   jax.ShapeDtypeStruct((t, d), BF16)],
        scratch_shapes=[pltpu.VMEM((MOE_BLOCK * SUBLANES, LANES), U32),
                        pltpu.SemaphoreType.DMA, pltpu.SemaphoreType.DMA],
        compiler_params=_params(("arbitrary",)),
        name="dispatch_shared",
    )(meta, dest, h2p, w_sh_in_bf, w_sh_out_bf)


def _expert_kernel(meta_ref, xs_ref, w1_hbm, w2_hbm, ys_ref, w1f, w2f, w1b_ref, w2b_ref, xbuf, ybuf,
                   sem_w, sem_in, sem_out, *, n_blocks):
    e = pl.program_id(0)
    n_exp = pl.num_programs(0)
    blk_rows = MOE_BLOCK * SUBLANES
    wslot = lax.rem(e, 2)
    low = 1

    def weights(ex, s):
        return (pltpu.make_async_copy(w1_hbm.at[ex], w1f.at[s], sem_w.at[0, s]),
                pltpu.make_async_copy(w2_hbm.at[ex], w2f.at[s], sem_w.at[1, s]))

    @pl.when(e == 0)
    def _():
        for cp in weights(0, 0):
            cp.start()

    for cp in weights(e, wslot):
        cp.wait()

    @pl.when(e + 1 < n_exp)
    def _():
        for cp in weights(e + 1, 1 - wslot):
            cp.start(priority=low)
    first = meta_ref[0, e] // MOE_BLOCK
    nblk = meta_ref[2, e] // MOE_BLOCK
    nfull = nblk // 2

    def rows_of(b):
        return pl.ds(pl.multiple_of((first + b) * blk_rows, blk_rows), blk_rows)

    def fetch(b, s, base=None):
        src = rows_of(b) if base is None else pl.ds(
            pl.multiple_of((base + b) * blk_rows, blk_rows), blk_rows)
        return pltpu.make_async_copy(xs_ref.at[src], xbuf.at[s], sem_in.at[s])

    def store(b, s):
        return pltpu.make_async_copy(ybuf.at[s], ys_ref.at[rows_of(b)], sem_out.at[s])

    def compute(s):
        out = _swiglu_packed(xbuf.at[s], w1b_ref, w2b_ref)
        half = out.shape[1] // 2
        _store_rows(ybuf.at[s], _pack2(out[:, :half], out[:, half:]))

    odd = lax.rem(nblk, 2) == 1
    tail_slot = 4
    row_queue = 0

    @pl.when(nblk > 0)
    def _():
        @pl.when(jnp.logical_and(nfull > 0, e == 0))
        def _():
            fetch(0, 0).start(priority=row_queue)
            fetch(1, 1).start(priority=row_queue)

        @pl.when(odd)
        def _():
            fetch(nblk - 1, tail_slot).start(priority=row_queue)

        w1b_ref[...] = w1f[wslot].astype(BF16)
        w2b_ref[...] = w2f[wslot].astype(BF16)

    def drain(n_blocks_of):
        nf = n_blocks_of // 2
        for back in (2, 1):
            @pl.when(nf >= back)
            def _():
                s0 = 2 * lax.rem(nf - back, 2)
                store(0, s0).wait()
                store(0, s0 + 1).wait()

        @pl.when(lax.rem(n_blocks_of, 2) == 1)
        def _():
            store(0, tail_slot).wait()

    @pl.when(e > 0)
    def _():
        drain(meta_ref[2, jnp.maximum(e - 1, 0)] // MOE_BLOCK)

    @pl.when(nblk > 0)
    def _():
        def pair(p, carry):
            s0 = 2 * lax.rem(p, 2)
            n0 = 2 - s0
            fetch(2 * p, s0).wait()
            fetch(2 * p + 1, s0 + 1).wait()

            @pl.when(p + 1 < nfull)
            def _():
                fetch(2 * p + 2, n0).start(priority=row_queue)
                fetch(2 * p + 3, n0 + 1).start(priority=row_queue)

            @pl.when(p >= 2)
            def _():
                store(2 * p - 4, s0).wait()
                store(2 * p - 3, s0 + 1).wait()

            compute(s0)
            compute(s0 + 1)
            store(2 * p, s0).start(priority=row_queue)
            store(2 * p + 1, s0 + 1).start(priority=row_queue)
            return carry

        lax.fori_loop(0, nfull, pair, 0)

        @pl.when(odd)
        def _():
            fetch(nblk - 1, tail_slot).wait()
            compute(tail_slot)
            store(nblk - 1, tail_slot).start(priority=row_queue)

    @pl.when(e + 1 < n_exp)
    def _():
        nxt = jnp.minimum(e + 1, n_exp - 1)
        nxt_first = meta_ref[0, nxt] // MOE_BLOCK

        @pl.when(meta_ref[2, nxt] // MOE_BLOCK >= 2)
        def _():
            fetch(0, 0, base=nxt_first).start(priority=row_queue)
            fetch(1, 1, base=nxt_first).start(priority=row_queue)

    @pl.when(e == pl.num_programs(0) - 1)
    def _():
        drain(nblk)
        _zero_rows(ybuf.at[0])

        def unused(j):
            start = pl.multiple_of((meta_ref[3, 0] + j) * blk_rows, blk_rows)
            return pltpu.make_async_copy(ybuf.at[0], ys_ref.at[pl.ds(start, blk_rows)],
                                         sem_out.at[0])

        for j in range(N_EXPERTS):
            @pl.when(meta_ref[3, 0] + j < n_blocks)
            def _():
                unused(j).start()
        for j in range(N_EXPERTS):
            @pl.when(meta_ref[3, 0] + j < n_blocks)
            def _():
                unused(j).wait()


def _expert_call(meta, xs, w_exp_in, w_exp_out):
    blk_rows = MOE_BLOCK * SUBLANES
    n_exp, d, h2 = w_exp_in.shape
    hid = w_exp_out.shape[1]
    n_blocks = xs.shape[0] // blk_rows
    return pl.pallas_call(
        functools.partial(_expert_kernel, n_blocks=n_blocks),
        grid_spec=pltpu.PrefetchScalarGridSpec(
            num_scalar_prefetch=1, grid=(n_exp,),
            in_specs=[pl.BlockSpec(memory_space=pl.ANY),
                      pl.BlockSpec(memory_space=pl.ANY),
                      pl.BlockSpec(memory_space=pl.ANY)],
            out_specs=pl.BlockSpec(memory_space=pl.ANY),
            scratch_shapes=[pltpu.VMEM((2, d, h2), F32), pltpu.VMEM((2, hid, d), F32),
                            pltpu.VMEM((d, h2), BF16), pltpu.VMEM((hid, d), BF16),
                            pltpu.VMEM((5, blk_rows, LANES), U32),
                            pltpu.VMEM((5, blk_rows, LANES), U32),
                            pltpu.SemaphoreType.DMA((2, 2)),
                            pltpu.SemaphoreType.DMA((5,)), pltpu.SemaphoreType.DMA((5,))]),
        out_shape=jax.ShapeDtypeStruct(xs.shape, U32),
        compiler_params=_params(("arbitrary",)),
        name="experts",
    )(meta, xs, w_exp_in, w_exp_out)


def _combine_kernel(dest_ref, dnext_ref, w_ref, ys_ref, sh_ref, x1_ref, mod_ref, g_ref, o_ref,
                    buf_ref, sem, *, tc, final_norm):
    i = pl.program_id(0)
    slot = lax.rem(i, 2)

    def gather(d_ref, s):
        def issue(r, carry):
            for k in range(TOP_K):
                pltpu.make_async_copy(_token(ys_ref, d_ref[k, r]), _token(buf_ref.at[s, k], r),
                                      sem.at[s]).start(priority=k % 2)
            return carry

        lax.fori_loop(0, tc, issue, 0)

    @pl.when(i == 0)
    def _():
        gather(dest_ref, 0)

    @pl.when(i + 1 < pl.num_programs(0))
    def _():
        gather(dnext_ref, 1 - slot)

    for k in range(TOP_K):
        pltpu.make_async_copy(ys_ref.at[pl.ds(0, tc * SUBLANES)], buf_ref.at[slot, k],
                              sem.at[slot]).wait()

    acc_lo = acc_hi = None
    for k in range(TOP_K):
        wk = w_ref[:, k:k + 1]
        lo, hi = _unpack2(_load_rows(buf_ref.at[slot, k]))
        acc_lo = wk * lo if acc_lo is None else acc_lo + wk * lo
        acc_hi = wk * hi if acc_hi is None else acc_hi + wk * hi
    routed = jnp.concatenate([acc_lo, acc_hi], axis=1)
    moe = routed + sh_ref[...].astype(F32)
    x2 = x1_ref[...] + mod_ref[5:6, :] * moe
    o_ref[...] = _rms(x2) * g_ref[...] if final_norm else x2


def _combine_call(dest, wsel_t, ys, shared, x1, mod3, g_final, seq, final_norm):
    t, d = x1.shape
    tc = min(256, seq)
    per_b = seq // tc
    n_tiles = t // tc
    return pl.pallas_call(
        functools.partial(_combine_kernel, tc=tc, final_norm=final_norm),
        grid=(n_tiles,),
        in_specs=[pl.BlockSpec((TOP_K, tc), lambda i: (0, i), memory_space=pltpu.SMEM),
                  pl.BlockSpec((TOP_K, tc), lambda i: (0, jnp.minimum(i + 1, n_tiles - 1)),
                               memory_space=pltpu.SMEM),
                  pl.BlockSpec((tc, TOP_K), lambda i: (i, 0)),
                  pl.BlockSpec(memory_space=pl.ANY),
                  pl.BlockSpec((tc, d), lambda i: (i, 0)),
                  pl.BlockSpec((tc, d), lambda i: (i, 0)),
                  pl.BlockSpec((None, 6, d), lambda i: (i // per_b, 0, 0)),
                  pl.BlockSpec((1, d), lambda i: (0, 0))],
        out_specs=pl.BlockSpec((tc, d), lambda i: (i, 0)),
        out_shape=jax.ShapeDtypeStruct((t, d), F32),
        scratch_shapes=[pltpu.VMEM((2, TOP_K, tc * SUBLANES, LANES), U32),
                        pltpu.SemaphoreType.DMA((2,))],
        compiler_params=_params(("arbitrary",)),
        name="combine_final",
    )(dest, dest, wsel_t, ys, shared, x1, mod3, g_final.reshape(1, d))


def _moe_plan(counts):
    pcounts = (counts + MOE_BLOCK - 1) // MOE_BLOCK * MOE_BLOCK
    pends = jnp.cumsum(pcounts)
    pstarts = pends - pcounts
    return pstarts.astype(I32), pcounts.astype(I32), pends.astype(I32)


def kernel(x, c, positions, w_ada, b_ada, g_mix, w_in, w_conv, b_conv, w_rg_a, b_rg_a, w_rg_x, b_rg_x, lam, beta_ret, beta_lru, w_out, g_moe, w_router, b_router, w_exp_in, w_exp_out, w_sh_in, w_sh_out, g_final):
    batch, seq, d = x.shape
    t = batch * seq
    depth = w_ada.shape[0]
    xcur = x.reshape(t, d)
    cos, sin = _rope_call(positions)
    n_rows = (t * TOP_K + MOE_BLOCK - 1) // MOE_BLOCK * MOE_BLOCK + N_EXPERTS * MOE_BLOCK
    for l in range(depth):
        mod3 = _mod_call(c, w_ada[l], b_ada[l]).reshape(batch, 6, d)
        proj = _in_call(xcur, mod3, g_mix[l], w_in[l].astype(BF16), seq)
        ret = _ret_call(proj, cos, sin, beta_ret[l], batch, seq)
        lru = _lru_call(proj, w_conv[l], b_conv[l], w_rg_a[l], b_rg_a[l], w_rg_x[l], b_rg_x[l],
                        lam[l], beta_lru[l], batch, seq)
        x1, h2p = _out_call(ret, lru, xcur, mod3, g_moe[l], w_out[l].astype(BF16), seq)

        eidx, wsel, pos, cnt = _route_call(h2p, w_router[l], b_router[l])
        counts = cnt[:, 0].astype(I32)
        pstarts, pcounts, pends = _moe_plan(counts)
        dest = _dest_call(pstarts, eidx, pos)
        n_used = (pends[-1] // MOE_BLOCK).astype(I32)
        meta = jnp.stack([pstarts, counts, pcounts, jnp.full_like(counts, n_used)])

        xs, shared = _dispatch_call(meta, dest, h2p, w_sh_in[l].astype(BF16),
                                    w_sh_out[l].astype(BF16), n_rows)
        ys = _expert_call(meta, xs, w_exp_in[l], w_exp_out[l])
        xcur = _combine_call(dest, wsel.T, ys, shared, x1, mod3, g_final, seq,
                             final_norm=(l == depth - 1))
    return xcur.reshape(batch, seq, d)
```

```python
import functools

import numpy as np
import jax
import jax.numpy as jnp
from jax import lax
from jax.experimental import pallas as pl
from jax.experimental.pallas import tpu as pltpu

F32 = jnp.float32
BF16 = jnp.bfloat16
I32 = jnp.int32
U32 = jnp.uint32

RET_HEADS = 8
HEAD_DIM = 128
CHUNK = 128
ROPE_THETA = 10000.0
LRU_BLOCKS = 8
CONV_WIDTH = 4
LRU_C = 8.0
N_EXPERTS = 64
N_GROUPS = 8
GROUP_SIZE = N_EXPERTS // N_GROUPS
TOPK_GROUPS = 4
TOP_K = 8
ROUTED_SCALE = 2.5
EPS = 1e-6

LANES = 128
SUBLANES = 8
VMEM_LIMIT_BYTES = 56 * 1024 * 1024

MOE_BLOCK = 256
NEG_INF = float("-inf")


def _params(semantics, vmem=VMEM_LIMIT_BYTES):
    return pltpu.CompilerParams(dimension_semantics=semantics, vmem_limit_bytes=vmem)


def _silu(x):
    return x * jax.nn.sigmoid(x)


def _rms(x):
    return x * lax.rsqrt(jnp.mean(x * x, axis=-1, keepdims=True) + EPS)


def _pack2(lo, hi):
    return pltpu.pack_elementwise([lo, hi], packed_dtype=jnp.bfloat16)


def _unpack2(w):
    lo = pltpu.unpack_elementwise(w, index=0, packed_dtype=jnp.bfloat16, unpacked_dtype=F32)
    hi = pltpu.unpack_elementwise(w, index=1, packed_dtype=jnp.bfloat16, unpacked_dtype=F32)
    return lo, hi


def _load_rows(ref):
    rows = ref.shape[0] // SUBLANES
    return jnp.concatenate([ref[pl.ds(j, rows, stride=SUBLANES), :] for j in range(SUBLANES)],
                           axis=1)


def _store_rows(ref, v):
    rows = ref.shape[0] // SUBLANES
    for j in range(SUBLANES):
        ref[pl.ds(j, rows, stride=SUBLANES), :] = v[:, j * LANES:(j + 1) * LANES]


def _zero_rows(ref):
    rows = ref.shape[0] // SUBLANES
    z = jnp.zeros((rows, LANES), F32)
    w = _pack2(z, z)
    for j in range(SUBLANES):
        ref[pl.ds(j, rows, stride=SUBLANES), :] = w


def _token(ref, r):
    return ref.at[pl.ds(pl.multiple_of(r * SUBLANES, SUBLANES), SUBLANES)]


def _mod_kernel(c_ref, w_ref, b_ref, o_ref):
    cs = _silu(c_ref[...])
    o_ref[...] = jnp.dot(cs.astype(BF16), w_ref[...].astype(BF16),
                         preferred_element_type=F32) + b_ref[...]


def _mod_call(c, w_ada, b_ada):
    b, d = c.shape
    n = w_ada.shape[1]
    tn = 1024
    return pl.pallas_call(
        _mod_kernel,
        grid=(n // tn,),
        in_specs=[pl.BlockSpec((b, d), lambda j: (0, 0)),
                  pl.BlockSpec((d, tn), lambda j: (0, j)),
                  pl.BlockSpec((1, tn), lambda j: (0, j))],
        out_specs=pl.BlockSpec((b, tn), lambda j: (0, j)),
        out_shape=jax.ShapeDtypeStruct((b, n), F32),
        compiler_params=_params(("parallel",)),
        name="adaln_mod",
    )(c, w_ada, b_ada.reshape(1, n))


def _in_kernel(x_ref, mod_ref, g_ref, w_ref, o_ref, h_ref):
    j = pl.program_id(1)
    tm = x_ref.shape[0]
    chunk = min(256, tm)

    @pl.when(j == 0)
    def _():
        for r0 in range(0, tm, chunk):
            rows = slice(r0, r0 + chunk)
            y = _rms(x_ref[rows, :]) * g_ref[...]
            h = (y * (1.0 + mod_ref[1:2, :]) + mod_ref[0:1, :]).astype(BF16)
            h_ref[rows, :] = h
            o_ref[rows, :] = jnp.dot(h, w_ref[...],
                                     preferred_element_type=F32).astype(o_ref.dtype)

    @pl.when(j > 0)
    def _():
        o_ref[...] = jnp.dot(h_ref[...], w_ref[...],
                             preferred_element_type=F32).astype(o_ref.dtype)


def _in_call(x2, mod3, g_mix, w_in_bf, seq):
    t, d = x2.shape
    n_out = w_in_bf.shape[1] // 1024
    tm = min(1024, seq)
    per_b = seq // tm
    return pl.pallas_call(
        _in_kernel,
        grid=(t // tm, n_out),
        in_specs=[pl.BlockSpec((tm, d), lambda i, j: (i, 0)),
                  pl.BlockSpec((None, 6, d), lambda i, j: (i // per_b, 0, 0)),
                  pl.BlockSpec((1, d), lambda i, j: (0, 0)),
                  pl.BlockSpec((d, 1024), lambda i, j: (0, j))],
        out_specs=pl.BlockSpec((None, tm, 1024), lambda i, j: (j, i, 0)),
        out_shape=jax.ShapeDtypeStruct((n_out, t, 1024), BF16),
        scratch_shapes=[pltpu.VMEM((tm, d), BF16)],
        compiler_params=_params(("parallel", "arbitrary")),
        name="in_proj",
    )(x2, mod3, g_mix.reshape(1, d), w_in_bf)


def _rope_kernel(pos_ref, invf_ref, sign_ref, cos_ref, sin_ref):
    ang = pos_ref[...].astype(F32) * invf_ref[...]
    cos_ref[...] = jnp.cos(ang)
    sin_ref[...] = jnp.sin(ang) * sign_ref[...]


def _rope_call(positions):
    t = positions.size
    tp = min(2048, t)
    half = HEAD_DIM // 2
    inv_freq = 1.0 / (ROPE_THETA ** (jnp.arange(0, HEAD_DIM, 2, dtype=F32) / HEAD_DIM))
    invf = jnp.concatenate([inv_freq, inv_freq]).reshape(1, HEAD_DIM)
    sign = jnp.concatenate([-jnp.ones((half,), F32), jnp.ones((half,), F32)]).reshape(1, HEAD_DIM)
    return pl.pallas_call(
        _rope_kernel,
        grid=(t // tp,),
        in_specs=[pl.BlockSpec((tp, 1), lambda i: (i, 0)),
                  pl.BlockSpec((1, HEAD_DIM), lambda i: (0, 0)),
                  pl.BlockSpec((1, HEAD_DIM), lambda i: (0, 0))],
        out_specs=[pl.BlockSpec((tp, HEAD_DIM), lambda i: (i, 0)),
                   pl.BlockSpec((tp, HEAD_DIM), lambda i: (i, 0))],
        out_shape=[jax.ShapeDtypeStruct((t, HEAD_DIM), F32)] * 2,
        compiler_params=_params(("parallel",)),
        name="rope_tables",
    )(positions.reshape(t, 1), invf, sign)


def _ret_tables():
    h = np.arange(RET_HEADS, dtype=np.float64)
    lg = np.log1p(-np.exp2(-5.0 - h))
    idx = np.arange(CHUNK, dtype=np.float64)
    diff = idx[:, None] - idx[None, :]
    intra = np.where(diff >= 0, np.exp(lg[:, None, None] * np.maximum(diff, 0.0)), 0.0)
    kdec = np.exp(lg[:, None] * (CHUNK - 1.0 - idx)[None, :])
    qdec = np.exp(lg[:, None] * (idx + 1.0)[None, :])
    cdec = np.exp(lg * CHUNK)
    bc = lambda a: np.ascontiguousarray(np.broadcast_to(a[:, :, None], (RET_HEADS, CHUNK, HEAD_DIM)))
    return (intra.astype(np.float32), bc(qdec).astype(np.float32), bc(kdec).astype(np.float32),
            [float(v) for v in cdec])


def _ret_kernel(q_ref, k_ref, v_ref, g_ref, cos_ref, sin_ref, intra_ref, qdec_ref, kdec_ref,
                beta_ref, o_ref, st_ref, *, cdec):
    @pl.when(pl.program_id(1) == 0)
    def _():
        st_ref[...] = jnp.zeros_like(st_ref)

    scale = HEAD_DIM ** -0.5
    nt = (((1,), (1,)), ((), ()))
    tn = (((0,), (0,)), ((), ()))
    n_chunks = q_ref.shape[0] // CHUNK
    outs = [[] for _ in range(n_chunks)]
    for h in range(RET_HEADS):
        sl = slice(h * HEAD_DIM, (h + 1) * HEAD_DIM)
        st = st_ref[h]
        for c in range(n_chunks):
            rows = slice(c * CHUNK, (c + 1) * CHUNK)
            cos = cos_ref[rows, :]
            sin = sin_ref[rows, :]
            q = q_ref[rows, sl].astype(F32)
            k = k_ref[rows, sl].astype(F32)
            v = v_ref[rows, sl]
            qr = (q * cos + pltpu.roll(q, HEAD_DIM // 2, 1) * sin) * scale
            kr = k * cos + pltpu.roll(k, HEAD_DIM // 2, 1) * sin
            s = lax.dot_general(qr.astype(BF16), kr.astype(BF16), nt,
                                preferred_element_type=F32) * intra_ref[h]
            inner = jnp.dot(s.astype(BF16), v, preferred_element_type=F32)
            cross = jnp.dot((qr * qdec_ref[h]).astype(BF16), st.astype(BF16),
                            preferred_element_type=F32)
            o = inner + cross
            kv = lax.dot_general((kr * kdec_ref[h]).astype(BF16), v, tn,
                                 preferred_element_type=F32)
            st = cdec[h] * st + kv
            oc = o - jnp.mean(o, axis=-1, keepdims=True)
            on = oc * lax.rsqrt(jnp.mean(oc * oc, axis=-1, keepdims=True) + EPS)
            outs[c].append(_silu(g_ref[rows, sl].astype(F32)) * on)
        st_ref[h] = st
    for c in range(n_chunks):
        ret = jnp.concatenate(outs[c], axis=1)
        o_ref[c * CHUNK:(c + 1) * CHUNK, :] = (_rms(ret) * beta_ref[...]).astype(o_ref.dtype)


def _ret_call(proj, cos, sin, beta_ret, batch, seq):
    t = batch * seq
    rw = RET_HEADS * HEAD_DIM
    tr = min(4 * CHUNK, seq)
    nc = seq // tr
    intra, qdec, kdec, cdec = _ret_tables()
    row = lambda which: pl.BlockSpec((None, tr, rw), lambda b, n: (which, b * nc + n, 0))
    tab = lambda: pl.BlockSpec((RET_HEADS, CHUNK, HEAD_DIM), lambda b, n: (0, 0, 0))
    cs = lambda: pl.BlockSpec((tr, HEAD_DIM), lambda b, n: (b * nc + n, 0))
    return pl.pallas_call(
        functools.partial(_ret_kernel, cdec=cdec),
        grid=(batch, nc),
        in_specs=[row(0), row(1), row(2), row(3), cs(), cs(), tab(), tab(), tab(),
                  pl.BlockSpec((1, rw), lambda b, n: (0, 0))],
        out_specs=pl.BlockSpec((tr, rw), lambda b, n: (b * nc + n, 0)),
        out_shape=jax.ShapeDtypeStruct((t, rw), BF16),
        scratch_shapes=[pltpu.VMEM((RET_HEADS, HEAD_DIM, HEAD_DIM), F32)],
        compiler_params=_params(("parallel", "arbitrary")),
        name="retention",
    )(proj, proj, proj, proj, cos, sin, jnp.asarray(intra), jnp.asarray(qdec), jnp.asarray(kdec),
      beta_ret.reshape(1, rw))


def _lru_kernel(xr_ref, yr_ref, wc_ref, bc_ref, wg_ref, ba_ref, bx_ref, lam_ref, beta_ref,
                o_ref, xtail_ref, h_ref, *, ts):
    @pl.when(pl.program_id(1) == 0)
    def _():
        xtail_ref[...] = jnp.zeros_like(xtail_ref)
        h_ref[...] = jnp.zeros_like(h_ref)

    x = xr_ref[...].astype(F32)
    w = x.shape[1]
    rows = lax.broadcasted_iota(I32, (ts, w), 0)
    head_rows = lax.broadcasted_iota(I32, (SUBLANES, w), 0)
    tail = xtail_ref[...]
    xc = bc_ref[...]
    for tap in range(CONV_WIDTH):
        back = CONV_WIDTH - 1 - tap
        if back == 0:
            xs = x
        else:
            rolled = pltpu.roll(x, back, 0)
            head = jnp.where(head_rows < back, pltpu.roll(tail, back, 0), rolled[:SUBLANES])
            xs = jnp.concatenate([head, rolled[SUBLANES:]], axis=0)
        xc = xc + xs * wc_ref[tap:tap + 1, :]
    xtail_ref[...] = x[ts - SUBLANES:ts, :]

    xcb = xc.astype(BF16)
    bd = w // LRU_BLOCKS
    rs, gs = [], []
    for g in range(LRU_BLOCKS):
        z = jnp.dot(xcb[:, g * bd:(g + 1) * bd], wg_ref[g], preferred_element_type=F32)
        rs.append(z[:, :bd])
        gs.append(z[:, bd:])
    r = jax.nn.sigmoid(jnp.concatenate(rs, axis=1) + ba_ref[...])
    i = jax.nn.sigmoid(jnp.concatenate(gs, axis=1) + bx_ref[...])
    log_a = -LRU_C * r * jax.nn.softplus(-lam_ref[...])
    a = jnp.exp(log_a)
    bt = jnp.sqrt(-jnp.tanh(log_a) * (a * a + 1.0)) * (i * xc)

    acc_a, acc_b = a, bt
    sh = 1
    while sh < ts:
        if sh < SUBLANES:
            m = rows >= sh
            acc_b = jnp.where(m, acc_a * pltpu.roll(acc_b, sh, 0) + acc_b, acc_b)
            acc_a = jnp.where(m, acc_a * pltpu.roll(acc_a, sh, 0), acc_a)
        else:
            new_b = acc_a[sh:] * acc_b[:ts - sh] + acc_b[sh:]
            acc_a = jnp.concatenate([acc_a[:sh], acc_a[sh:] * acc_a[:ts - sh]], axis=0)
            acc_b = jnp.concatenate([acc_b[:sh], new_b], axis=0)
        sh *= 2
    hs = acc_a * h_ref[...] + acc_b
    h_ref[...] = hs[ts - 1:ts, :]

    lru = hs * jax.nn.gelu(yr_ref[...].astype(F32), approximate=True)
    o_ref[...] = (_rms(lru) * beta_ref[...]).astype(o_ref.dtype)


def _lru_call(proj, w_conv, b_conv, w_rg_a, b_rg_a, w_rg_x, b_rg_x, lam, beta_lru, batch, seq):
    t = batch * seq
    w = w_conv.shape[1]
    ts = min(256, seq)
    nt = seq // ts
    wg = jnp.concatenate([w_rg_a, w_rg_x], axis=-1).astype(BF16)
    vec = lambda: pl.BlockSpec((1, w), lambda b, j: (0, 0))
    row = lambda which: pl.BlockSpec((None, ts, w), lambda b, j: (which, b * nt + j, 0))
    return pl.pallas_call(
        functools.partial(_lru_kernel, ts=ts),
        grid=(batch, nt),
        in_specs=[row(4), row(5),
                  pl.BlockSpec((CONV_WIDTH, w), lambda b, j: (0, 0)), vec(),
                  pl.BlockSpec(wg.shape, lambda b, j: (0, 0, 0)),
                  vec(), vec(), vec(), vec()],
        out_specs=pl.BlockSpec((ts, w), lambda b, j: (b * nt + j, 0)),
        out_shape=jax.ShapeDtypeStruct((t, w), BF16),
        scratch_shapes=[pltpu.VMEM((SUBLANES, w), F32), pltpu.VMEM((1, w), F32)],
        compiler_params=_params(("parallel", "arbitrary")),
        name="rg_lru",
    )(proj, proj, w_conv, b_conv.reshape(1, w), wg, b_rg_a.reshape(1, w), b_rg_x.reshape(1, w),
      lam.reshape(1, w), beta_lru.reshape(1, w))


def _out_kernel(ret_ref, lru_ref, x_ref, mod_ref, g_ref, w_ref, x1_ref, h2_ref):
    rw = ret_ref.shape[1]
    mix = jnp.dot(ret_ref[...], w_ref[0:rw, :], preferred_element_type=F32)
    mix = mix + jnp.dot(lru_ref[...], w_ref[rw:, :], preferred_element_type=F32)
    x1 = x_ref[...] + mod_ref[2:3, :] * mix
    x1_ref[...] = x1
    h2 = _rms(x1) * g_ref[...] * (1.0 + mod_ref[4:5, :]) + mod_ref[3:4, :]
    half = h2.shape[1] // 2
    _store_rows(h2_ref, _pack2(h2[:, :half], h2[:, half:]))


def _out_call(ret, lru, x2, mod3, g_moe, w_out_bf, seq):
    t, d = x2.shape
    assert d // 2 == SUBLANES * LANES, "packed token rows are exactly one (8, 128) word tile"
    rw, lw = ret.shape[1], lru.shape[1]
    tm = min(512, seq)
    per_b = seq // tm
    return pl.pallas_call(
        _out_kernel,
        grid=(t // tm,),
        in_specs=[pl.BlockSpec((tm, rw), lambda i: (i, 0)),
                  pl.BlockSpec((tm, lw), lambda i: (i, 0)),
                  pl.BlockSpec((tm, d), lambda i: (i, 0)),
                  pl.BlockSpec((None, 6, d), lambda i: (i // per_b, 0, 0)),
                  pl.BlockSpec((1, d), lambda i: (0, 0)),
                  pl.BlockSpec((rw + lw, d), lambda i: (0, 0))],
        out_specs=[pl.BlockSpec((tm, d), lambda i: (i, 0)),
                   pl.BlockSpec((tm * SUBLANES, LANES), lambda i: (i, 0))],
        out_shape=[jax.ShapeDtypeStruct((t, d), F32),
                   jax.ShapeDtypeStruct((t * SUBLANES, LANES), U32)],
        compiler_params=_params(("parallel",)),
        name="out_proj",
    )(ret, lru, x2, mod3, g_moe.reshape(1, d), w_out_bf)


def _route_kernel(h_ref, wlo_ref, whi_ref, bias_ref, tri_ref, eidx_ref, wsel_ref, pos_ref,
                  cnt_ref, carry_ref):
    @pl.when(pl.program_id(0) == 0)
    def _():
        carry_ref[...] = jnp.zeros_like(carry_ref)

    tr = h_ref.shape[0] // SUBLANES
    lo, hi = _unpack2(_load_rows(h_ref))
    nt = (((1,), (1,)), ((), ()))
    logits = lax.dot_general(wlo_ref[...], lo.astype(BF16), nt, preferred_element_type=F32)
    logits = logits + lax.dot_general(whi_ref[...], hi.astype(BF16), nt,
                                      preferred_element_type=F32)
    scores = jax.nn.sigmoid(logits)
    biased = scores + bias_ref[:, 0:1]
    shape3 = (N_GROUPS, GROUP_SIZE, tr)
    s3 = scores.reshape(shape3)
    b3 = biased.reshape(shape3)
    member = lax.broadcasted_iota(I32, shape3, 1)
    group = lax.broadcasted_iota(I32, shape3, 0)
    expert = group * GROUP_SIZE + member

    m1 = jnp.max(b3, axis=1, keepdims=True)
    i1 = jnp.min(jnp.where(b3 == m1, member, GROUP_SIZE), axis=1, keepdims=True)
    m2 = jnp.max(jnp.where(member == i1, NEG_INF, b3), axis=1, keepdims=True)
    gscore = m1 + m2

    gid = lax.broadcasted_iota(I32, (N_GROUPS, 1, tr), 0)
    rank = jnp.zeros((N_GROUPS, 1, tr), I32)
    for j in range(N_GROUPS):
        gj = gscore[j:j + 1]
        ahead = (gj > gscore) | ((gj == gscore) & (gid > j))
        rank = rank + ahead.astype(I32)
    masked = jnp.where(rank < TOPK_GROUPS, b3, NEG_INF)

    idxs, vals = [], []
    picked = jnp.zeros(shape3, F32)
    for _ in range(TOP_K):
        m = jnp.max(jnp.max(masked, axis=0, keepdims=True), axis=1, keepdims=True)
        cand = jnp.where(masked == m, expert, N_EXPERTS)
        idx = jnp.min(jnp.min(cand, axis=0, keepdims=True), axis=1, keepdims=True)
        hit = expert == idx
        val = jnp.sum(jnp.sum(jnp.where(hit, s3, 0.0), axis=0, keepdims=True), axis=1, keepdims=True)
        masked = jnp.where(hit, NEG_INF, masked)
        picked = jnp.where(hit, 1.0, picked)
        idxs.append(idx)
        vals.append(val)
    total = vals[0]
    for v in vals[1:]:
        total = total + v

    before = jnp.dot(picked.reshape(N_EXPERTS, tr).astype(BF16), tri_ref[...],
                     preferred_element_type=F32)
    carry = carry_ref[...]
    posm = (before + carry[:, 0:1]).reshape(shape3)
    carry = carry + jnp.sum(picked.reshape(N_EXPERTS, tr), axis=1, keepdims=True)
    carry_ref[...] = carry
    cnt_ref[...] = carry

    for k in range(TOP_K):
        hit = expert == idxs[k]
        p = jnp.sum(jnp.sum(jnp.where(hit, posm, 0.0), axis=0, keepdims=True), axis=1, keepdims=True)
        eidx_ref[k:k + 1, :] = idxs[k].reshape(1, tr)
        wsel_ref[k:k + 1, :] = (vals[k] / total * ROUTED_SCALE).reshape(1, tr)
        pos_ref[k:k + 1, :] = p.reshape(1, tr).astype(I32)


def _route_call(h2p, w_router, b_router):
    t = h2p.shape[0] // SUBLANES
    half = SUBLANES * LANES
    tr = min(512, t)
    wt = w_router.T.astype(BF16)
    bias = jnp.broadcast_to(b_router.astype(F32)[:, None], (N_EXPERTS, LANES))
    tri = jnp.asarray(np.triu(np.ones((tr, tr), np.float32), k=1), dtype=BF16)
    kt = lambda dt: jax.ShapeDtypeStruct((TOP_K, t), dt)
    krow = lambda: pl.BlockSpec((TOP_K, tr), lambda i: (0, i))
    return pl.pallas_call(
        _route_kernel,
        grid=(t // tr,),
        in_specs=[pl.BlockSpec((tr * SUBLANES, LANES), lambda i: (i, 0)),
                  pl.BlockSpec((N_EXPERTS, half), lambda i: (0, 0)),
                  pl.BlockSpec((N_EXPERTS, half), lambda i: (0, 0)),
                  pl.BlockSpec((N_EXPERTS, LANES), lambda i: (0, 0)),
                  pl.BlockSpec((tr, tr), lambda i: (0, 0))],
        out_specs=[krow(), krow(), krow(),
                   pl.BlockSpec((N_EXPERTS, LANES), lambda i: (0, 0))],
        out_shape=[kt(I32), kt(F32), kt(I32), jax.ShapeDtypeStruct((N_EXPERTS, LANES), F32)],
        scratch_shapes=[pltpu.VMEM((N_EXPERTS, LANES), F32)],
        compiler_params=_params(("arbitrary",)),
        name="router",
    )(h2p, wt[:, :half], wt[:, half:], bias, tri)


def _dest_kernel(pst_ref, eidx_ref, pos_ref, o_ref):
    e = eidx_ref[...]
    acc = pos_ref[...]
    for j in range(N_EXPERTS):
        acc = acc + jnp.where(e == j, pst_ref[j], 0)
    o_ref[...] = acc


def _dest_call(pstarts, eidx, pos):
    k, t = eidx.shape
    tb = min(2048, t)
    blk = lambda: pl.BlockSpec((k, tb), lambda i, pst: (0, i))
    return pl.pallas_call(
        _dest_kernel,
        grid_spec=pltpu.PrefetchScalarGridSpec(
            num_scalar_prefetch=1, grid=(t // tb,),
            in_specs=[blk(), blk()], out_specs=blk()),
        out_shape=jax.ShapeDtypeStruct((k, t), I32),
        compiler_params=_params(("parallel",)),
        name="dest_rows",
    )(pstarts, eidx, pos)


def _swiglu_packed(x_ref, w1_ref, w2_ref):
    xw = _load_rows(x_ref)
    half = xw.shape[1]
    lo, hi = _unpack2(xw)
    gu = jnp.dot(lo.astype(BF16), w1_ref[0:half, :], preferred_element_type=F32)
    gu = gu + jnp.dot(hi.astype(BF16), w1_ref[half:, :], preferred_element_type=F32)
    hid = gu.shape[1] // 2
    act = _silu(gu[:, :hid]) * gu[:, hid:]
    return jnp.dot(act.astype(BF16), w2_ref[...], preferred_element_type=F32)


def _dispatch_kernel(meta_ref, dest_ref, h_ref, w1_ref, w2_ref, xs_ref, sh_ref, zero_ref,
                     sem_z, sem_r, *, td):

    @pl.when(pl.program_id(0) == 0)
    def _():
        _zero_rows(zero_ref)

        blk_rows = MOE_BLOCK * SUBLANES

        def block_copy(first_token):
            start = pl.multiple_of(first_token * SUBLANES, blk_rows)
            return pltpu.make_async_copy(zero_ref, xs_ref.at[pl.ds(start, blk_rows)], sem_z)

        def tail_copy(e):
            return block_copy(meta_ref[0, e] + meta_ref[2, e] - MOE_BLOCK)

        def unused_copy(j):
            return block_copy((meta_ref[3, 0] + j) * MOE_BLOCK)

        n_blocks = xs_ref.shape[0] // blk_rows
        for e in range(N_EXPERTS):
            @pl.when(meta_ref[2, e] > meta_ref[1, e])
            def _():
                tail_copy(e).start()

            @pl.when(meta_ref[3, 0] + e < n_blocks)
            def _():
                unused_copy(e).start()
        for e in range(N_EXPERTS):
            @pl.when(meta_ref[2, e] > meta_ref[1, e])
            def _():
                tail_copy(e).wait()

            @pl.when(meta_ref[3, 0] + e < n_blocks)
            def _():
                unused_copy(e).wait()

    def issue(r, carry):
        src = _token(h_ref, r)
        for k in range(TOP_K):
            pltpu.make_async_copy(src, _token(xs_ref, dest_ref[k, r]), sem_r).start(priority=k % 2)
        return carry

    lax.fori_loop(0, td, issue, 0)

    sh_ref[...] = _swiglu_packed(h_ref, w1_ref, w2_ref).astype(sh_ref.dtype)

    for k in range(TOP_K):
        pltpu.make_async_copy(h_ref, xs_ref.at[pl.ds(0, td * SUBLANES)], sem_r).wait()


def _dispatch_call(meta, dest, h2p, w_sh_in_bf, w_sh_out_bf, n_rows):
    t = h2p.shape[0] // SUBLANES
    d = 2 * SUBLANES * LANES
    td = min(256, t)
    return pl.pallas_call(
        functools.partial(_dispatch_kernel, td=td),
        grid=(t // td,),
        in_specs=[pl.BlockSpec(memory_space=pltpu.SMEM),
                  pl.BlockSpec((TOP_K, td), lambda i: (0, i), memory_space=pltpu.SMEM),
                  pl.BlockSpec((td * SUBLANES, LANES), lambda i: (i, 0)),
                  pl.BlockSpec(w_sh_in_bf.shape, lambda i: (0, 0)),
                  pl.BlockSpec(w_sh_out_bf.shape, lambda i: (0, 0))],
        out_specs=[pl.BlockSpec(memory_space=pl.ANY),
                   pl.BlockSpec((td, d), lambda i: (i, 0))],
        out_shape=[jax.ShapeDtypeStruct((n_rows * SUBLANES, LANES), U32),
                   jax.ShapeDtypeStruct((t, d), BF16)],
        scratch_shapes=[pltpu.VMEM((MOE_BLOCK * SUBLANES, LANES), U32),
                        pltpu.SemaphoreType.DMA, pltpu.SemaphoreType.DMA],
        compiler_params=_params(("arbitrary",)),
        name="dispatch_shared",
    )(meta, dest, h2p, w_sh_in_bf, w_sh_out_bf)


def _expert_kernel(meta_ref, xs_ref, w1_hbm, w2_hbm, ys_ref, w1f, w2f, w1b_ref, w2b_ref, xbuf, ybuf,
                   sem_w, sem_in, sem_out, *, n_blocks):
    e = pl.program_id(0)
    n_exp = pl.num_programs(0)
    blk_rows = MOE_BLOCK * SUBLANES
    wslot = lax.rem(e, 2)
    low = 1

    def weights(ex, s):
        return (pltpu.make_async_copy(w1_hbm.at[ex], w1f.at[s], sem_w.at[0, s]),
                pltpu.make_async_copy(w2_hbm.at[ex], w2f.at[s], sem_w.at[1, s]))

    @pl.when(e == 0)
    def _():
        for cp in weights(0, 0):
            cp.start()

    for cp in weights(e, wslot):
        cp.wait()

    @pl.when(e + 1 < n_exp)
    def _():
        for cp in weights(e + 1, 1 - wslot):
            cp.start(priority=low)
    first = meta_ref[0, e] // MOE_BLOCK
    nblk = meta_ref[2, e] // MOE_BLOCK
    nfull = nblk // 2

    def rows_of(b):
        return pl.ds(pl.multiple_of((first + b) * blk_rows, blk_rows), blk_rows)

    def pair_sem(s):
        return (s // 2) * 2 if isinstance(s, int) else lax.div(s, 2) * 2

    def fetch(b, s, base=None):
        src = rows_of(b) if base is None else pl.ds(
            pl.multiple_of((base + b) * blk_rows, blk_rows), blk_rows)
        return pltpu.make_async_copy(xs_ref.at[src], xbuf.at[s], sem_in.at[pair_sem(s)])

    def store(b, s):
        return pltpu.make_async_copy(ybuf.at[s], ys_ref.at[rows_of(b)],
                                     sem_out.at[pair_sem(s)])

    def compute(s):
        out = _swiglu_packed(xbuf.at[s], w1b_ref, w2b_ref)
        half = out.shape[1] // 2
        _store_rows(ybuf.at[s], _pack2(out[:, :half], out[:, half:]))

    odd = lax.rem(nblk, 2) == 1
    tail_slot = 4
    row_queue = 0

    @pl.when(nblk > 0)
    def _():
        @pl.when(jnp.logical_and(nfull > 0, e == 0))
        def _():
            fetch(0, 0).start(priority=row_queue)
            fetch(1, 1).start(priority=row_queue)

        @pl.when(odd)
        def _():
            fetch(nblk - 1, tail_slot).start(priority=row_queue)

        w1b_ref[...] = w1f[wslot].astype(BF16)
        w2b_ref[...] = w2f[wslot].astype(BF16)

    def drain(n_blocks_of):
        nf = n_blocks_of // 2
        for back in (2, 1):
            @pl.when(nf >= back)
            def _():
                s0 = 2 * lax.rem(nf - back, 2)
                store(0, s0).wait()
                store(0, s0 + 1).wait()

        @pl.when(lax.rem(n_blocks_of, 2) == 1)
        def _():
            store(0, tail_slot).wait()

    @pl.when(e > 0)
    def _():
        drain(meta_ref[2, jnp.maximum(e - 1, 0)] // MOE_BLOCK)

    @pl.when(nblk > 0)
    def _():
        def pair(p, carry):
            s0 = 2 * lax.rem(p, 2)
            n0 = 2 - s0
            fetch(2 * p, s0).wait()
            fetch(2 * p + 1, s0 + 1).wait()

            @pl.when(p + 1 < nfull)
            def _():
                fetch(2 * p + 2, n0).start(priority=row_queue)
                fetch(2 * p + 3, n0 + 1).start(priority=row_queue)

            @pl.when(p >= 2)
            def _():
                store(2 * p - 4, s0).wait()
                store(2 * p - 3, s0 + 1).wait()

            compute(s0)
            compute(s0 + 1)
            store(2 * p, s0).start(priority=row_queue)
            store(2 * p + 1, s0 + 1).start(priority=row_queue)
            return carry

        lax.fori_loop(0, nfull, pair, 0)

        @pl.when(odd)
        def _():
            fetch(nblk - 1, tail_slot).wait()
            compute(tail_slot)
            store(nblk - 1, tail_slot).start(priority=row_queue)

    @pl.when(e + 1 < n_exp)
    def _():
        nxt = jnp.minimum(e + 1, n_exp - 1)
        nxt_first = meta_ref[0, nxt] // MOE_BLOCK

        @pl.when(meta_ref[2, nxt] // MOE_BLOCK >= 2)
        def _():
            fetch(0, 0, base=nxt_first).start(priority=row_queue)
            fetch(1, 1, base=nxt_first).start(priority=row_queue)

    @pl.when(e == pl.num_programs(0) - 1)
    def _():
        drain(nblk)
        _zero_rows(ybuf.at[0])

        def unused(j):
            start = pl.multiple_of((meta_ref[3, 0] + j) * blk_rows, blk_rows)
            return pltpu.make_async_copy(ybuf.at[0], ys_ref.at[pl.ds(start, blk_rows)],
                                         sem_out.at[0])

        for j in range(N_EXPERTS):
            @pl.when(meta_ref[3, 0] + j < n_blocks)
            def _():
                unused(j).start()
        for j in range(N_EXPERTS):
            @pl.when(meta_ref[3, 0] + j < n_blocks)
            def _():
                unused(j).wait()


def _expert_call(meta, xs, w_exp_in, w_exp_out):
    blk_rows = MOE_BLOCK * SUBLANES
    n_exp, d, h2 = w_exp_in.shape
    hid = w_exp_out.shape[1]
    n_blocks = xs.shape[0] // blk_rows
    return pl.pallas_call(
        functools.partial(_expert_kernel, n_blocks=n_blocks),
        grid_spec=pltpu.PrefetchScalarGridSpec(
            num_scalar_prefetch=1, grid=(n_exp,),
            in_specs=[pl.BlockSpec(memory_space=pl.ANY),
                      pl.BlockSpec(memory_space=pl.ANY),
                      pl.BlockSpec(memory_space=pl.ANY)],
            out_specs=pl.BlockSpec(memory_space=pl.ANY),
            scratch_shapes=[pltpu.VMEM((2, d, h2), F32), pltpu.VMEM((2, hid, d), F32),
                            pltpu.VMEM((d, h2), BF16), pltpu.VMEM((hid, d), BF16),
                            pltpu.VMEM((5, blk_rows, LANES), U32),
                            pltpu.VMEM((5, blk_rows, LANES), U32),
                            pltpu.SemaphoreType.DMA((2, 2)),
                            pltpu.SemaphoreType.DMA((5,)), pltpu.SemaphoreType.DMA((5,))]),
        out_shape=jax.ShapeDtypeStruct(xs.shape, U32),
        compiler_params=_params(("arbitrary",)),
        name="experts",
    )(meta, xs, w_exp_in, w_exp_out)


def _combine_kernel(dest_ref, dnext_ref, w_ref, ys_ref, sh_ref, x1_ref, mod_ref, g_ref, o_ref,
                    buf_ref, sem, *, tc, final_norm):
    i = pl.program_id(0)
    slot = lax.rem(i, 2)

    def gather(d_ref, s):
        def issue(r, carry):
            for k in range(TOP_K):
                pltpu.make_async_copy(_token(ys_ref, d_ref[k, r]), _token(buf_ref.at[s, k], r),
                                      sem.at[s]).start(priority=k % 2)
            return carry

        lax.fori_loop(0, tc, issue, 0)

    @pl.when(i == 0)
    def _():
        gather(dest_ref, 0)

    @pl.when(i + 1 < pl.num_programs(0))
    def _():
        gather(dnext_ref, 1 - slot)

    for k in range(TOP_K):
        pltpu.make_async_copy(ys_ref.at[pl.ds(0, tc * SUBLANES)], buf_ref.at[slot, k],
                              sem.at[slot]).wait()

    acc_lo = acc_hi = None
    for k in range(TOP_K):
        wk = w_ref[:, k:k + 1]
        lo, hi = _unpack2(_load_rows(buf_ref.at[slot, k]))
        acc_lo = wk * lo if acc_lo is None else acc_lo + wk * lo
        acc_hi = wk * hi if acc_hi is None else acc_hi + wk * hi
    routed = jnp.concatenate([acc_lo, acc_hi], axis=1)
    moe = routed + sh_ref[...].astype(F32)
    x2 = x1_ref[...] + mod_ref[5:6, :] * moe
    o_ref[...] = _rms(x2) * g_ref[...] if final_norm else x2


def _combine_call(dest, wsel_t, ys, shared, x1, mod3, g_final, seq, final_norm):
    t, d = x1.shape
    tc = min(256, seq)
    per_b = seq // tc
    n_tiles = t // tc
    return pl.pallas_call(
        functools.partial(_combine_kernel, tc=tc, final_norm=final_norm),
        grid=(n_tiles,),
        in_specs=[pl.BlockSpec((TOP_K, tc), lambda i: (0, i), memory_space=pltpu.SMEM),
                  pl.BlockSpec((TOP_K, tc), lambda i: (0, jnp.minimum(i + 1, n_tiles - 1)),
                               memory_space=pltpu.SMEM),
                  pl.BlockSpec((tc, TOP_K), lambda i: (i, 0)),
                  pl.BlockSpec(memory_space=pl.ANY),
                  pl.BlockSpec((tc, d), lambda i: (i, 0)),
                  pl.BlockSpec((tc, d), lambda i: (i, 0)),
                  pl.BlockSpec((None, 6, d), lambda i: (i // per_b, 0, 0)),
                  pl.BlockSpec((1, d), lambda i: (0, 0))],
        out_specs=pl.BlockSpec((tc, d), lambda i: (i, 0)),
        out_shape=jax.ShapeDtypeStruct((t, d), F32),
        scratch_shapes=[pltpu.VMEM((2, TOP_K, tc * SUBLANES, LANES), U32),
                        pltpu.SemaphoreType.DMA((2,))],
        compiler_params=_params(("arbitrary",)),
        name="combine_final",
    )(dest, dest, wsel_t, ys, shared, x1, mod3, g_final.reshape(1, d))


def _moe_plan(counts):
    pcounts = (counts + MOE_BLOCK - 1) // MOE_BLOCK * MOE_BLOCK
    pends = jnp.cumsum(pcounts)
    pstarts = pends - pcounts
    return pstarts.astype(I32), pcounts.astype(I32), pends.astype(I32)


def kernel(x, c, positions, w_ada, b_ada, g_mix, w_in, w_conv, b_conv, w_rg_a, b_rg_a, w_rg_x, b_rg_x, lam, beta_ret, beta_lru, w_out, g_moe, w_router, b_router, w_exp_in, w_exp_out, w_sh_in, w_sh_out, g_final):
    batch, seq, d = x.shape
    t = batch * seq
    depth = w_ada.shape[0]
    xcur = x.reshape(t, d)
    cos, sin = _rope_call(positions)
    n_rows = (t * TOP_K + MOE_BLOCK - 1) // MOE_BLOCK * MOE_BLOCK + N_EXPERTS * MOE_BLOCK
    for l in range(depth):
        mod3 = _mod_call(c, w_ada[l], b_ada[l]).reshape(batch, 6, d)
        proj = _in_call(xcur, mod3, g_mix[l], w_in[l].astype(BF16), seq)
        ret = _ret_call(proj, cos, sin, beta_ret[l], batch, seq)
        lru = _lru_call(proj, w_conv[l], b_conv[l], w_rg_a[l], b_rg_a[l], w_rg_x[l], b_rg_x[l],
                        lam[l], beta_lru[l], batch, seq)
        x1, h2p = _out_call(ret, lru, xcur, mod3, g_moe[l], w_out[l].astype(BF16), seq)

        eidx, wsel, pos, cnt = _route_call(h2p, w_router[l], b_router[l])
        counts = cnt[:, 0].astype(I32)
        pstarts, pcounts, pends = _moe_plan(counts)
        dest = _dest_call(pstarts, eidx, pos)
        n_used = (pends[-1] // MOE_BLOCK).astype(I32)
        meta = jnp.stack([pstarts, counts, pcounts, jnp.full_like(counts, n_used)])

        xs, shared = _dispatch_call(meta, dest, h2p, w_sh_in[l].astype(BF16),
                                    w_sh_out[l].astype(BF16), n_rows)
        ys = _expert_call(meta, xs, w_exp_in[l], w_exp_out[l])
        xcur = _combine_call(dest, wsel.T, ys, shared, x1, mod3, g_final, seq,
                             final_norm=(l == depth - 1))
    return xcur.reshape(batch, seq, d)
```

```python
import functools

import numpy as np
import jax
import jax.numpy as jnp
from jax import lax
from jax.experimental import pallas as pl
from jax.experimental.pallas import tpu as pltpu

F32 = jnp.float32
BF16 = jnp.bfloat16
I32 = jnp.int32
U32 = jnp.uint32

RET_HEADS = 8
HEAD_DIM = 128
CHUNK = 128
ROPE_THETA = 10000.0
LRU_BLOCKS = 8
CONV_WIDTH = 4
LRU_C = 8.0
N_EXPERTS = 64
N_GROUPS = 8
GROUP_SIZE = N_EXPERTS // N_GROUPS
TOPK_GROUPS = 4
TOP_K = 8
ROUTED_SCALE = 2.5
EPS = 1e-6

LANES = 128
SUBLANES = 8
VMEM_LIMIT_BYTES = 56 * 1024 * 1024

MOE_BLOCK = 256
NEG_INF = float("-inf")


def _params(semantics, vmem=VMEM_LIMIT_BYTES):
    return pltpu.CompilerParams(dimension_semantics=semantics, vmem_limit_bytes=vmem)


def _silu(x):
    return x * jax.nn.sigmoid(x)


def _rms(x):
    return x * lax.rsqrt(jnp.mean(x * x, axis=-1, keepdims=True) + EPS)


def _pack2(lo, hi):
    return pltpu.pack_elementwise([lo, hi], packed_dtype=jnp.bfloat16)


def _unpack2(w):
    lo = pltpu.unpack_elementwise(w, index=0, packed_dtype=jnp.bfloat16, unpacked_dtype=F32)
    hi = pltpu.unpack_elementwise(w, index=1, packed_dtype=jnp.bfloat16, unpacked_dtype=F32)
    return lo, hi


def _load_rows(ref):
    rows = ref.shape[0] // SUBLANES
    return jnp.concatenate([ref[pl.ds(j, rows, stride=SUBLANES), :] for j in range(SUBLANES)],
                           axis=1)


def _store_rows(ref, v):
    rows = ref.shape[0] // SUBLANES
    for j in range(SUBLANES):
        ref[pl.ds(j, rows, stride=SUBLANES), :] = v[:, j * LANES:(j + 1) * LANES]


def _zero_rows(ref):
    rows = ref.shape[0] // SUBLANES
    z = jnp.zeros((rows, LANES), F32)
    w = _pack2(z, z)
    for j in range(SUBLANES):
        ref[pl.ds(j, rows, stride=SUBLANES), :] = w


def _token(ref, r):
    return ref.at[pl.ds(pl.multiple_of(r * SUBLANES, SUBLANES), SUBLANES)]


def _mod_kernel(c_ref, w_ref, b_ref, o_ref):
    cs = _silu(c_ref[...])
    o_ref[...] = jnp.dot(cs.astype(BF16), w_ref[...].astype(BF16),
                         preferred_element_type=F32) + b_ref[...]


def _mod_call(c, w_ada, b_ada):
    b, d = c.shape
    n = w_ada.shape[1]
    tn = 1024
    return pl.pallas_call(
        _mod_kernel,
        grid=(n // tn,),
        in_specs=[pl.BlockSpec((b, d), lambda j: (0, 0)),
                  pl.BlockSpec((d, tn), lambda j: (0, j)),
                  pl.BlockSpec((1, tn), lambda j: (0, j))],
        out_specs=pl.BlockSpec((b, tn), lambda j: (0, j)),
        out_shape=jax.ShapeDtypeStruct((b, n), F32),
        compiler_params=_params(("parallel",)),
        name="adaln_mod",
    )(c, w_ada, b_ada.reshape(1, n))


def _in_kernel(x_ref, mod_ref, g_ref, w_ref, o_ref, h_ref):
    j = pl.program_id(1)
    tm = x_ref.shape[0]
    chunk = min(256, tm)

    @pl.when(j == 0)
    def _():
        for r0 in range(0, tm, chunk):
            rows = slice(r0, r0 + chunk)
            y = _rms(x_ref[rows, :]) * g_ref[...]
            h = (y * (1.0 + mod_ref[1:2, :]) + mod_ref[0:1, :]).astype(BF16)
            h_ref[rows, :] = h
            o_ref[rows, :] = jnp.dot(h, w_ref[...],
                                     preferred_element_type=F32).astype(o_ref.dtype)

    @pl.when(j > 0)
    def _():
        o_ref[...] = jnp.dot(h_ref[...], w_ref[...],
                             preferred_element_type=F32).astype(o_ref.dtype)


def _in_call(x2, mod3, g_mix, w_in_bf, seq):
    t, d = x2.shape
    n_out = w_in_bf.shape[1] // 1024
    tm = min(1024, seq)
    per_b = seq // tm
    return pl.pallas_call(
        _in_kernel,
        grid=(t // tm, n_out),
        in_specs=[pl.BlockSpec((tm, d), lambda i, j: (i, 0)),
                  pl.BlockSpec((None, 6, d), lambda i, j: (i // per_b, 0, 0)),
                  pl.BlockSpec((1, d), lambda i, j: (0, 0)),
                  pl.BlockSpec((d, 1024), lambda i, j: (0, j))],
        out_specs=pl.BlockSpec((None, tm, 1024), lambda i, j: (j, i, 0)),
        out_shape=jax.ShapeDtypeStruct((n_out, t, 1024), BF16),
        scratch_shapes=[pltpu.VMEM((tm, d), BF16)],
        compiler_params=_params(("parallel", "arbitrary")),
        name="in_proj",
    )(x2, mod3, g_mix.reshape(1, d), w_in_bf)


def _rope_kernel(pos_ref, invf_ref, sign_ref, cos_ref, sin_ref):
    ang = pos_ref[...].astype(F32) * invf_ref[...]
    cos_ref[...] = jnp.cos(ang)
    sin_ref[...] = jnp.sin(ang) * sign_ref[...]


def _rope_call(positions):
    t = positions.size
    tp = min(2048, t)
    half = HEAD_DIM // 2
    inv_freq = 1.0 / (ROPE_THETA ** (jnp.arange(0, HEAD_DIM, 2, dtype=F32) / HEAD_DIM))
    invf = jnp.concatenate([inv_freq, inv_freq]).reshape(1, HEAD_DIM)
    sign = jnp.concatenate([-jnp.ones((half,), F32), jnp.ones((half,), F32)]).reshape(1, HEAD_DIM)
    return pl.pallas_call(
        _rope_kernel,
        grid=(t // tp,),
        in_specs=[pl.BlockSpec((tp, 1), lambda i: (i, 0)),
                  pl.BlockSpec((1, HEAD_DIM), lambda i: (0, 0)),
                  pl.BlockSpec((1, HEAD_DIM), lambda i: (0, 0))],
        out_specs=[pl.BlockSpec((tp, HEAD_DIM), lambda i: (i, 0)),
                   pl.BlockSpec((tp, HEAD_DIM), lambda i: (i, 0))],
        out_shape=[jax.ShapeDtypeStruct((t, HEAD_DIM), F32)] * 2,
        compiler_params=_params(("parallel",)),
        name="rope_tables",
    )(positions.reshape(t, 1), invf, sign)


def _ret_tables():
    h = np.arange(RET_HEADS, dtype=np.float64)
    lg = np.log1p(-np.exp2(-5.0 - h))
    idx = np.arange(CHUNK, dtype=np.float64)
    diff = idx[:, None] - idx[None, :]
    intra = np.where(diff >= 0, np.exp(lg[:, None, None] * np.maximum(diff, 0.0)), 0.0)
    kdec = np.exp(lg[:, None] * (CHUNK - 1.0 - idx)[None, :])
    qdec = np.exp(lg[:, None] * (idx + 1.0)[None, :])
    cdec = np.exp(lg * CHUNK)
    bc = lambda a: np.ascontiguousarray(np.broadcast_to(a[:, :, None], (RET_HEADS, CHUNK, HEAD_DIM)))
    return (intra.astype(np.float32), bc(qdec).astype(np.float32), bc(kdec).astype(np.float32),
            [float(v) for v in cdec])


def _ret_kernel(q_ref, k_ref, v_ref, g_ref, cos_ref, sin_ref, intra_ref, qdec_ref, kdec_ref,
                beta_ref, o_ref, st_ref, *, cdec):
    @pl.when(pl.program_id(1) == 0)
    def _():
        st_ref[...] = jnp.zeros_like(st_ref)

    scale = HEAD_DIM ** -0.5
    nt = (((1,), (1,)), ((), ()))
    tn = (((0,), (0,)), ((), ()))
    n_chunks = q_ref.shape[0] // CHUNK
    outs = [[] for _ in range(n_chunks)]
    for h in range(RET_HEADS):
        sl = slice(h * HEAD_DIM, (h + 1) * HEAD_DIM)
        st = st_ref[h]
        for c in range(n_chunks):
            rows = slice(c * CHUNK, (c + 1) * CHUNK)
            cos = cos_ref[rows, :]
            sin = sin_ref[rows, :]
            q = q_ref[rows, sl].astype(F32)
            k = k_ref[rows, sl].astype(F32)
            v = v_ref[rows, sl]
            qr = (q * cos + pltpu.roll(q, HEAD_DIM // 2, 1) * sin) * scale
            kr = k * cos + pltpu.roll(k, HEAD_DIM // 2, 1) * sin
            s = lax.dot_general(qr.astype(BF16), kr.astype(BF16), nt,
                                preferred_element_type=F32) * intra_ref[h]
            inner = jnp.dot(s.astype(BF16), v, preferred_element_type=F32)
            cross = jnp.dot((qr * qdec_ref[h]).astype(BF16), st.astype(BF16),
                            preferred_element_type=F32)
            o = inner + cross
            kv = lax.dot_general((kr * kdec_ref[h]).astype(BF16), v, tn,
                                 preferred_element_type=F32)
            st = cdec[h] * st + kv
            oc = o - jnp.mean(o, axis=-1, keepdims=True)
            on = oc * lax.rsqrt(jnp.mean(oc * oc, axis=-1, keepdims=True) + EPS)
            outs[c].append(_silu(g_ref[rows, sl].astype(F32)) * on)
        st_ref[h] = st
    for c in range(n_chunks):
        ret = jnp.concatenate(outs[c], axis=1)
        o_ref[c * CHUNK:(c + 1) * CHUNK, :] = (_rms(ret) * beta_ref[...]).astype(o_ref.dtype)


def _ret_call(proj, cos, sin, beta_ret, batch, seq):
    t = batch * seq
    rw = RET_HEADS * HEAD_DIM
    tr = min(4 * CHUNK, seq)
    nc = seq // tr
    intra, qdec, kdec, cdec = _ret_tables()
    row = lambda which: pl.BlockSpec((None, tr, rw), lambda b, n: (which, b * nc + n, 0))
    tab = lambda: pl.BlockSpec((RET_HEADS, CHUNK, HEAD_DIM), lambda b, n: (0, 0, 0))
    cs = lambda: pl.BlockSpec((tr, HEAD_DIM), lambda b, n: (b * nc + n, 0))
    return pl.pallas_call(
        functools.partial(_ret_kernel, cdec=cdec),
        grid=(batch, nc),
        in_specs=[row(0), row(1), row(2), row(3), cs(), cs(), tab(), tab(), tab(),
                  pl.BlockSpec((1, rw), lambda b, n: (0, 0))],
        out_specs=pl.BlockSpec((tr, rw), lambda b, n: (b * nc + n, 0)),
        out_shape=jax.ShapeDtypeStruct((t, rw), BF16),
        scratch_shapes=[pltpu.VMEM((RET_HEADS, HEAD_DIM, HEAD_DIM), F32)],
        compiler_params=_params(("parallel", "arbitrary")),
        name="retention",
    )(proj, proj, proj, proj, cos, sin, jnp.asarray(intra), jnp.asarray(qdec), jnp.asarray(kdec),
      beta_ret.reshape(1, rw))


def _lru_kernel(xr_ref, yr_ref, wc_ref, bc_ref, wg_ref, ba_ref, bx_ref, lam_ref, beta_ref,
                o_ref, xtail_ref, h_ref, *, ts):
    @pl.when(pl.program_id(1) == 0)
    def _():
        xtail_ref[...] = jnp.zeros_like(xtail_ref)
        h_ref[...] = jnp.zeros_like(h_ref)

    x = xr_ref[...].astype(F32)
    w = x.shape[1]
    rows = lax.broadcasted_iota(I32, (ts, w), 0)
    head_rows = lax.broadcasted_iota(I32, (SUBLANES, w), 0)
    tail = xtail_ref[...]
    xc = bc_ref[...]
    for tap in range(CONV_WIDTH):
        back = CONV_WIDTH - 1 - tap
        if back == 0:
            xs = x
        else:
            rolled = pltpu.roll(x, back, 0)
            head = jnp.where(head_rows < back, pltpu.roll(tail, back, 0), rolled[:SUBLANES])
            xs = jnp.concatenate([head, rolled[SUBLANES:]], axis=0)
        xc = xc + xs * wc_ref[tap:tap + 1, :]
    xtail_ref[...] = x[ts - SUBLANES:ts, :]

    xcb = xc.astype(BF16)
    bd = w // LRU_BLOCKS
    rs, gs = [], []
    for g in range(LRU_BLOCKS):
        z = jnp.dot(xcb[:, g * bd:(g + 1) * bd], wg_ref[g], preferred_element_type=F32)
        rs.append(z[:, :bd])
        gs.append(z[:, bd:])
    r = jax.nn.sigmoid(jnp.concatenate(rs, axis=1) + ba_ref[...])
    i = jax.nn.sigmoid(jnp.concatenate(gs, axis=1) + bx_ref[...])
    log_a = -LRU_C * r * jax.nn.softplus(-lam_ref[...])
    a = jnp.exp(log_a)
    bt = jnp.sqrt(-jnp.tanh(log_a) * (a * a + 1.0)) * (i * xc)

    acc_a, acc_b = a, bt
    sh = 1
    while sh < ts:
        if sh < SUBLANES:
            m = rows >= sh
            acc_b = jnp.where(m, acc_a * pltpu.roll(acc_b, sh, 0) + acc_b, acc_b)
            acc_a = jnp.where(m, acc_a * pltpu.roll(acc_a, sh, 0), acc_a)
        else:
            new_b = acc_a[sh:] * acc_b[:ts - sh] + acc_b[sh:]
            acc_a = jnp.concatenate([acc_a[:sh], acc_a[sh:] * acc_a[:ts - sh]], axis=0)
            acc_b = jnp.concatenate([acc_b[:sh], new_b], axis=0)
        sh *= 2
    hs = acc_a * h_ref[...] + acc_b
    h_ref[...] = hs[ts - 1:ts, :]

    lru = hs * jax.nn.gelu(yr_ref[...].astype(F32), approximate=True)
    o_ref[...] = (_rms(lru) * beta_ref[...]).astype(o_ref.dtype)


def _lru_call(proj, w_conv, b_conv, w_rg_a, b_rg_a, w_rg_x, b_rg_x, lam, beta_lru, batch, seq):
    t = batch * seq
    w = w_conv.shape[1]
    ts = min(256, seq)
    nt = seq // ts
    wg = jnp.concatenate([w_rg_a, w_rg_x], axis=-1).astype(BF16)
    vec = lambda: pl.BlockSpec((1, w), lambda b, j: (0, 0))
    row = lambda which: pl.BlockSpec((None, ts, w), lambda b, j: (which, b * nt + j, 0))
    return pl.pallas_call(
        functools.partial(_lru_kernel, ts=ts),
        grid=(batch, nt),
        in_specs=[row(4), row(5),
                  pl.BlockSpec((CONV_WIDTH, w), lambda b, j: (0, 0)), vec(),
                  pl.BlockSpec(wg.shape, lambda b, j: (0, 0, 0)),
                  vec(), vec(), vec(), vec()],
        out_specs=pl.BlockSpec((ts, w), lambda b, j: (b * nt + j, 0)),
        out_shape=jax.ShapeDtypeStruct((t, w), BF16),
        scratch_shapes=[pltpu.VMEM((SUBLANES, w), F32), pltpu.VMEM((1, w), F32)],
        compiler_params=_params(("parallel", "arbitrary")),
        name="rg_lru",
    )(proj, proj, w_conv, b_conv.reshape(1, w), wg, b_rg_a.reshape(1, w), b_rg_x.reshape(1, w),
      lam.reshape(1, w), beta_lru.reshape(1, w))


def _out_kernel(ret_ref, lru_ref, x_ref, mod_ref, g_ref, w_ref, x1_ref, h2_ref):
    rw = ret_ref.shape[1]
    tm = x_ref.shape[0]
    chunk = min(256, tm)
    for r0 in range(0, tm, chunk):
        rows = slice(r0, r0 + chunk)
        mix = jnp.dot(ret_ref[rows, :], w_ref[0:rw, :], preferred_element_type=F32)
        mix = mix + jnp.dot(lru_ref[rows, :], w_ref[rw:, :], preferred_element_type=F32)
        x1 = x_ref[rows, :] + mod_ref[2:3, :] * mix
        x1_ref[rows, :] = x1
        h2 = _rms(x1) * g_ref[...] * (1.0 + mod_ref[4:5, :]) + mod_ref[3:4, :]
        half = h2.shape[1] // 2
        _store_rows(h2_ref.at[pl.ds(r0 * SUBLANES, chunk * SUBLANES)],
                    _pack2(h2[:, :half], h2[:, half:]))


def _out_call(ret, lru, x2, mod3, g_moe, w_out_bf, seq):
    t, d = x2.shape
    assert d // 2 == SUBLANES * LANES, "packed token rows are exactly one (8, 128) word tile"
    rw, lw = ret.shape[1], lru.shape[1]
    tm = min(512, seq)
    per_b = seq // tm
    return pl.pallas_call(
        _out_kernel,
        grid=(t // tm,),
        in_specs=[pl.BlockSpec((tm, rw), lambda i: (i, 0)),
                  pl.BlockSpec((tm, lw), lambda i: (i, 0)),
                  pl.BlockSpec((tm, d), lambda i: (i, 0)),
                  pl.BlockSpec((None, 6, d), lambda i: (i // per_b, 0, 0)),
                  pl.BlockSpec((1, d), lambda i: (0, 0)),
                  pl.BlockSpec((rw + lw, d), lambda i: (0, 0))],
        out_specs=[pl.BlockSpec((tm, d), lambda i: (i, 0)),
                   pl.BlockSpec((tm * SUBLANES, LANES), lambda i: (i, 0))],
        out_shape=[jax.ShapeDtypeStruct((t, d), F32),
                   jax.ShapeDtypeStruct((t * SUBLANES, LANES), U32)],
        compiler_params=_params(("parallel",)),
        name="out_proj",
    )(ret, lru, x2, mod3, g_moe.reshape(1, d), w_out_bf)


def _route_kernel(h_ref, wlo_ref, whi_ref, bias_ref, tri_ref, eidx_ref, wsel_ref, pos_ref,
                  cnt_ref, carry_ref):
    @pl.when(pl.program_id(0) == 0)
    def _():
        carry_ref[...] = jnp.zeros_like(carry_ref)

    tr = h_ref.shape[0] // SUBLANES
    lo, hi = _unpack2(_load_rows(h_ref))
    nt = (((1,), (1,)), ((), ()))
    logits = lax.dot_general(wlo_ref[...], lo.astype(BF16), nt, preferred_element_type=F32)
    logits = logits + lax.dot_general(whi_ref[...], hi.astype(BF16), nt,
                                      preferred_element_type=F32)
    scores = jax.nn.sigmoid(logits)
    biased = scores + bias_ref[:, 0:1]
    shape3 = (N_GROUPS, GROUP_SIZE, tr)
    s3 = scores.reshape(shape3)
    b3 = biased.reshape(shape3)
    member = lax.broadcasted_iota(I32, shape3, 1)
    group = lax.broadcasted_iota(I32, shape3, 0)
    expert = group * GROUP_SIZE + member

    m1 = jnp.max(b3, axis=1, keepdims=True)
    i1 = jnp.min(jnp.where(b3 == m1, member, GROUP_SIZE), axis=1, keepdims=True)
    m2 = jnp.max(jnp.where(member == i1, NEG_INF, b3), axis=1, keepdims=True)
    gscore = m1 + m2

    gid = lax.broadcasted_iota(I32, (N_GROUPS, 1, tr), 0)
    rank = jnp.zeros((N_GROUPS, 1, tr), I32)
    for j in range(N_GROUPS):
        gj = gscore[j:j + 1]
        ahead = (gj > gscore) | ((gj == gscore) & (gid > j))
        rank = rank + ahead.astype(I32)
    masked = jnp.where(rank < TOPK_GROUPS, b3, NEG_INF)

    idxs, vals = [], []
    picked = jnp.zeros(shape3, F32)
    for _ in range(TOP_K):
        m = jnp.max(jnp.max(masked, axis=0, keepdims=True), axis=1, keepdims=True)
        cand = jnp.where(masked == m, expert, N_EXPERTS)
        idx = jnp.min(jnp.min(cand, axis=0, keepdims=True), axis=1, keepdims=True)
        hit = expert == idx
        val = jnp.sum(jnp.sum(jnp.where(hit, s3, 0.0), axis=0, keepdims=True), axis=1, keepdims=True)
        masked = jnp.where(hit, NEG_INF, masked)
        picked = jnp.where(hit, 1.0, picked)
        idxs.append(idx)
        vals.append(val)
    total = vals[0]
    for v in vals[1:]:
        total = total + v

    before = jnp.dot(picked.reshape(N_EXPERTS, tr).astype(BF16), tri_ref[...],
                     preferred_element_type=F32)
    carry = carry_ref[...]
    posm = (before + carry[:, 0:1]).reshape(shape3)
    carry = carry + jnp.sum(picked.reshape(N_EXPERTS, tr), axis=1, keepdims=True)
    carry_ref[...] = carry
    cnt_ref[...] = carry

    for k in range(TOP_K):
        hit = expert == idxs[k]
        p = jnp.sum(jnp.sum(jnp.where(hit, posm, 0.0), axis=0, keepdims=True), axis=1, keepdims=True)
        eidx_ref[k:k + 1, :] = idxs[k].reshape(1, tr)
        wsel_ref[k:k + 1, :] = (vals[k] / total * ROUTED_SCALE).reshape(1, tr)
        pos_ref[k:k + 1, :] = p.reshape(1, tr).astype(I32)


def _route_call(h2p, w_router, b_router):
    t = h2p.shape[0] // SUBLANES
    half = SUBLANES * LANES
    tr = min(512, t)
    wt = w_router.T.astype(BF16)
    bias = jnp.broadcast_to(b_router.astype(F32)[:, None], (N_EXPERTS, LANES))
    tri = jnp.asarray(np.triu(np.ones((tr, tr), np.float32), k=1), dtype=BF16)
    kt = lambda dt: jax.ShapeDtypeStruct((TOP_K, t), dt)
    krow = lambda: pl.BlockSpec((TOP_K, tr), lambda i: (0, i))
    return pl.pallas_call(
        _route_kernel,
        grid=(t // tr,),
        in_specs=[pl.BlockSpec((tr * SUBLANES, LANES), lambda i: (i, 0)),
                  pl.BlockSpec((N_EXPERTS, half), lambda i: (0, 0)),
                  pl.BlockSpec((N_EXPERTS, half), lambda i: (0, 0)),
                  pl.BlockSpec((N_EXPERTS, LANES), lambda i: (0, 0)),
                  pl.BlockSpec((tr, tr), lambda i: (0, 0))],
        out_specs=[krow(), krow(), krow(),
                   pl.BlockSpec((N_EXPERTS, LANES), lambda i: (0, 0))],
        out_shape=[kt(I32), kt(F32), kt(I32), jax.ShapeDtypeStruct((N_EXPERTS, LANES), F32)],
        scratch_shapes=[pltpu.VMEM((N_EXPERTS, LANES), F32)],
        compiler_params=_params(("arbitrary",)),
        name="router",
    )(h2p, wt[:, :half], wt[:, half:], bias, tri)


def _dest_kernel(pst_ref, eidx_ref, pos_ref, o_ref):
    e = eidx_ref[...]
    acc = pos_ref[...]
    for j in range(N_EXPERTS):
        acc = acc + jnp.where(e == j, pst_ref[j], 0)
    o_ref[...] = acc


def _dest_call(pstarts, eidx, pos):
    k, t = eidx.shape
    tb = min(2048, t)
    blk = lambda: pl.BlockSpec((k, tb), lambda i, pst: (0, i))
    return pl.pallas_call(
        _dest_kernel,
        grid_spec=pltpu.PrefetchScalarGridSpec(
            num_scalar_prefetch=1, grid=(t // tb,),
            in_specs=[blk(), blk()], out_specs=blk()),
        out_shape=jax.ShapeDtypeStruct((k, t), I32),
        compiler_params=_params(("parallel",)),
        name="dest_rows",
    )(pstarts, eidx, pos)


def _swiglu_packed(x_ref, w1_ref, w2_ref):
    xw = _load_rows(x_ref)
    half = xw.shape[1]
    lo, hi = _unpack2(xw)
    gu = jnp.dot(lo.astype(BF16), w1_ref[0:half, :], preferred_element_type=F32)
    gu = gu + jnp.dot(hi.astype(BF16), w1_ref[half:, :], preferred_element_type=F32)
    hid = gu.shape[1] // 2
    act = _silu(gu[:, :hid]) * gu[:, hid:]
    return jnp.dot(act.astype(BF16), w2_ref[...], preferred_element_type=F32)


def _dispatch_kernel(meta_ref, dest_ref, h_ref, w1_ref, w2_ref, xs_ref, sh_ref, zero_ref,
                     sem_z, sem_r, *, td):

    @pl.when(pl.program_id(0) == 0)
    def _():
        _zero_rows(zero_ref)

        blk_rows = MOE_BLOCK * SUBLANES

        def block_copy(first_token):
            start = pl.multiple_of(first_token * SUBLANES, blk_rows)
            return pltpu.make_async_copy(zero_ref, xs_ref.at[pl.ds(start, blk_rows)], sem_z)

        def tail_copy(e):
            return block_copy(meta_ref[0, e] + meta_ref[2, e] - MOE_BLOCK)

        def unused_copy(j):
            return block_copy((meta_ref[3, 0] + j) * MOE_BLOCK)

        n_blocks = xs_ref.shape[0] // blk_rows
        for e in range(N_EXPERTS):
            @pl.when(meta_ref[2, e] > meta_ref[1, e])
            def _():
                tail_copy(e).start()

            @pl.when(meta_ref[3, 0] + e < n_blocks)
            def _():
                unused_copy(e).start()
        for e in range(N_EXPERTS):
            @pl.when(meta_ref[2, e] > meta_ref[1, e])
            def _():
                tail_copy(e).wait()

            @pl.when(meta_ref[3, 0] + e < n_blocks)
            def _():
                unused_copy(e).wait()

    def issue(r, carry):
        src = _token(h_ref, r)
        for k in range(TOP_K):
            pltpu.make_async_copy(src, _token(xs_ref, dest_ref[k, r]), sem_r).start(priority=k % 2)
        return carry

    lax.fori_loop(0, td, issue, 0)

    sh_ref[...] = _swiglu_packed(h_ref, w1_ref, w2_ref).astype(sh_ref.dtype)

    for k in range(TOP_K):
        pltpu.make_async_copy(h_ref, xs_ref.at[pl.ds(0, td * SUBLANES)], sem_r).wait()


def _dispatch_call(meta, dest, h2p, w_sh_in_bf, w_sh_out_bf, n_rows):
    t = h2p.shape[0] // SUBLANES
    d = 2 * SUBLANES * LANES
    td = min(256, t)
    return pl.pallas_call(
        functools.partial(_dispatch_kernel, td=td),
        grid=(t // td,),
        in_specs=[pl.BlockSpec(memory_space=pltpu.SMEM),
                  pl.BlockSpec((TOP_K, td), lambda i: (0, i), memory_space=pltpu.SMEM),
                  pl.BlockSpec((td * SUBLANES, LANES), lambda i: (i, 0)),
                  pl.BlockSpec(w_sh_in_bf.shape, lambda i: (0, 0)),
                  pl.BlockSpec(w_sh_out_bf.shape, lambda i: (0, 0))],
        out_specs=[pl.BlockSpec(memory_space=pl.ANY),
                   pl.BlockSpec((td, d), lambda i: (i, 0))],
        out_shape=[jax.ShapeDtypeStruct((n_rows * SUBLANES, LANES), U32),
                   jax.ShapeDtypeStruct((t, d), BF16)],
        scratch_shapes=[pltpu.VMEM((MOE_BLOCK * SUBLANES, LANES), U32),
                        pltpu.SemaphoreType.DMA, pltpu.SemaphoreType.DMA],
        compiler_params=_params(("arbitrary",)),
        name="dispatch_shared",
    )(meta, dest, h2p, w_sh_in_bf, w_sh_out_bf)


def _expert_kernel(meta_ref, xs_ref, w1_hbm, w2_hbm, ys_ref, w1f, w2f, w1b_ref, w2b_ref, xbuf, ybuf,
                   sem_w, sem_in, sem_out, *, n_blocks):
    e = pl.program_id(0)
    n_exp = pl.num_programs(0)
    blk_rows = MOE_BLOCK * SUBLANES
    wslot = lax.rem(e, 2)
    low = 1

    def weights(ex, s):
        return (pltpu.make_async_copy(w1_hbm.at[ex], w1f.at[s], sem_w.at[0, s]),
                pltpu.make_async_copy(w2_hbm.at[ex], w2f.at[s], sem_w.at[1, s]))

    @pl.when(e == 0)
    def _():
        for cp in weights(0, 0):
            cp.start()

    for cp in weights(e, wslot):
        cp.wait()

    @pl.when(e + 1 < n_exp)
    def _():
        for cp in weights(e + 1, 1 - wslot):
            cp.start(priority=low)
    first = meta_ref[0, e] // MOE_BLOCK
    nblk = meta_ref[2, e] // MOE_BLOCK
    nfull = nblk // 2

    def rows_of(b):
        return pl.ds(pl.multiple_of((first + b) * blk_rows, blk_rows), blk_rows)

    def pair_sem(s):
        return (s // 2) * 2 if isinstance(s, int) else lax.div(s, 2) * 2

    def fetch(b, s, base=None):
        src = rows_of(b) if base is None else pl.ds(
            pl.multiple_of((base + b) * blk_rows, blk_rows), blk_rows)
        return pltpu.make_async_copy(xs_ref.at[src], xbuf.at[s], sem_in.at[pair_sem(s)])

    def store(b, s):
        return pltpu.make_async_copy(ybuf.at[s], ys_ref.at[rows_of(b)],
                                     sem_out.at[pair_sem(s)])

    def compute(s):
        out = _swiglu_packed(xbuf.at[s], w1b_ref, w2b_ref)
        half = out.shape[1] // 2
        _store_rows(ybuf.at[s], _pack2(out[:, :half], out[:, half:]))

    odd = lax.rem(nblk, 2) == 1
    tail_slot = 4
    row_queue = 0

    @pl.when(nblk > 0)
    def _():
        @pl.when(jnp.logical_and(nfull > 0, e == 0))
        def _():
            fetch(0, 0).start(priority=row_queue)
            fetch(1, 1).start(priority=row_queue)

        @pl.when(odd)
        def _():
            fetch(nblk - 1, tail_slot).start(priority=row_queue)

        w1b_ref[...] = w1f[wslot].astype(BF16)
        w2b_ref[...] = w2f[wslot].astype(BF16)

    def drain(n_blocks_of):
        nf = n_blocks_of // 2
        for back in (2, 1):
            @pl.when(nf >= back)
            def _():
                s0 = 2 * lax.rem(nf - back, 2)
                store(0, s0).wait()
                store(0, s0 + 1).wait()

        @pl.when(lax.rem(n_blocks_of, 2) == 1)
        def _():
            store(0, tail_slot).wait()

    @pl.when(e > 0)
    def _():
        drain(meta_ref[2, jnp.maximum(e - 1, 0)] // MOE_BLOCK)

    @pl.when(nblk > 0)
    def _():
        def pair(p, carry):
            s0 = 2 * lax.rem(p, 2)
            n0 = 2 - s0
            fetch(2 * p, s0).wait()
            fetch(2 * p + 1, s0 + 1).wait()

            @pl.when(p + 1 < nfull)
            def _():
                fetch(2 * p + 2, n0).start(priority=row_queue)
                fetch(2 * p + 3, n0 + 1).start(priority=row_queue)

            @pl.when(p >= 2)
            def _():
                store(2 * p - 4, s0).wait()
                store(2 * p - 3, s0 + 1).wait()

            compute(s0)
            compute(s0 + 1)
            store(2 * p, s0).start(priority=row_queue)
            store(2 * p + 1, s0 + 1).start(priority=row_queue)
            return carry

        lax.fori_loop(0, nfull, pair, 0)

        @pl.when(odd)
        def _():
            fetch(nblk - 1, tail_slot).wait()
            compute(tail_slot)
            store(nblk - 1, tail_slot).start(priority=row_queue)

    @pl.when(e + 1 < n_exp)
    def _():
        nxt = jnp.minimum(e + 1, n_exp - 1)
        nxt_first = meta_ref[0, nxt] // MOE_BLOCK

        @pl.when(meta_ref[2, nxt] // MOE_BLOCK >= 2)
        def _():
            fetch(0, 0, base=nxt_first).start(priority=row_queue)
            fetch(1, 1, base=nxt_first).start(priority=row_queue)

    @pl.when(e == pl.num_programs(0) - 1)
    def _():
        drain(nblk)
        _zero_rows(ybuf.at[0])

        def unused(j):
            start = pl.multiple_of((meta_ref[3, 0] + j) * blk_rows, blk_rows)
            return pltpu.make_async_copy(ybuf.at[0], ys_ref.at[pl.ds(start, blk_rows)],
                                         sem_out.at[0])

        for j in range(N_EXPERTS):
            @pl.when(meta_ref[3, 0] + j < n_blocks)
            def _():
                unused(j).start()
        for j in range(N_EXPERTS):
            @pl.when(meta_ref[3, 0] + j < n_blocks)
            def _():
                unused(j).wait()


def _expert_call(meta, xs, w_exp_in, w_exp_out):
    blk_rows = MOE_BLOCK * SUBLANES
    n_exp, d, h2 = w_exp_in.shape
    hid = w_exp_out.shape[1]
    n_blocks = xs.shape[0] // blk_rows
    return pl.pallas_call(
        functools.partial(_expert_kernel, n_blocks=n_blocks),
        grid_spec=pltpu.PrefetchScalarGridSpec(
            num_scalar_prefetch=1, grid=(n_exp,),
            in_specs=[pl.BlockSpec(memory_space=pl.ANY),
                      pl.BlockSpec(memory_space=pl.ANY),
                      pl.BlockSpec(memory_space=pl.ANY)],
            out_specs=pl.BlockSpec(memory_space=pl.ANY),
            scratch_shapes=[pltpu.VMEM((2, d, h2), F32), pltpu.VMEM((2, hid, d), F32),
                            pltpu.VMEM((d, h2), BF16), pltpu.VMEM((hid, d), BF16),
                            pltpu.VMEM((5, blk_rows, LANES), U32),
                            pltpu.VMEM((5, blk_rows, LANES), U32),
                            pltpu.SemaphoreType.DMA((2, 2)),
                            pltpu.SemaphoreType.DMA((5,)), pltpu.SemaphoreType.DMA((5,))]),
        out_shape=jax.ShapeDtypeStruct(xs.shape, U32),
        compiler_params=_params(("arbitrary",)),
        name="experts",
    )(meta, xs, w_exp_in, w_exp_out)


def _combine_kernel(dest_ref, dnext_ref, w_ref, ys_ref, sh_ref, x1_ref, mod_ref, g_ref, o_ref,
                    buf_ref, sem, *, tc, final_norm):
    i = pl.program_id(0)
    slot = lax.rem(i, 2)

    def gather(d_ref, s):
        def issue(r, carry):
            for k in range(TOP_K):
                pltpu.make_async_copy(_token(ys_ref, d_ref[k, r]), _token(buf_ref.at[s, k], r),
                                      sem.at[s]).start(priority=k % 2)
            return carry

        lax.fori_loop(0, tc, issue, 0)

    @pl.when(i == 0)
    def _():
        gather(dest_ref, 0)

    @pl.when(i + 1 < pl.num_programs(0))
    def _():
        gather(dnext_ref, 1 - slot)

    for k in range(TOP_K):
        pltpu.make_async_copy(ys_ref.at[pl.ds(0, tc * SUBLANES)], buf_ref.at[slot, k],
                              sem.at[slot]).wait()

    acc_lo = acc_hi = None
    for k in range(TOP_K):
        wk = w_ref[:, k:k + 1]
        lo, hi = _unpack2(_load_rows(buf_ref.at[slot, k]))
        acc_lo = wk * lo if acc_lo is None else acc_lo + wk * lo
        acc_hi = wk * hi if acc_hi is None else acc_hi + wk * hi
    routed = jnp.concatenate([acc_lo, acc_hi], axis=1)
    moe = routed + sh_ref[...].astype(F32)
    x2 = x1_ref[...] + mod_ref[5:6, :] * moe
    o_ref[...] = _rms(x2) * g_ref[...] if final_norm else x2


def _combine_call(dest, wsel_t, ys, shared, x1, mod3, g_final, seq, final_norm):
    t, d = x1.shape
    tc = min(256, seq)
    per_b = seq // tc
    n_tiles = t // tc
    return pl.pallas_call(
        functools.partial(_combine_kernel, tc=tc, final_norm=final_norm),
        grid=(n_tiles,),
        in_specs=[pl.BlockSpec((TOP_K, tc), lambda i: (0, i), memory_space=pltpu.SMEM),
                  pl.BlockSpec((TOP_K, tc), lambda i: (0, jnp.minimum(i + 1, n_tiles - 1)),
                               memory_space=pltpu.SMEM),
                  pl.BlockSpec((tc, TOP_K), lambda i: (i, 0)),
                  pl.BlockSpec(memory_space=pl.ANY),
                  pl.BlockSpec((tc, d), lambda i: (i, 0)),
                  pl.BlockSpec((tc, d), lambda i: (i, 0)),
                  pl.BlockSpec((None, 6, d), lambda i: (i // per_b, 0, 0)),
                  pl.BlockSpec((1, d), lambda i: (0, 0))],
        out_specs=pl.BlockSpec((tc, d), lambda i: (i, 0)),
        out_shape=jax.ShapeDtypeStruct((t, d), F32),
        scratch_shapes=[pltpu.VMEM((2, TOP_K, tc * SUBLANES, LANES), U32),
                        pltpu.SemaphoreType.DMA((2,))],
        compiler_params=_params(("arbitrary",)),
        name="combine_final",
    )(dest, dest, wsel_t, ys, shared, x1, mod3, g_final.reshape(1, d))


def _moe_plan(counts):
    pcounts = (counts + MOE_BLOCK - 1) // MOE_BLOCK * MOE_BLOCK
    pends = jnp.cumsum(pcounts)
    pstarts = pends - pcounts
    return pstarts.astype(I32), pcounts.astype(I32), pends.astype(I32)


def kernel(x, c, positions, w_ada, b_ada, g_mix, w_in, w_conv, b_conv, w_rg_a, b_rg_a, w_rg_x, b_rg_x, lam, beta_ret, beta_lru, w_out, g_moe, w_router, b_router, w_exp_in, w_exp_out, w_sh_in, w_sh_out, g_final):
    batch, seq, d = x.shape
    t = batch * seq
    depth = w_ada.shape[0]
    xcur = x.reshape(t, d)
    cos, sin = _rope_call(positions)
    n_rows = (t * TOP_K + MOE_BLOCK - 1) // MOE_BLOCK * MOE_BLOCK + N_EXPERTS * MOE_BLOCK
    for l in range(depth):
        mod3 = _mod_call(c, w_ada[l], b_ada[l]).reshape(batch, 6, d)
        proj = _in_call(xcur, mod3, g_mix[l], w_in[l].astype(BF16), seq)
        ret = _ret_call(proj, cos, sin, beta_ret[l], batch, seq)
        lru = _lru_call(proj, w_conv[l], b_conv[l], w_rg_a[l], b_rg_a[l], w_rg_x[l], b_rg_x[l],
                        lam[l], beta_lru[l], batch, seq)
        x1, h2p = _out_call(ret, lru, xcur, mod3, g_moe[l], w_out[l].astype(BF16), seq)

        eidx, wsel, pos, cnt = _route_call(h2p, w_router[l], b_router[l])
        counts = cnt[:, 0].astype(I32)
        pstarts, pcounts, pends = _moe_plan(counts)
        dest = _dest_call(pstarts, eidx, pos)
        n_used = (pends[-1] // MOE_BLOCK).astype(I32)
        meta = jnp.stack([pstarts, counts, pcounts, jnp.full_like(counts, n_used)])

        xs, shared = _dispatch_call(meta, dest, h2p, w_sh_in[l].astype(BF16),
                                    w_sh_out[l].astype(BF16), n_rows)
        ys = _expert_call(meta, xs, w_exp_in[l], w_exp_out[l])
        xcur = _combine_call(dest, wsel.T, ys, shared, x1, mod3, g_final, seq,
                             final_norm=(l == depth - 1))
    return xcur.reshape(batch, seq, d)
```

```python
import functools

import numpy as np
import jax
import jax.numpy as jnp
from jax import lax
from jax.experimental import pallas as pl
from jax.experimental.pallas import tpu as pltpu

F32 = jnp.float32
BF16 = jnp.bfloat16
I32 = jnp.int32
U32 = jnp.uint32

RET_HEADS = 8
HEAD_DIM = 128
CHUNK = 128
ROPE_THETA = 10000.0
LRU_BLOCKS = 8
CONV_WIDTH = 4
LRU_C = 8.0
N_EXPERTS = 64
N_GROUPS = 8
GROUP_SIZE = N_EXPERTS // N_GROUPS
TOPK_GROUPS = 4
TOP_K = 8
ROUTED_SCALE = 2.5
EPS = 1e-6

LANES = 128
SUBLANES = 8
VMEM_LIMIT_BYTES = 56 * 1024 * 1024

MOE_BLOCK = 256
NEG_INF = float("-inf")


def _params(semantics, vmem=VMEM_LIMIT_BYTES):
    return pltpu.CompilerParams(dimension_semantics=semantics, vmem_limit_bytes=vmem)


def _silu(x):
    return x * jax.nn.sigmoid(x)


def _rms(x):
    return x * lax.rsqrt(jnp.mean(x * x, axis=-1, keepdims=True) + EPS)


def _pack2(lo, hi):
    return pltpu.pack_elementwise([lo, hi], packed_dtype=jnp.bfloat16)


def _unpack2(w):
    lo = pltpu.unpack_elementwise(w, index=0, packed_dtype=jnp.bfloat16, unpacked_dtype=F32)
    hi = pltpu.unpack_elementwise(w, index=1, packed_dtype=jnp.bfloat16, unpacked_dtype=F32)
    return lo, hi


def _load_rows(ref):
    rows = ref.shape[0] // SUBLANES
    return jnp.concatenate([ref[pl.ds(j, rows, stride=SUBLANES), :] for j in range(SUBLANES)],
                           axis=1)


def _store_rows(ref, v):
    rows = ref.shape[0] // SUBLANES
    for j in range(SUBLANES):
        ref[pl.ds(j, rows, stride=SUBLANES), :] = v[:, j * LANES:(j + 1) * LANES]


def _zero_rows(ref):
    rows = ref.shape[0] // SUBLANES
    z = jnp.zeros((rows, LANES), F32)
    w = _pack2(z, z)
    for j in range(SUBLANES):
        ref[pl.ds(j, rows, stride=SUBLANES), :] = w


def _token(ref, r):
    return ref.at[pl.ds(pl.multiple_of(r * SUBLANES, SUBLANES), SUBLANES)]


def _mod_kernel(c_ref, w_ref, b_ref, o_ref):
    cs = _silu(c_ref[...])
    o_ref[...] = jnp.dot(cs.astype(BF16), w_ref[...].astype(BF16),
                         preferred_element_type=F32) + b_ref[...]


def _mod_call(c, w_ada, b_ada):
    b, d = c.shape
    n = w_ada.shape[1]
    tn = 1024
    return pl.pallas_call(
        _mod_kernel,
        grid=(n // tn,),
        in_specs=[pl.BlockSpec((b, d), lambda j: (0, 0)),
                  pl.BlockSpec((d, tn), lambda j: (0, j)),
                  pl.BlockSpec((1, tn), lambda j: (0, j))],
        out_specs=pl.BlockSpec((b, tn), lambda j: (0, j)),
        out_shape=jax.ShapeDtypeStruct((b, n), F32),
        compiler_params=_params(("parallel",)),
        name="adaln_mod",
    )(c, w_ada, b_ada.reshape(1, n))


def _in_kernel(x_ref, mod_ref, g_ref, w_ref, o_ref, h_ref):
    j = pl.program_id(1)
    tm = x_ref.shape[0]
    chunk = min(256, tm)

    @pl.when(j == 0)
    def _():
        for r0 in range(0, tm, chunk):
            rows = slice(r0, r0 + chunk)
            y = _rms(x_ref[rows, :]) * g_ref[...]
            h = (y * (1.0 + mod_ref[1:2, :]) + mod_ref[0:1, :]).astype(BF16)
            h_ref[rows, :] = h
            o_ref[rows, :] = jnp.dot(h, w_ref[...],
                                     preferred_element_type=F32).astype(o_ref.dtype)

    @pl.when(j > 0)
    def _():
        o_ref[...] = jnp.dot(h_ref[...], w_ref[...],
                             preferred_element_type=F32).astype(o_ref.dtype)


def _in_call(x2, mod3, g_mix, w_in_bf, seq):
    t, d = x2.shape
    n_out = w_in_bf.shape[1] // 1024
    tm = min(1024, seq)
    per_b = seq // tm
    return pl.pallas_call(
        _in_kernel,
        grid=(t // tm, n_out),
        in_specs=[pl.BlockSpec((tm, d), lambda i, j: (i, 0)),
                  pl.BlockSpec((None, 6, d), lambda i, j: (i // per_b, 0, 0)),
                  pl.BlockSpec((1, d), lambda i, j: (0, 0)),
                  pl.BlockSpec((d, 1024), lambda i, j: (0, j))],
        out_specs=pl.BlockSpec((None, tm, 1024), lambda i, j: (j, i, 0)),
        out_shape=jax.ShapeDtypeStruct((n_out, t, 1024), BF16),
        scratch_shapes=[pltpu.VMEM((tm, d), BF16)],
        compiler_params=_params(("parallel", "arbitrary")),
        name="in_proj",
    )(x2, mod3, g_mix.reshape(1, d), w_in_bf)


def _rope_kernel(pos_ref, invf_ref, sign_ref, cos_ref, sin_ref):
    ang = pos_ref[...].astype(F32) * invf_ref[...]
    cos_ref[...] = jnp.cos(ang)
    sin_ref[...] = jnp.sin(ang) * sign_ref[...]


def _rope_call(positions):
    t = positions.size
    tp = min(2048, t)
    half = HEAD_DIM // 2
    inv_freq = 1.0 / (ROPE_THETA ** (jnp.arange(0, HEAD_DIM, 2, dtype=F32) / HEAD_DIM))
    invf = jnp.concatenate([inv_freq, inv_freq]).reshape(1, HEAD_DIM)
    sign = jnp.concatenate([-jnp.ones((half,), F32), jnp.ones((half,), F32)]).reshape(1, HEAD_DIM)
    return pl.pallas_call(
        _rope_kernel,
        grid=(t // tp,),
        in_specs=[pl.BlockSpec((tp, 1), lambda i: (i, 0)),
                  pl.BlockSpec((1, HEAD_DIM), lambda i: (0, 0)),
                  pl.BlockSpec((1, HEAD_DIM), lambda i: (0, 0))],
        out_specs=[pl.BlockSpec((tp, HEAD_DIM), lambda i: (i, 0)),
                   pl.BlockSpec((tp, HEAD_DIM), lambda i: (i, 0))],
        out_shape=[jax.ShapeDtypeStruct((t, HEAD_DIM), F32)] * 2,
        compiler_params=_params(("parallel",)),
        name="rope_tables",
    )(positions.reshape(t, 1), invf, sign)


def _ret_tables():
    h = np.arange(RET_HEADS, dtype=np.float64)
    lg = np.log1p(-np.exp2(-5.0 - h))
    idx = np.arange(CHUNK, dtype=np.float64)
    diff = idx[:, None] - idx[None, :]
    intra = np.where(diff >= 0, np.exp(lg[:, None, None] * np.maximum(diff, 0.0)), 0.0)
    kdec = np.exp(lg[:, None] * (CHUNK - 1.0 - idx)[None, :])
    qdec = np.exp(lg[:, None] * (idx + 1.0)[None, :])
    cdec = np.exp(lg * CHUNK)
    bc = lambda a: np.ascontiguousarray(np.broadcast_to(a[:, :, None], (RET_HEADS, CHUNK, HEAD_DIM)))
    return (intra.astype(np.float32), bc(qdec).astype(np.float32), bc(kdec).astype(np.float32),
            [float(v) for v in cdec])


def _ret_kernel(q_ref, k_ref, v_ref, g_ref, cos_ref, sin_ref, intra_ref, qdec_ref, kdec_ref,
                beta_ref, o_ref, st_ref, *, cdec):
    @pl.when(pl.program_id(1) == 0)
    def _():
        st_ref[...] = jnp.zeros_like(st_ref)

    scale = HEAD_DIM ** -0.5
    nt = (((1,), (1,)), ((), ()))
    tn = (((0,), (0,)), ((), ()))
    n_chunks = q_ref.shape[0] // CHUNK
    outs = [[] for _ in range(n_chunks)]
    for h in range(RET_HEADS):
        sl = slice(h * HEAD_DIM, (h + 1) * HEAD_DIM)
        st = st_ref[h]
        for c in range(n_chunks):
            rows = slice(c * CHUNK, (c + 1) * CHUNK)
            cos = cos_ref[rows, :]
            sin = sin_ref[rows, :]
            q = q_ref[rows, sl].astype(F32)
            k = k_ref[rows, sl].astype(F32)
            v = v_ref[rows, sl]
            qr = (q * cos + pltpu.roll(q, HEAD_DIM // 2, 1) * sin) * scale
            kr = k * cos + pltpu.roll(k, HEAD_DIM // 2, 1) * sin
            s = lax.dot_general(qr.astype(BF16), kr.astype(BF16), nt,
                                preferred_element_type=F32) * intra_ref[h]
            inner = jnp.dot(s.astype(BF16), v, preferred_element_type=F32)
            cross = jnp.dot((qr * qdec_ref[h]).astype(BF16), st.astype(BF16),
                            preferred_element_type=F32)
            o = inner + cross
            kv = lax.dot_general((kr * kdec_ref[h]).astype(BF16), v, tn,
                                 preferred_element_type=F32)
            st = cdec[h] * st + kv
            oc = o - jnp.mean(o, axis=-1, keepdims=True)
            on = oc * lax.rsqrt(jnp.mean(oc * oc, axis=-1, keepdims=True) + EPS)
            outs[c].append(_silu(g_ref[rows, sl].astype(F32)) * on)
        st_ref[h] = st
    for c in range(n_chunks):
        ret = jnp.concatenate(outs[c], axis=1)
        o_ref[c * CHUNK:(c + 1) * CHUNK, :] = (_rms(ret) * beta_ref[...]).astype(o_ref.dtype)


def _ret_call(proj, cos, sin, beta_ret, batch, seq):
    t = batch * seq
    rw = RET_HEADS * HEAD_DIM
    tr = min(4 * CHUNK, seq)
    nc = seq // tr
    intra, qdec, kdec, cdec = _ret_tables()
    row = lambda which: pl.BlockSpec((None, tr, rw), lambda b, n: (which, b * nc + n, 0))
    tab = lambda: pl.BlockSpec((RET_HEADS, CHUNK, HEAD_DIM), lambda b, n: (0, 0, 0))
    cs = lambda: pl.BlockSpec((tr, HEAD_DIM), lambda b, n: (b * nc + n, 0))
    return pl.pallas_call(
        functools.partial(_ret_kernel, cdec=cdec),
        grid=(batch, nc),
        in_specs=[row(0), row(1), row(2), row(3), cs(), cs(), tab(), tab(), tab(),
                  pl.BlockSpec((1, rw), lambda b, n: (0, 0))],
        out_specs=pl.BlockSpec((tr, rw), lambda b, n: (b * nc + n, 0)),
        out_shape=jax.ShapeDtypeStruct((t, rw), BF16),
        scratch_shapes=[pltpu.VMEM((RET_HEADS, HEAD_DIM, HEAD_DIM), F32)],
        compiler_params=_params(("parallel", "arbitrary")),
        name="retention",
    )(proj, proj, proj, proj, cos, sin, jnp.asarray(intra), jnp.asarray(qdec), jnp.asarray(kdec),
      beta_ret.reshape(1, rw))


def _lru_kernel(xr_ref, yr_ref, wc_ref, bc_ref, wg_ref, ba_ref, bx_ref, lam_ref, beta_ref,
                o_ref, xtail_ref, h_ref, *, ts):
    @pl.when(pl.program_id(1) == 0)
    def _():
        xtail_ref[...] = jnp.zeros_like(xtail_ref)
        h_ref[...] = jnp.zeros_like(h_ref)

    x = xr_ref[...].astype(F32)
    w = x.shape[1]
    rows = lax.broadcasted_iota(I32, (ts, w), 0)
    head_rows = lax.broadcasted_iota(I32, (SUBLANES, w), 0)
    tail = xtail_ref[...]
    xc = bc_ref[...]
    for tap in range(CONV_WIDTH):
        back = CONV_WIDTH - 1 - tap
        if back == 0:
            xs = x
        else:
            rolled = pltpu.roll(x, back, 0)
            head = jnp.where(head_rows < back, pltpu.roll(tail, back, 0), rolled[:SUBLANES])
            xs = jnp.concatenate([head, rolled[SUBLANES:]], axis=0)
        xc = xc + xs * wc_ref[tap:tap + 1, :]
    xtail_ref[...] = x[ts - SUBLANES:ts, :]

    xcb = xc.astype(BF16)
    bd = w // LRU_BLOCKS
    rs, gs = [], []
    for g in range(LRU_BLOCKS):
        z = jnp.dot(xcb[:, g * bd:(g + 1) * bd], wg_ref[g], preferred_element_type=F32)
        rs.append(z[:, :bd])
        gs.append(z[:, bd:])
    r = jax.nn.sigmoid(jnp.concatenate(rs, axis=1) + ba_ref[...])
    i = jax.nn.sigmoid(jnp.concatenate(gs, axis=1) + bx_ref[...])
    log_a = -LRU_C * r * jax.nn.softplus(-lam_ref[...])
    a = jnp.exp(log_a)
    bt = jnp.sqrt(-jnp.tanh(log_a) * (a * a + 1.0)) * (i * xc)

    acc_a, acc_b = a, bt
    sh = 1
    while sh < ts:
        if sh < SUBLANES:
            m = rows >= sh
            acc_b = jnp.where(m, acc_a * pltpu.roll(acc_b, sh, 0) + acc_b, acc_b)
            acc_a = jnp.where(m, acc_a * pltpu.roll(acc_a, sh, 0), acc_a)
        else:
            new_b = acc_a[sh:] * acc_b[:ts - sh] + acc_b[sh:]
            acc_a = jnp.concatenate([acc_a[:sh], acc_a[sh:] * acc_a[:ts - sh]], axis=0)
            acc_b = jnp.concatenate([acc_b[:sh], new_b], axis=0)
        sh *= 2
    hs = acc_a * h_ref[...] + acc_b
    h_ref[...] = hs[ts - 1:ts, :]

    lru = hs * jax.nn.gelu(yr_ref[...].astype(F32), approximate=True)
    o_ref[...] = (_rms(lru) * beta_ref[...]).astype(o_ref.dtype)


def _lru_call(proj, w_conv, b_conv, w_rg_a, b_rg_a, w_rg_x, b_rg_x, lam, beta_lru, batch, seq):
    t = batch * seq
    w = w_conv.shape[1]
    ts = min(256, seq)
    nt = seq // ts
    wg = jnp.concatenate([w_rg_a, w_rg_x], axis=-1).astype(BF16)
    vec = lambda: pl.BlockSpec((1, w), lambda b, j: (0, 0))
    row = lambda which: pl.BlockSpec((None, ts, w), lambda b, j: (which, b * nt + j, 0))
    return pl.pallas_call(
        functools.partial(_lru_kernel, ts=ts),
        grid=(batch, nt),
        in_specs=[row(4), row(5),
                  pl.BlockSpec((CONV_WIDTH, w), lambda b, j: (0, 0)), vec(),
                  pl.BlockSpec(wg.shape, lambda b, j: (0, 0, 0)),
                  vec(), vec(), vec(), vec()],
        out_specs=pl.BlockSpec((ts, w), lambda b, j: (b * nt + j, 0)),
        out_shape=jax.ShapeDtypeStruct((t, w), BF16),
        scratch_shapes=[pltpu.VMEM((SUBLANES, w), F32), pltpu.VMEM((1, w), F32)],
        compiler_params=_params(("parallel", "arbitrary")),
        name="rg_lru",
    )(proj, proj, w_conv, b_conv.reshape(1, w), wg, b_rg_a.reshape(1, w), b_rg_x.reshape(1, w),
      lam.reshape(1, w), beta_lru.reshape(1, w))


def _out_kernel(ret_ref, lru_ref, x_ref, mod_ref, g_ref, w_ref, x1_ref, h2_ref):
    rw = ret_ref.shape[1]
    mix = jnp.dot(ret_ref[...], w_ref[0:rw, :], preferred_element_type=F32)
    mix = mix + jnp.dot(lru_ref[...], w_ref[rw:, :], preferred_element_type=F32)
    x1 = x_ref[...] + mod_ref[2:3, :] * mix
    x1_ref[...] = x1
    h2 = _rms(x1) * g_ref[...] * (1.0 + mod_ref[4:5, :]) + mod_ref[3:4, :]
    half = h2.shape[1] // 2
    _store_rows(h2_ref, _pack2(h2[:, :half], h2[:, half:]))


def _out_call(ret, lru, x2, mod3, g_moe, w_out_bf, seq):
    t, d = x2.shape
    assert d // 2 == SUBLANES * LANES, "packed token rows are exactly one (8, 128) word tile"
    rw, lw = ret.shape[1], lru.shape[1]
    tm = min(512, seq)
    per_b = seq // tm
    return pl.pallas_call(
        _out_kernel,
        grid=(t // tm,),
        in_specs=[pl.BlockSpec((tm, rw), lambda i: (i, 0)),
                  pl.BlockSpec((tm, lw), lambda i: (i, 0)),
                  pl.BlockSpec((tm, d), lambda i: (i, 0)),
                  pl.BlockSpec((None, 6, d), lambda i: (i // per_b, 0, 0)),
                  pl.BlockSpec((1, d), lambda i: (0, 0)),
                  pl.BlockSpec((rw + lw, d), lambda i: (0, 0))],
        out_specs=[pl.BlockSpec((tm, d), lambda i: (i, 0)),
                   pl.BlockSpec((tm * SUBLANES, LANES), lambda i: (i, 0))],
        out_shape=[jax.ShapeDtypeStruct((t, d), F32),
                   jax.ShapeDtypeStruct((t * SUBLANES, LANES), U32)],
        compiler_params=_params(("parallel",)),
        name="out_proj",
    )(ret, lru, x2, mod3, g_moe.reshape(1, d), w_out_bf)


def _route_kernel(h_ref, wlo_ref, whi_ref, bias_ref, tri_ref, eidx_ref, wsel_ref, pos_ref,
                  cnt_ref, carry_ref):
    @pl.when(pl.program_id(0) == 0)
    def _():
        carry_ref[...] = jnp.zeros_like(carry_ref)

    tr = h_ref.shape[0] // SUBLANES
    lo, hi = _unpack2(_load_rows(h_ref))
    nt = (((1,), (1,)), ((), ()))
    logits = lax.dot_general(wlo_ref[...], lo.astype(BF16), nt, preferred_element_type=F32)
    logits = logits + lax.dot_general(whi_ref[...], hi.astype(BF16), nt,
                                      preferred_element_type=F32)
    scores = jax.nn.sigmoid(logits)
    biased = scores + bias_ref[:, 0:1]
    shape3 = (N_GROUPS, GROUP_SIZE, tr)
    s3 = scores.reshape(shape3)
    b3 = biased.reshape(shape3)
    member = lax.broadcasted_iota(I32, shape3, 1)
    group = lax.broadcasted_iota(I32, shape3, 0)
    expert = group * GROUP_SIZE + member

    m1 = jnp.max(b3, axis=1, keepdims=True)
    i1 = jnp.min(jnp.where(b3 == m1, member, GROUP_SIZE), axis=1, keepdims=True)
    m2 = jnp.max(jnp.where(member == i1, NEG_INF, b3), axis=1, keepdims=True)
    gscore = m1 + m2

    gid = lax.broadcasted_iota(I32, (N_GROUPS, 1, tr), 0)
    rank = jnp.zeros((N_GROUPS, 1, tr), I32)
    for j in range(N_GROUPS):
        gj = gscore[j:j + 1]
        ahead = (gj > gscore) | ((gj == gscore) & (gid > j))
        rank = rank + ahead.astype(I32)
    masked = jnp.where(rank < TOPK_GROUPS, b3, NEG_INF)

    idxs, vals = [], []
    picked = jnp.zeros(shape3, F32)
    for _ in range(TOP_K):
        m = jnp.max(jnp.max(masked, axis=0, keepdims=True), axis=1, keepdims=True)
        cand = jnp.where(masked == m, expert, N_EXPERTS)
        idx = jnp.min(jnp.min(cand, axis=0, keepdims=True), axis=1, keepdims=True)
        hit = expert == idx
        val = jnp.sum(jnp.sum(jnp.where(hit, s3, 0.0), axis=0, keepdims=True), axis=1, keepdims=True)
        masked = jnp.where(hit, NEG_INF, masked)
        picked = jnp.where(hit, 1.0, picked)
        idxs.append(idx)
        vals.append(val)
    total = vals[0]
    for v in vals[1:]:
        total = total + v

    before = jnp.dot(picked.reshape(N_EXPERTS, tr).astype(BF16), tri_ref[...],
                     preferred_element_type=F32)
    carry = carry_ref[...]
    posm = (before + carry[:, 0:1]).reshape(shape3)
    carry = carry + jnp.sum(picked.reshape(N_EXPERTS, tr), axis=1, keepdims=True)
    carry_ref[...] = carry
    cnt_ref[...] = carry

    for k in range(TOP_K):
        hit = expert == idxs[k]
        p = jnp.sum(jnp.sum(jnp.where(hit, posm, 0.0), axis=0, keepdims=True), axis=1, keepdims=True)
        eidx_ref[k:k + 1, :] = idxs[k].reshape(1, tr)
        wsel_ref[k:k + 1, :] = (vals[k] / total * ROUTED_SCALE).reshape(1, tr)
        pos_ref[k:k + 1, :] = p.reshape(1, tr).astype(I32)


def _route_call(h2p, w_router, b_router):
    t = h2p.shape[0] // SUBLANES
    half = SUBLANES * LANES
    tr = min(512, t)
    wt = w_router.T.astype(BF16)
    bias = jnp.broadcast_to(b_router.astype(F32)[:, None], (N_EXPERTS, LANES))
    tri = jnp.asarray(np.triu(np.ones((tr, tr), np.float32), k=1), dtype=BF16)
    kt = lambda dt: jax.ShapeDtypeStruct((TOP_K, t), dt)
    krow = lambda: pl.BlockSpec((TOP_K, tr), lambda i: (0, i))
    return pl.pallas_call(
        _route_kernel,
        grid=(t // tr,),
        in_specs=[pl.BlockSpec((tr * SUBLANES, LANES), lambda i: (i, 0)),
                  pl.BlockSpec((N_EXPERTS, half), lambda i: (0, 0)),
                  pl.BlockSpec((N_EXPERTS, half), lambda i: (0, 0)),
                  pl.BlockSpec((N_EXPERTS, LANES), lambda i: (0, 0)),
                  pl.BlockSpec((tr, tr), lambda i: (0, 0))],
        out_specs=[krow(), krow(), krow(),
                   pl.BlockSpec((N_EXPERTS, LANES), lambda i: (0, 0))],
        out_shape=[kt(I32), kt(F32), kt(I32), jax.ShapeDtypeStruct((N_EXPERTS, LANES), F32)],
        scratch_shapes=[pltpu.VMEM((N_EXPERTS, LANES), F32)],
        compiler_params=_params(("arbitrary",)),
        name="router",
    )(h2p, wt[:, :half], wt[:, half:], bias, tri)


def _dest_kernel(pst_ref, eidx_ref, pos_ref, o_ref):
    e = eidx_ref[...]
    acc = pos_ref[...]
    for j in range(N_EXPERTS):
        acc = acc + jnp.where(e == j, pst_ref[j], 0)
    o_ref[...] = acc


def _dest_call(pstarts, eidx, pos):
    k, t = eidx.shape
    tb = min(2048, t)
    blk = lambda: pl.BlockSpec((k, tb), lambda i, pst: (0, i))
    return pl.pallas_call(
        _dest_kernel,
        grid_spec=pltpu.PrefetchScalarGridSpec(
            num_scalar_prefetch=1, grid=(t // tb,),
            in_specs=[blk(), blk()], out_specs=blk()),
        out_shape=jax.ShapeDtypeStruct((k, t), I32),
        compiler_params=_params(("parallel",)),
        name="dest_rows",
    )(pstarts, eidx, pos)


def _swiglu_packed(x_ref, w1_ref, w2_ref):
    xw = _load_rows(x_ref)
    half = xw.shape[1]
    lo, hi = _unpack2(xw)
    gu = jnp.dot(lo.astype(BF16), w1_ref[0:half, :], preferred_element_type=F32)
    gu = gu + jnp.dot(hi.astype(BF16), w1_ref[half:, :], preferred_element_type=F32)
    hid = gu.shape[1] // 2
    act = _silu(gu[:, :hid]) * gu[:, hid:]
    return jnp.dot(act.astype(BF16), w2_ref[...], preferred_element_type=F32)


def _dispatch_kernel(meta_ref, dest_ref, h_ref, w1_ref, w2_ref, xs_ref, sh_ref, zero_ref,
                     sem_z, sem_r, *, td):

    @pl.when(pl.program_id(0) == 0)
    def _():
        _zero_rows(zero_ref)

        blk_rows = MOE_BLOCK * SUBLANES

        def block_copy(first_token):
            start = pl.multiple_of(first_token * SUBLANES, blk_rows)
            return pltpu.make_async_copy(zero_ref, xs_ref.at[pl.ds(start, blk_rows)], sem_z)

        def tail_copy(e):
            return block_copy(meta_ref[0, e] + meta_ref[2, e] - MOE_BLOCK)

        def unused_copy(j):
            return block_copy((meta_ref[3, 0] + j) * MOE_BLOCK)

        n_blocks = xs_ref.shape[0] // blk_rows
        for e in range(N_EXPERTS):
            @pl.when(meta_ref[2, e] > meta_ref[1, e])
            def _():
                tail_copy(e).start()

            @pl.when(meta_ref[3, 0] + e < n_blocks)
            def _():
                unused_copy(e).start()
        for e in range(N_EXPERTS):
            @pl.when(meta_ref[2, e] > meta_ref[1, e])
            def _():
                tail_copy(e).wait()

            @pl.when(meta_ref[3, 0] + e < n_blocks)
            def _():
                unused_copy(e).wait()

    def issue(r, carry):
        src = _token(h_ref, r)
        for k in range(TOP_K):
            pltpu.make_async_copy(src, _token(xs_ref, dest_ref[k, r]), sem_r).start(priority=k % 2)
        return carry

    lax.fori_loop(0, td, issue, 0)

    sh_ref[...] = _swiglu_packed(h_ref, w1_ref, w2_ref).astype(sh_ref.dtype)

    for k in range(TOP_K):
        pltpu.make_async_copy(h_ref, xs_ref.at[pl.ds(0, td * SUBLANES)], sem_r).wait()


def _dispatch_call(meta, dest, h2p, w_sh_in_bf, w_sh_out_bf, n_rows):
    t = h2p.shape[0] // SUBLANES
    d = 2 * SUBLANES * LANES
    td = min(256, t)
    return pl.pallas_call(
        functools.partial(_dispatch_kernel, td=td),
        grid=(t // td,),
        in_specs=[pl.BlockSpec(memory_space=pltpu.SMEM),
                  pl.BlockSpec((TOP_K, td), lambda i: (0, i), memory_space=pltpu.SMEM),
                  pl.BlockSpec((td * SUBLANES, LANES), lambda i: (i, 0)),
                  pl.BlockSpec(w_sh_in_bf.shape, lambda i: (0, 0)),
                  pl.BlockSpec(w_sh_out_bf.shape, lambda i: (0, 0))],
        out_specs=[pl.BlockSpec(memory_space=pl.ANY),
                   pl.BlockSpec((td, d), lambda i: (i, 0))],
        out_shape=[jax.ShapeDtypeStruct((n_rows * SUBLANES, LANES), U32),
                   jax.ShapeDtypeStruct((t, d), BF16)],
        scratch_shapes=[pltpu.VMEM((MOE_BLOCK * SUBLANES, LANES), U32),
                        pltpu.SemaphoreType.DMA, pltpu.SemaphoreType.DMA],
        compiler_params=_params(("arbitrary",)),
        name="dispatch_shared",
    )(meta, dest, h2p, w_sh_in_bf, w_sh_out_bf)


def _expert_kernel(meta_ref, xs_ref, w1_hbm, w2_hbm, ys_ref, w1f, w2f, w1b_ref, w2b_ref, xbuf, ybuf,
                   sem_w, sem_in, sem_out, *, n_blocks):
    e = pl.program_id(0)
    n_exp = pl.num_programs(0)
    blk_rows = MOE_BLOCK * SUBLANES
    wslot = lax.rem(e, 2)
    low = 1

    def weights(ex, s):
        return (pltpu.make_async_copy(w1_hbm.at[ex], w1f.at[s], sem_w.at[0, s]),
                pltpu.make_async_copy(w2_hbm.at[ex], w2f.at[s], sem_w.at[1, s]))

    @pl.when(e == 0)
    def _():
        for cp in weights(0, 0):
            cp.start()

    for cp in weights(e, wslot):
        cp.wait()

    @pl.when(e + 1 < n_exp)
    def _():
        for cp in weights(e + 1, 1 - wslot):
            cp.start(priority=low)
    first = meta_ref[0, e] // MOE_BLOCK
    nblk = meta_ref[2, e] // MOE_BLOCK
    nfull = nblk // 2

    def rows_of(b):
        return pl.ds(pl.multiple_of((first + b) * blk_rows, blk_rows), blk_rows)

    def pair_sem(s):
        return (s // 2) * 2 if isinstance(s, int) else lax.div(s, 2) * 2

    def fetch(b, s, base=None):
        src = rows_of(b) if base is None else pl.ds(
            pl.multiple_of((base + b) * blk_rows, blk_rows), blk_rows)
        return pltpu.make_async_copy(xs_ref.at[src], xbuf.at[s], sem_in.at[pair_sem(s)])

    def store(b, s):
        return pltpu.make_async_copy(ybuf.at[s], ys_ref.at[rows_of(b)],
                                     sem_out.at[pair_sem(s)])

    def compute(s):
        out = _swiglu_packed(xbuf.at[s], w1b_ref, w2b_ref)
        half = out.shape[1] // 2
        _store_rows(ybuf.at[s], _pack2(out[:, :half], out[:, half:]))

    odd = lax.rem(nblk, 2) == 1
    tail_slot = 4
    row_queue = 0

    @pl.when(nblk > 0)
    def _():
        @pl.when(jnp.logical_and(nfull > 0, e == 0))
        def _():
            fetch(0, 0).start(priority=row_queue)
            fetch(1, 1).start(priority=row_queue)

        @pl.when(odd)
        def _():
            fetch(nblk - 1, tail_slot).start(priority=row_queue)

        w1b_ref[...] = w1f[wslot].astype(BF16)
        w2b_ref[...] = w2f[wslot].astype(BF16)

    def drain(n_blocks_of):
        nf = n_blocks_of // 2
        for back in (2, 1):
            @pl.when(nf >= back)
            def _():
                s0 = 2 * lax.rem(nf - back, 2)
                store(0, s0).wait()
                store(0, s0 + 1).wait()

        @pl.when(lax.rem(n_blocks_of, 2) == 1)
        def _():
            store(0, tail_slot).wait()

    @pl.when(e > 0)
    def _():
        drain(meta_ref[2, jnp.maximum(e - 1, 0)] // MOE_BLOCK)

    @pl.when(nblk > 0)
    def _():
        def pair(p, carry):
            s0 = 2 * lax.rem(p, 2)
            n0 = 2 - s0
            fetch(2 * p, s0).wait()
            fetch(2 * p + 1, s0 + 1).wait()

            @pl.when(p + 1 < nfull)
            def _():
                fetch(2 * p + 2, n0).start(priority=row_queue)
                fetch(2 * p + 3, n0 + 1).start(priority=row_queue)

            @pl.when(p >= 2)
            def _():
                store(2 * p - 4, s0).wait()
                store(2 * p - 3, s0 + 1).wait()

            compute(s0)
            compute(s0 + 1)
            store(2 * p, s0).start(priority=row_queue)
            store(2 * p + 1, s0 + 1).start(priority=row_queue)
            return carry

        lax.fori_loop(0, nfull, pair, 0)

        @pl.when(odd)
        def _():
            fetch(nblk - 1, tail_slot).wait()
            compute(tail_slot)
            store(nblk - 1, tail_slot).start(priority=row_queue)

    @pl.when(e + 1 < n_exp)
    def _():
        nxt = jnp.minimum(e + 1, n_exp - 1)
        nxt_first = meta_ref[0, nxt] // MOE_BLOCK

        @pl.when(meta_ref[2, nxt] // MOE_BLOCK >= 2)
        def _():
            fetch(0, 0, base=nxt_first).start(priority=row_queue)
            fetch(1, 1, base=nxt_first).start(priority=row_queue)

    @pl.when(e == pl.num_programs(0) - 1)
    def _():
        drain(nblk)
        _zero_rows(ybuf.at[0])

        def unused(j):
            start = pl.multiple_of((meta_ref[3, 0] + j) * blk_rows, blk_rows)
            return pltpu.make_async_copy(ybuf.at[0], ys_ref.at[pl.ds(start, blk_rows)],
                                         sem_out.at[0])

        for j in range(N_EXPERTS):
            @pl.when(meta_ref[3, 0] + j < n_blocks)
            def _():
                unused(j).start()
        for j in range(N_EXPERTS):
            @pl.when(meta_ref[3, 0] + j < n_blocks)
            def _():
                unused(j).wait()


def _expert_call(meta, xs, w_exp_in, w_exp_out):
    blk_rows = MOE_BLOCK * SUBLANES
    n_exp, d, h2 = w_exp_in.shape
    hid = w_exp_out.shape[1]
    n_blocks = xs.shape[0] // blk_rows
    return pl.pallas_call(
        functools.partial(_expert_kernel, n_blocks=n_blocks),
        grid_spec=pltpu.PrefetchScalarGridSpec(
            num_scalar_prefetch=1, grid=(n_exp,),
            in_specs=[pl.BlockSpec(memory_space=pl.ANY),
                      pl.BlockSpec(memory_space=pl.ANY),
                      pl.BlockSpec(memory_space=pl.ANY)],
            out_specs=pl.BlockSpec(memory_space=pl.ANY),
            scratch_shapes=[pltpu.VMEM((2, d, h2), F32), pltpu.VMEM((2, hid, d), F32),
                            pltpu.VMEM((d, h2), BF16), pltpu.VMEM((hid, d), BF16),
                            pltpu.VMEM((5, blk_rows, LANES), U32),
                            pltpu.VMEM((5, blk_rows, LANES), U32),
                            pltpu.SemaphoreType.DMA((2, 2)),
                            pltpu.SemaphoreType.DMA((5,)), pltpu.SemaphoreType.DMA((5,))]),
        out_shape=jax.ShapeDtypeStruct(xs.shape, U32),
        compiler_params=_params(("arbitrary",)),
        name="experts",
    )(meta, xs, w_exp_in, w_exp_out)


def _combine_kernel(dest_ref, dnext_ref, w_ref, ys_ref, sh_ref, x1_ref, mod_ref, g_ref, o_ref,
                    buf_ref, sem, *, tc, final_norm):
    i = pl.program_id(0)
    slot = lax.rem(i, 2)

    def gather(d_ref, s):
        def issue(r, carry):
            for k in range(TOP_K):
                pltpu.make_async_copy(_token(ys_ref, d_ref[k, r]), _token(buf_ref.at[s, k], r),
                                      sem.at[s]).start(priority=k % 2)
            return carry

        lax.fori_loop(0, tc, issue, 0)

    def wait_slot(s):
        for k in range(TOP_K):
            pltpu.make_async_copy(ys_ref.at[pl.ds(0, tc * SUBLANES)], buf_ref.at[s, k],
                                  sem.at[s]).wait()

    @pl.when(i == 0)
    def _():
        gather(dest_ref, 0)

    wait_slot(slot)

    for r in range(tc):
        for k in range(TOP_K):
            pltpu.make_async_copy(
                _token(ys_ref, dnext_ref[k, r]),
                buf_ref.at[1 - slot, k, pl.ds(r * SUBLANES, SUBLANES)],
                sem.at[1 - slot]).start(priority=k % 2)

    acc_lo = acc_hi = None
    for k in range(TOP_K):
        wk = w_ref[:, k:k + 1]
        lo, hi = _unpack2(_load_rows(buf_ref.at[slot, k]))
        acc_lo = wk * lo if acc_lo is None else acc_lo + wk * lo
        acc_hi = wk * hi if acc_hi is None else acc_hi + wk * hi
    routed = jnp.concatenate([acc_lo, acc_hi], axis=1)
    moe = routed + sh_ref[...].astype(F32)
    x2 = x1_ref[...] + mod_ref[5:6, :] * moe
    o_ref[...] = _rms(x2) * g_ref[...] if final_norm else x2

    @pl.when(i == pl.num_programs(0) - 1)
    def _():
        wait_slot(1 - slot)


def _combine_call(dest, wsel_t, ys, shared, x1, mod3, g_final, seq, final_norm):
    t, d = x1.shape
    tc = min(256, seq)
    per_b = seq // tc
    n_tiles = t // tc
    return pl.pallas_call(
        functools.partial(_combine_kernel, tc=tc, final_norm=final_norm),
        grid=(n_tiles,),
        in_specs=[pl.BlockSpec((TOP_K, tc), lambda i: (0, i), memory_space=pltpu.SMEM),
                  pl.BlockSpec((TOP_K, tc), lambda i: (0, jnp.minimum(i + 1, n_tiles - 1)),
                               memory_space=pltpu.SMEM),
                  pl.BlockSpec((tc, TOP_K), lambda i: (i, 0)),
                  pl.BlockSpec(memory_space=pl.ANY),
                  pl.BlockSpec((tc, d), lambda i: (i, 0)),
                  pl.BlockSpec((tc, d), lambda i: (i, 0)),
                  pl.BlockSpec((None, 6, d), lambda i: (i // per_b, 0, 0)),
                  pl.BlockSpec((1, d), lambda i: (0, 0))],
        out_specs=pl.BlockSpec((tc, d), lambda i: (i, 0)),
        out_shape=jax.ShapeDtypeStruct((t, d), F32),
        scratch_shapes=[pltpu.VMEM((2, TOP_K, tc * SUBLANES, LANES), U32),
                        pltpu.SemaphoreType.DMA((2,))],
        compiler_params=_params(("arbitrary",)),
        name="combine_final",
    )(dest, dest, wsel_t, ys, shared, x1, mod3, g_final.reshape(1, d))


def _moe_plan(counts):
    pcounts = (counts + MOE_BLOCK - 1) // MOE_BLOCK * MOE_BLOCK
    pends = jnp.cumsum(pcounts)
    pstarts = pends - pcounts
    return pstarts.astype(I32), pcounts.astype(I32), pends.astype(I32)


def kernel(x, c, positions, w_ada, b_ada, g_mix, w_in, w_conv, b_conv, w_rg_a, b_rg_a, w_rg_x, b_rg_x, lam, beta_ret, beta_lru, w_out, g_moe, w_router, b_router, w_exp_in, w_exp_out, w_sh_in, w_sh_out, g_final):
    batch, seq, d = x.shape
    t = batch * seq
    depth = w_ada.shape[0]
    xcur = x.reshape(t, d)
    cos, sin = _rope_call(positions)
    n_rows = (t * TOP_K + MOE_BLOCK - 1) // MOE_BLOCK * MOE_BLOCK + N_EXPERTS * MOE_BLOCK
    for l in range(depth):
        mod3 = _mod_call(c, w_ada[l], b_ada[l]).reshape(batch, 6, d)
        proj = _in_call(xcur, mod3, g_mix[l], w_in[l].astype(BF16), seq)
        ret = _ret_call(proj, cos, sin, beta_ret[l], batch, seq)
        lru = _lru_call(proj, w_conv[l], b_conv[l], w_rg_a[l], b_rg_a[l], w_rg_x[l], b_rg_x[l],
                        lam[l], beta_lru[l], batch, seq)
        x1, h2p = _out_call(ret, lru, xcur, mod3, g_moe[l], w_out[l].astype(BF16), seq)

        eidx, wsel, pos, cnt = _route_call(h2p, w_router[l], b_router[l])
        counts = cnt[:, 0].astype(I32)
        pstarts, pcounts, pends = _moe_plan(counts)
        dest = _dest_call(pstarts, eidx, pos)
        n_used = (pends[-1] // MOE_BLOCK).astype(I32)
        meta = jnp.stack([pstarts, counts, pcounts, jnp.full_like(counts, n_used)])

        xs, shared = _dispatch_call(meta, dest, h2p, w_sh_in[l].astype(BF16),
                                    w_sh_out[l].astype(BF16), n_rows)
        ys = _expert_call(meta, xs, w_exp_in[l], w_exp_out[l])
        xcur = _combine_call(dest, wsel.T, ys, shared, x1, mod3, g_final, seq,
                             final_norm=(l == depth - 1))
    return xcur.reshape(batch, seq, d)
```

```python
import functools

import numpy as np
import jax
import jax.numpy as jnp
from jax import lax
from jax.experimental import pallas as pl
from jax.experimental.pallas import tpu as pltpu

F32 = jnp.float32
BF16 = jnp.bfloat16
I32 = jnp.int32
U32 = jnp.uint32

RET_HEADS = 8
HEAD_DIM = 128
CHUNK = 128
ROPE_THETA = 10000.0
LRU_BLOCKS = 8
CONV_WIDTH = 4
LRU_C = 8.0
N_EXPERTS = 64
N_GROUPS = 8
GROUP_SIZE = N_EXPERTS // N_GROUPS
TOPK_GROUPS = 4
TOP_K = 8
ROUTED_SCALE = 2.5
EPS = 1e-6

LANES = 128
SUBLANES = 8
VMEM_LIMIT_BYTES = 56 * 1024 * 1024

MOE_BLOCK = 256
NEG_INF = float("-inf")


def _params(semantics, vmem=VMEM_LIMIT_BYTES):
    return pltpu.CompilerParams(dimension_semantics=semantics, vmem_limit_bytes=vmem)


def _silu(x):
    return x * jax.nn.sigmoid(x)


def _rms(x):
    return x * lax.rsqrt(jnp.mean(x * x, axis=-1, keepdims=True) + EPS)


def _pack2(lo, hi):
    return pltpu.pack_elementwise([lo, hi], packed_dtype=jnp.bfloat16)


def _unpack2(w):
    lo = pltpu.unpack_elementwise(w, index=0, packed_dtype=jnp.bfloat16, unpacked_dtype=F32)
    hi = pltpu.unpack_elementwise(w, index=1, packed_dtype=jnp.bfloat16, unpacked_dtype=F32)
    return lo, hi


def _load_rows(ref):
    rows = ref.shape[0] // SUBLANES
    return jnp.concatenate([ref[pl.ds(j, rows, stride=SUBLANES), :] for j in range(SUBLANES)],
                           axis=1)


def _store_rows(ref, v):
    rows = ref.shape[0] // SUBLANES
    for j in range(SUBLANES):
        ref[pl.ds(j, rows, stride=SUBLANES), :] = v[:, j * LANES:(j + 1) * LANES]


def _zero_rows(ref):
    rows = ref.shape[0] // SUBLANES
    z = jnp.zeros((rows, LANES), F32)
    w = _pack2(z, z)
    for j in range(SUBLANES):
        ref[pl.ds(j, rows, stride=SUBLANES), :] = w


def _token(ref, r):
    return ref.at[pl.ds(pl.multiple_of(r * SUBLANES, SUBLANES), SUBLANES)]


def _mod_kernel(c_ref, w_ref, b_ref, o_ref):
    cs = _silu(c_ref[...])
    o_ref[...] = jnp.dot(cs.astype(BF16), w_ref[...].astype(BF16),
                         preferred_element_type=F32) + b_ref[...]


def _mod_call(c, w_ada, b_ada):
    b, d = c.shape
    n = w_ada.shape[1]
    tn = 1024
    return pl.pallas_call(
        _mod_kernel,
        grid=(n // tn,),
        in_specs=[pl.BlockSpec((b, d), lambda j: (0, 0)),
                  pl.BlockSpec((d, tn), lambda j: (0, j)),
                  pl.BlockSpec((1, tn), lambda j: (0, j))],
        out_specs=pl.BlockSpec((b, tn), lambda j: (0, j)),
        out_shape=jax.ShapeDtypeStruct((b, n), F32),
        compiler_params=_params(("parallel",)),
        name="adaln_mod",
    )(c, w_ada, b_ada.reshape(1, n))


def _in_kernel(x_ref, mod_ref, g_ref, w_ref, o_ref, h_ref):
    j = pl.program_id(1)
    tm = x_ref.shape[0]
    chunk = min(256, tm)

    @pl.when(j == 0)
    def _():
        for r0 in range(0, tm, chunk):
            rows = slice(r0, r0 + chunk)
            y = _rms(x_ref[rows, :]) * g_ref[...]
            h = (y * (1.0 + mod_ref[1:2, :]) + mod_ref[0:1, :]).astype(BF16)
            h_ref[rows, :] = h
            o_ref[rows, :] = jnp.dot(h, w_ref[...],
                                     preferred_element_type=F32).astype(o_ref.dtype)

    @pl.when(j > 0)
    def _():
        o_ref[...] = jnp.dot(h_ref[...], w_ref[...],
                             preferred_element_type=F32).astype(o_ref.dtype)


def _in_call(x2, mod3, g_mix, w_in_bf, seq):
    t, d = x2.shape
    n_out = w_in_bf.shape[1] // 1024
    tm = min(1024, seq)
    per_b = seq // tm
    return pl.pallas_call(
        _in_kernel,
        grid=(t // tm, n_out),
        in_specs=[pl.BlockSpec((tm, d), lambda i, j: (i, 0)),
                  pl.BlockSpec((None, 6, d), lambda i, j: (i // per_b, 0, 0)),
                  pl.BlockSpec((1, d), lambda i, j: (0, 0)),
                  pl.BlockSpec((d, 1024), lambda i, j: (0, j))],
        out_specs=pl.BlockSpec((None, tm, 1024), lambda i, j: (j, i, 0)),
        out_shape=jax.ShapeDtypeStruct((n_out, t, 1024), BF16),
        scratch_shapes=[pltpu.VMEM((tm, d), BF16)],
        compiler_params=_params(("parallel", "arbitrary")),
        name="in_proj",
    )(x2, mod3, g_mix.reshape(1, d), w_in_bf)


def _rope_kernel(pos_ref, invf_ref, sign_ref, cos_ref, sin_ref):
    ang = pos_ref[...].astype(F32) * invf_ref[...]
    cos_ref[...] = jnp.cos(ang)
    sin_ref[...] = jnp.sin(ang) * sign_ref[...]


def _rope_call(positions):
    t = positions.size
    tp = min(2048, t)
    half = HEAD_DIM // 2
    inv_freq = 1.0 / (ROPE_THETA ** (jnp.arange(0, HEAD_DIM, 2, dtype=F32) / HEAD_DIM))
    invf = jnp.concatenate([inv_freq, inv_freq]).reshape(1, HEAD_DIM)
    sign = jnp.concatenate([-jnp.ones((half,), F32), jnp.ones((half,), F32)]).reshape(1, HEAD_DIM)
    return pl.pallas_call(
        _rope_kernel,
        grid=(t // tp,),
        in_specs=[pl.BlockSpec((tp, 1), lambda i: (i, 0)),
                  pl.BlockSpec((1, HEAD_DIM), lambda i: (0, 0)),
                  pl.BlockSpec((1, HEAD_DIM), lambda i: (0, 0))],
        out_specs=[pl.BlockSpec((tp, HEAD_DIM), lambda i: (i, 0)),
                   pl.BlockSpec((tp, HEAD_DIM), lambda i: (i, 0))],
        out_shape=[jax.ShapeDtypeStruct((t, HEAD_DIM), F32)] * 2,
        compiler_params=_params(("parallel",)),
        name="rope_tables",
    )(positions.reshape(t, 1), invf, sign)


def _ret_tables():
    h = np.arange(RET_HEADS, dtype=np.float64)
    lg = np.log1p(-np.exp2(-5.0 - h))
    idx = np.arange(CHUNK, dtype=np.float64)
    diff = idx[:, None] - idx[None, :]
    intra = np.where(diff >= 0, np.exp(lg[:, None, None] * np.maximum(diff, 0.0)), 0.0)
    kdec = np.exp(lg[:, None] * (CHUNK - 1.0 - idx)[None, :])
    qdec = np.exp(lg[:, None] * (idx + 1.0)[None, :])
    cdec = np.exp(lg * CHUNK)
    bc = lambda a: np.ascontiguousarray(np.broadcast_to(a[:, :, None], (RET_HEADS, CHUNK, HEAD_DIM)))
    return (intra.astype(np.float32), bc(qdec).astype(np.float32), bc(kdec).astype(np.float32),
            [float(v) for v in cdec])


def _ret_kernel(q_ref, k_ref, v_ref, g_ref, cos_ref, sin_ref, intra_ref, qdec_ref, kdec_ref,
                beta_ref, o_ref, st_ref, *, cdec):
    @pl.when(pl.program_id(1) == 0)
    def _():
        st_ref[...] = jnp.zeros_like(st_ref)

    scale = HEAD_DIM ** -0.5
    nt = (((1,), (1,)), ((), ()))
    tn = (((0,), (0,)), ((), ()))
    n_chunks = q_ref.shape[0] // CHUNK
    outs = [[] for _ in range(n_chunks)]
    for h in range(RET_HEADS):
        sl = slice(h * HEAD_DIM, (h + 1) * HEAD_DIM)
        st = st_ref[h]
        for c in range(n_chunks):
            rows = slice(c * CHUNK, (c + 1) * CHUNK)
            cos = cos_ref[rows, :]
            sin = sin_ref[rows, :]
            q = q_ref[rows, sl].astype(F32)
            k = k_ref[rows, sl].astype(F32)
            v = v_ref[rows, sl]
            qr = (q * cos + pltpu.roll(q, HEAD_DIM // 2, 1) * sin) * scale
            kr = k * cos + pltpu.roll(k, HEAD_DIM // 2, 1) * sin
            s = lax.dot_general(qr.astype(BF16), kr.astype(BF16), nt,
                                preferred_element_type=F32) * intra_ref[h]
            inner = jnp.dot(s.astype(BF16), v, preferred_element_type=F32)
            cross = jnp.dot((qr * qdec_ref[h]).astype(BF16), st.astype(BF16),
                            preferred_element_type=F32)
            o = inner + cross
            kv = lax.dot_general((kr * kdec_ref[h]).astype(BF16), v, tn,
                                 preferred_element_type=F32)
            st = cdec[h] * st + kv
            oc = o - jnp.mean(o, axis=-1, keepdims=True)
            on = oc * lax.rsqrt(jnp.mean(oc * oc, axis=-1, keepdims=True) + EPS)
            outs[c].append(_silu(g_ref[rows, sl].astype(F32)) * on)
        st_ref[h] = st
    for c in range(n_chunks):
        ret = jnp.concatenate(outs[c], axis=1)
        o_ref[c * CHUNK:(c + 1) * CHUNK, :] = (_rms(ret) * beta_ref[...]).astype(o_ref.dtype)


def _ret_call(proj, cos, sin, beta_ret, batch, seq):
    t = batch * seq
    rw = RET_HEADS * HEAD_DIM
    tr = min(4 * CHUNK, seq)
    nc = seq // tr
    intra, qdec, kdec, cdec = _ret_tables()
    row = lambda which: pl.BlockSpec((None, tr, rw), lambda b, n: (which, b * nc + n, 0))
    tab = lambda: pl.BlockSpec((RET_HEADS, CHUNK, HEAD_DIM), lambda b, n: (0, 0, 0))
    cs = lambda: pl.BlockSpec((tr, HEAD_DIM), lambda b, n: (b * nc + n, 0))
    return pl.pallas_call(
        functools.partial(_ret_kernel, cdec=cdec),
        grid=(batch, nc),
        in_specs=[row(0), row(1), row(2), row(3), cs(), cs(), tab(), tab(), tab(),
                  pl.BlockSpec((1, rw), lambda b, n: (0, 0))],
        out_specs=pl.BlockSpec((tr, rw), lambda b, n: (b * nc + n, 0)),
        out_shape=jax.ShapeDtypeStruct((t, rw), BF16),
        scratch_shapes=[pltpu.VMEM((RET_HEADS, HEAD_DIM, HEAD_DIM), F32)],
        compiler_params=_params(("parallel", "arbitrary")),
        name="retention",
    )(proj, proj, proj, proj, cos, sin, jnp.asarray(intra), jnp.asarray(qdec), jnp.asarray(kdec),
      beta_ret.reshape(1, rw))


def _lru_kernel(xr_ref, yr_ref, wc_ref, bc_ref, wg_ref, ba_ref, bx_ref, lam_ref, beta_ref,
                o_ref, xtail_ref, h_ref, *, ts):
    @pl.when(pl.program_id(1) == 0)
    def _():
        xtail_ref[...] = jnp.zeros_like(xtail_ref)
        h_ref[...] = jnp.zeros_like(h_ref)

    x = xr_ref[...].astype(F32)
    w = x.shape[1]
    rows = lax.broadcasted_iota(I32, (ts, w), 0)
    head_rows = lax.broadcasted_iota(I32, (SUBLANES, w), 0)
    tail = xtail_ref[...]
    xc = bc_ref[...]
    for tap in range(CONV_WIDTH):
        back = CONV_WIDTH - 1 - tap
        if back == 0:
            xs = x
        else:
            rolled = pltpu.roll(x, back, 0)
            head = jnp.where(head_rows < back, pltpu.roll(tail, back, 0), rolled[:SUBLANES])
            xs = jnp.concatenate([head, rolled[SUBLANES:]], axis=0)
        xc = xc + xs * wc_ref[tap:tap + 1, :]
    xtail_ref[...] = x[ts - SUBLANES:ts, :]

    xcb = xc.astype(BF16)
    bd = w // LRU_BLOCKS
    rs, gs = [], []
    for g in range(LRU_BLOCKS):
        z = jnp.dot(xcb[:, g * bd:(g + 1) * bd], wg_ref[g], preferred_element_type=F32)
        rs.append(z[:, :bd])
        gs.append(z[:, bd:])
    r = jax.nn.sigmoid(jnp.concatenate(rs, axis=1) + ba_ref[...])
    i = jax.nn.sigmoid(jnp.concatenate(gs, axis=1) + bx_ref[...])
    log_a = -LRU_C * r * jax.nn.softplus(-lam_ref[...])
    a = jnp.exp(log_a)
    bt = jnp.sqrt(-jnp.tanh(log_a) * (a * a + 1.0)) * (i * xc)

    acc_a, acc_b = a, bt
    sh = 1
    while sh < ts:
        if sh < SUBLANES:
            m = rows >= sh
            acc_b = jnp.where(m, acc_a * pltpu.roll(acc_b, sh, 0) + acc_b, acc_b)
            acc_a = jnp.where(m, acc_a * pltpu.roll(acc_a, sh, 0), acc_a)
        else:
            new_b = acc_a[sh:] * acc_b[:ts - sh] + acc_b[sh:]
            acc_a = jnp.concatenate([acc_a[:sh], acc_a[sh:] * acc_a[:ts - sh]], axis=0)
            acc_b = jnp.concatenate([acc_b[:sh], new_b], axis=0)
        sh *= 2
    hs = acc_a * h_ref[...] + acc_b
    h_ref[...] = hs[ts - 1:ts, :]

    lru = hs * jax.nn.gelu(yr_ref[...].astype(F32), approximate=True)
    o_ref[...] = (_rms(lru) * beta_ref[...]).astype(o_ref.dtype)


def _lru_call(proj, w_conv, b_conv, w_rg_a, b_rg_a, w_rg_x, b_rg_x, lam, beta_lru, batch, seq):
    t = batch * seq
    w = w_conv.shape[1]
    ts = min(256, seq)
    nt = seq // ts
    wg = jnp.concatenate([w_rg_a, w_rg_x], axis=-1).astype(BF16)
    vec = lambda: pl.BlockSpec((1, w), lambda b, j: (0, 0))
    row = lambda which: pl.BlockSpec((None, ts, w), lambda b, j: (which, b * nt + j, 0))
    return pl.pallas_call(
        functools.partial(_lru_kernel, ts=ts),
        grid=(batch, nt),
        in_specs=[row(4), row(5),
                  pl.BlockSpec((CONV_WIDTH, w), lambda b, j: (0, 0)), vec(),
                  pl.BlockSpec(wg.shape, lambda b, j: (0, 0, 0)),
                  vec(), vec(), vec(), vec()],
        out_specs=pl.BlockSpec((ts, w), lambda b, j: (b * nt + j, 0)),
        out_shape=jax.ShapeDtypeStruct((t, w), BF16),
        scratch_shapes=[pltpu.VMEM((SUBLANES, w), F32), pltpu.VMEM((1, w), F32)],
        compiler_params=_params(("parallel", "arbitrary")),
        name="rg_lru",
    )(proj, proj, w_conv, b_conv.reshape(1, w), wg, b_rg_a.reshape(1, w), b_rg_x.reshape(1, w),
      lam.reshape(1, w), beta_lru.reshape(1, w))


def _out_kernel(ret_ref, lru_ref, x_ref, mod_ref, g_ref, w_ref, x1_ref, h2_ref):
    rw = ret_ref.shape[1]
    mix = jnp.dot(ret_ref[...], w_ref[0:rw, :], preferred_element_type=F32)
    mix = mix + jnp.dot(lru_ref[...], w_ref[rw:, :], preferred_element_type=F32)
    x1 = x_ref[...] + mod_ref[2:3, :] * mix
    x1_ref[...] = x1
    h2 = _rms(x1) * g_ref[...] * (1.0 + mod_ref[4:5, :]) + mod_ref[3:4, :]
    half = h2.shape[1] // 2
    _store_rows(h2_ref, _pack2(h2[:, :half], h2[:, half:]))


def _out_call(ret, lru, x2, mod3, g_moe, w_out_bf, seq):
    t, d = x2.shape
    assert d // 2 == SUBLANES * LANES, "packed token rows are exactly one (8, 128) word tile"
    rw, lw = ret.shape[1], lru.shape[1]
    tm = min(512, seq)
    per_b = seq // tm
    return pl.pallas_call(
        _out_kernel,
        grid=(t // tm,),
        in_specs=[pl.BlockSpec((tm, rw), lambda i: (i, 0)),
                  pl.BlockSpec((tm, lw), lambda i: (i, 0)),
                  pl.BlockSpec((tm, d), lambda i: (i, 0)),
                  pl.BlockSpec((None, 6, d), lambda i: (i // per_b, 0, 0)),
                  pl.BlockSpec((1, d), lambda i: (0, 0)),
                  pl.BlockSpec((rw + lw, d), lambda i: (0, 0))],
        out_specs=[pl.BlockSpec((tm, d), lambda i: (i, 0)),
                   pl.BlockSpec((tm * SUBLANES, LANES), lambda i: (i, 0))],
        out_shape=[jax.ShapeDtypeStruct((t, d), F32),
                   jax.ShapeDtypeStruct((t * SUBLANES, LANES), U32)],
        compiler_params=_params(("parallel",)),
        name="out_proj",
    )(ret, lru, x2, mod3, g_moe.reshape(1, d), w_out_bf)


def _route_kernel(h_ref, wlo_ref, whi_ref, bias_ref, tri_ref, eidx_ref, wsel_ref, pos_ref,
                  cnt_ref, carry_ref):
    @pl.when(pl.program_id(0) == 0)
    def _():
        carry_ref[...] = jnp.zeros_like(carry_ref)

    tr = h_ref.shape[0] // SUBLANES
    lo, hi = _unpack2(_load_rows(h_ref))
    nt = (((1,), (1,)), ((), ()))
    logits = lax.dot_general(wlo_ref[...], lo.astype(BF16), nt, preferred_element_type=F32)
    logits = logits + lax.dot_general(whi_ref[...], hi.astype(BF16), nt,
                                      preferred_element_type=F32)
    scores = jax.nn.sigmoid(logits)
    biased = scores + bias_ref[:, 0:1]
    shape3 = (N_GROUPS, GROUP_SIZE, tr)
    s3 = scores.reshape(shape3)
    b3 = biased.reshape(shape3)
    member = lax.broadcasted_iota(I32, shape3, 1)
    group = lax.broadcasted_iota(I32, shape3, 0)
    expert = group * GROUP_SIZE + member

    m1 = jnp.max(b3, axis=1, keepdims=True)
    i1 = jnp.min(jnp.where(b3 == m1, member, GROUP_SIZE), axis=1, keepdims=True)
    m2 = jnp.max(jnp.where(member == i1, NEG_INF, b3), axis=1, keepdims=True)
    gscore = m1 + m2

    gid = lax.broadcasted_iota(I32, (N_GROUPS, 1, tr), 0)
    rank = jnp.zeros((N_GROUPS, 1, tr), I32)
    for j in range(N_GROUPS):
        gj = gscore[j:j + 1]
        ahead = (gj > gscore) | ((gj == gscore) & (gid > j))
        rank = rank + ahead.astype(I32)
    masked = jnp.where(rank < TOPK_GROUPS, b3, NEG_INF)

    idxs, vals = [], []
    picked = jnp.zeros(shape3, F32)
    for _ in range(TOP_K):
        m = jnp.max(jnp.max(masked, axis=0, keepdims=True), axis=1, keepdims=True)
        cand = jnp.where(masked == m, expert, N_EXPERTS)
        idx = jnp.min(jnp.min(cand, axis=0, keepdims=True), axis=1, keepdims=True)
        hit = expert == idx
        val = jnp.sum(jnp.sum(jnp.where(hit, s3, 0.0), axis=0, keepdims=True), axis=1, keepdims=True)
        masked = jnp.where(hit, NEG_INF, masked)
        picked = jnp.where(hit, 1.0, picked)
        idxs.append(idx)
        vals.append(val)
    total = vals[0]
    for v in vals[1:]:
        total = total + v

    before = jnp.dot(picked.reshape(N_EXPERTS, tr).astype(BF16), tri_ref[...],
                     preferred_element_type=F32)
    carry = carry_ref[...]
    posm = (before + carry[:, 0:1]).reshape(shape3)
    carry = carry + jnp.sum(picked.reshape(N_EXPERTS, tr), axis=1, keepdims=True)
    carry_ref[...] = carry
    cnt_ref[...] = carry

    for k in range(TOP_K):
        hit = expert == idxs[k]
        p = jnp.sum(jnp.sum(jnp.where(hit, posm, 0.0), axis=0, keepdims=True), axis=1, keepdims=True)
        eidx_ref[k:k + 1, :] = idxs[k].reshape(1, tr)
        wsel_ref[k:k + 1, :] = (vals[k] / total * ROUTED_SCALE).reshape(1, tr)
        pos_ref[k:k + 1, :] = p.reshape(1, tr).astype(I32)


def _route_call(h2p, w_router, b_router):
    t = h2p.shape[0] // SUBLANES
    half = SUBLANES * LANES
    tr = min(512, t)
    wt = w_router.T.astype(BF16)
    bias = jnp.broadcast_to(b_router.astype(F32)[:, None], (N_EXPERTS, LANES))
    tri = jnp.asarray(np.triu(np.ones((tr, tr), np.float32), k=1), dtype=BF16)
    kt = lambda dt: jax.ShapeDtypeStruct((TOP_K, t), dt)
    krow = lambda: pl.BlockSpec((TOP_K, tr), lambda i: (0, i))
    return pl.pallas_call(
        _route_kernel,
        grid=(t // tr,),
        in_specs=[pl.BlockSpec((tr * SUBLANES, LANES), lambda i: (i, 0)),
                  pl.BlockSpec((N_EXPERTS, half), lambda i: (0, 0)),
                  pl.BlockSpec((N_EXPERTS, half), lambda i: (0, 0)),
                  pl.BlockSpec((N_EXPERTS, LANES), lambda i: (0, 0)),
                  pl.BlockSpec((tr, tr), lambda i: (0, 0))],
        out_specs=[krow(), krow(), krow(),
                   pl.BlockSpec((N_EXPERTS, LANES), lambda i: (0, 0))],
        out_shape=[kt(I32), kt(F32), kt(I32), jax.ShapeDtypeStruct((N_EXPERTS, LANES), F32)],
        scratch_shapes=[pltpu.VMEM((N_EXPERTS, LANES), F32)],
        compiler_params=_params(("arbitrary",)),
        name="router",
    )(h2p, wt[:, :half], wt[:, half:], bias, tri)


def _dest_kernel(pst_ref, eidx_ref, pos_ref, o_ref):
    e = eidx_ref[...]
    acc = pos_ref[...]
    for j in range(N_EXPERTS):
        acc = acc + jnp.where(e == j, pst_ref[j], 0)
    o_ref[...] = acc


def _dest_call(pstarts, eidx, pos):
    k, t = eidx.shape
    tb = min(2048, t)
    blk = lambda: pl.BlockSpec((k, tb), lambda i, pst: (0, i))
    return pl.pallas_call(
        _dest_kernel,
        grid_spec=pltpu.PrefetchScalarGridSpec(
            num_scalar_prefetch=1, grid=(t // tb,),
            in_specs=[blk(), blk()], out_specs=blk()),
        out_shape=jax.ShapeDtypeStruct((k, t), I32),
        compiler_params=_params(("parallel",)),
        name="dest_rows",
    )(pstarts, eidx, pos)


def _swiglu_packed(x_ref, w1_ref, w2_ref):
    xw = _load_rows(x_ref)
    half = xw.shape[1]
    lo, hi = _unpack2(xw)
    gu = jnp.dot(lo.astype(BF16), w1_ref[0:half, :], preferred_element_type=F32)
    gu = gu + jnp.dot(hi.astype(BF16), w1_ref[half:, :], preferred_element_type=F32)
    hid = gu.shape[1] // 2
    act = _silu(gu[:, :hid]) * gu[:, hid:]
    return jnp.dot(act.astype(BF16), w2_ref[...], preferred_element_type=F32)


def _dispatch_kernel(meta_ref, dest_ref, h_ref, w1_ref, w2_ref, xs_ref, sh_ref, zero_ref,
                     sem_z, sem_r, *, td):

    @pl.when(pl.program_id(0) == 0)
    def _():
        _zero_rows(zero_ref)

        blk_rows = MOE_BLOCK * SUBLANES

        def block_copy(first_token):
            start = pl.multiple_of(first_token * SUBLANES, blk_rows)
            return pltpu.make_async_copy(zero_ref, xs_ref.at[pl.ds(start, blk_rows)], sem_z)

        def tail_copy(e):
            return block_copy(meta_ref[0, e] + meta_ref[2, e] - MOE_BLOCK)

        def unused_copy(j):
            return block_copy((meta_ref[3, 0] + j) * MOE_BLOCK)

        n_blocks = xs_ref.shape[0] // blk_rows
        for e in range(N_EXPERTS):
            @pl.when(meta_ref[2, e] > meta_ref[1, e])
            def _():
                tail_copy(e).start()

            @pl.when(meta_ref[3, 0] + e < n_blocks)
            def _():
                unused_copy(e).start()
        for e in range(N_EXPERTS):
            @pl.when(meta_ref[2, e] > meta_ref[1, e])
            def _():
                tail_copy(e).wait()

            @pl.when(meta_ref[3, 0] + e < n_blocks)
            def _():
                unused_copy(e).wait()

    for r in range(td):
        src = h_ref.at[pl.ds(r * SUBLANES, SUBLANES)]
        for k in range(TOP_K):
            pltpu.make_async_copy(src, _token(xs_ref, dest_ref[k, r]), sem_r).start(priority=k % 2)

    sh_ref[...] = _swiglu_packed(h_ref, w1_ref, w2_ref).astype(sh_ref.dtype)

    for k in range(TOP_K):
        pltpu.make_async_copy(h_ref, xs_ref.at[pl.ds(0, td * SUBLANES)], sem_r).wait()


def _dispatch_call(meta, dest, h2p, w_sh_in_bf, w_sh_out_bf, n_rows):
    t = h2p.shape[0] // SUBLANES
    d = 2 * SUBLANES * LANES
    td = min(256, t)
    return pl.pallas_call(
        functools.partial(_dispatch_kernel, td=td),
        grid=(t // td,),
        in_specs=[pl.BlockSpec(memory_space=pltpu.SMEM),
                  pl.BlockSpec((TOP_K, td), lambda i: (0, i), memory_space=pltpu.SMEM),
                  pl.BlockSpec((td * SUBLANES, LANES), lambda i: (i, 0)),
                  pl.BlockSpec(w_sh_in_bf.shape, lambda i: (0, 0)),
                  pl.BlockSpec(w_sh_out_bf.shape, lambda i: (0, 0))],
        out_specs=[pl.BlockSpec(memory_space=pl.ANY),
                   pl.BlockSpec((td, d), lambda i: (i, 0))],
        out_shape=[jax.ShapeDtypeStruct((n_rows * SUBLANES, LANES), U32),
                   jax.ShapeDtypeStruct((t, d), BF16)],
        scratch_shapes=[pltpu.VMEM((MOE_BLOCK * SUBLANES, LANES), U32),
                        pltpu.SemaphoreType.DMA, pltpu.SemaphoreType.DMA],
        compiler_params=_params(("arbitrary",)),
        name="dispatch_shared",
    )(meta, dest, h2p, w_sh_in_bf, w_sh_out_bf)


def _expert_kernel(meta_ref, xs_ref, w1_hbm, w2_hbm, ys_ref, w1f, w2f, w1b_ref, w2b_ref, xbuf, ybuf,
                   sem_w, sem_in, sem_out, *, n_blocks):
    e = pl.program_id(0)
    n_exp = pl.num_programs(0)
    blk_rows = MOE_BLOCK * SUBLANES
    wslot = lax.rem(e, 2)
    low = 1

    def weights(ex, s):
        return (pltpu.make_async_copy(w1_hbm.at[ex], w1f.at[s], sem_w.at[0, s]),
                pltpu.make_async_copy(w2_hbm.at[ex], w2f.at[s], sem_w.at[1, s]))

    @pl.when(e == 0)
    def _():
        for cp in weights(0, 0):
            cp.start()

    for cp in weights(e, wslot):
        cp.wait()

    @pl.when(e + 1 < n_exp)
    def _():
        for cp in weights(e + 1, 1 - wslot):
            cp.start(priority=low)
    first = meta_ref[0, e] // MOE_BLOCK
    nblk = meta_ref[2, e] // MOE_BLOCK
    nfull = nblk // 2

    def rows_of(b):
        return pl.ds(pl.multiple_of((first + b) * blk_rows, blk_rows), blk_rows)

    def pair_sem(s):
        return (s // 2) * 2 if isinstance(s, int) else lax.div(s, 2) * 2

    def fetch(b, s, base=None):
        src = rows_of(b) if base is None else pl.ds(
            pl.multiple_of((base + b) * blk_rows, blk_rows), blk_rows)
        return pltpu.make_async_copy(xs_ref.at[src], xbuf.at[s], sem_in.at[pair_sem(s)])

    def store(b, s):
        return pltpu.make_async_copy(ybuf.at[s], ys_ref.at[rows_of(b)],
                                     sem_out.at[pair_sem(s)])

    def compute(s):
        out = _swiglu_packed(xbuf.at[s], w1b_ref, w2b_ref)
        half = out.shape[1] // 2
        _store_rows(ybuf.at[s], _pack2(out[:, :half], out[:, half:]))

    odd = lax.rem(nblk, 2) == 1
    tail_slot = 4
    row_queue = 0

    @pl.when(nblk > 0)
    def _():
        @pl.when(jnp.logical_and(nfull > 0, e == 0))
        def _():
            fetch(0, 0).start(priority=row_queue)
            fetch(1, 1).start(priority=row_queue)

        @pl.when(odd)
        def _():
            fetch(nblk - 1, tail_slot).start(priority=row_queue)

        w1b_ref[...] = w1f[wslot].astype(BF16)
        w2b_ref[...] = w2f[wslot].astype(BF16)

    def drain(n_blocks_of):
        nf = n_blocks_of // 2
        for back in (2, 1):
            @pl.when(nf >= back)
            def _():
                s0 = 2 * lax.rem(nf - back, 2)
                store(0, s0).wait()
                store(0, s0 + 1).wait()

        @pl.when(lax.rem(n_blocks_of, 2) == 1)
        def _():
            store(0, tail_slot).wait()

    @pl.when(e > 0)
    def _():
        drain(meta_ref[2, jnp.maximum(e - 1, 0)] // MOE_BLOCK)

    @pl.when(nblk > 0)
    def _():
        def pair(p, carry):
            s0 = 2 * lax.rem(p, 2)
            n0 = 2 - s0
            fetch(2 * p, s0).wait()
            fetch(2 * p + 1, s0 + 1).wait()

            @pl.when(p + 1 < nfull)
            def _():
                fetch(2 * p + 2, n0).start(priority=row_queue)
                fetch(2 * p + 3, n0 + 1).start(priority=row_queue)

            @pl.when(p >= 2)
            def _():
                store(2 * p - 4, s0).wait()
                store(2 * p - 3, s0 + 1).wait()

            compute(s0)
            compute(s0 + 1)
            store(2 * p, s0).start(priority=row_queue)
            store(2 * p + 1, s0 + 1).start(priority=row_queue)
            return carry

        lax.fori_loop(0, nfull, pair, 0)

        @pl.when(odd)
        def _():
            fetch(nblk - 1, tail_slot).wait()
            compute(tail_slot)
            store(nblk - 1, tail_slot).start(priority=row_queue)

    @pl.when(e + 1 < n_exp)
    def _():
        nxt = jnp.minimum(e + 1, n_exp - 1)
        nxt_first = meta_ref[0, nxt] // MOE_BLOCK

        @pl.when(meta_ref[2, nxt] // MOE_BLOCK >= 2)
        def _():
            fetch(0, 0, base=nxt_first).start(priority=row_queue)
            fetch(1, 1, base=nxt_first).start(priority=row_queue)

    @pl.when(e == pl.num_programs(0) - 1)
    def _():
        drain(nblk)
        _zero_rows(ybuf.at[0])

        def unused(j):
            start = pl.multiple_of((meta_ref[3, 0] + j) * blk_rows, blk_rows)
            return pltpu.make_async_copy(ybuf.at[0], ys_ref.at[pl.ds(start, blk_rows)],
                                         sem_out.at[0])

        for j in range(N_EXPERTS):
            @pl.when(meta_ref[3, 0] + j < n_blocks)
            def _():
                unused(j).start()
        for j in range(N_EXPERTS):
            @pl.when(meta_ref[3, 0] + j < n_blocks)
            def _():
                unused(j).wait()


def _expert_call(meta, xs, w_exp_in, w_exp_out):
    blk_rows = MOE_BLOCK * SUBLANES
    n_exp, d, h2 = w_exp_in.shape
    hid = w_exp_out.shape[1]
    n_blocks = xs.shape[0] // blk_rows
    return pl.pallas_call(
        functools.partial(_expert_kernel, n_blocks=n_blocks),
        grid_spec=pltpu.PrefetchScalarGridSpec(
            num_scalar_prefetch=1, grid=(n_exp,),
            in_specs=[pl.BlockSpec(memory_space=pl.ANY),
                      pl.BlockSpec(memory_space=pl.ANY),
                      pl.BlockSpec(memory_space=pl.ANY)],
            out_specs=pl.BlockSpec(memory_space=pl.ANY),
            scratch_shapes=[pltpu.VMEM((2, d, h2), F32), pltpu.VMEM((2, hid, d), F32),
                            pltpu.VMEM((d, h2), BF16), pltpu.VMEM((hid, d), BF16),
                            pltpu.VMEM((5, blk_rows, LANES), U32),
                            pltpu.VMEM((5, blk_rows, LANES), U32),
                            pltpu.SemaphoreType.DMA((2, 2)),
                            pltpu.SemaphoreType.DMA((5,)), pltpu.SemaphoreType.DMA((5,))]),
        out_shape=jax.ShapeDtypeStruct(xs.shape, U32),
        compiler_params=_params(("arbitrary",)),
        name="experts",
    )(meta, xs, w_exp_in, w_exp_out)


def _combine_kernel(dest_ref, dnext_ref, w_ref, ys_ref, sh_ref, x1_ref, mod_ref, g_ref, o_ref,
                    buf_ref, sem, *, tc, final_norm):
    i = pl.program_id(0)
    slot = lax.rem(i, 2)

    def gather(d_ref, s):
        def issue(r, carry):
            for k in range(TOP_K):
                pltpu.make_async_copy(_token(ys_ref, d_ref[k, r]), _token(buf_ref.at[s, k], r),
                                      sem.at[s]).start(priority=k % 2)
            return carry

        lax.fori_loop(0, tc, issue, 0)

    def wait_slot(s):
        for k in range(TOP_K):
            pltpu.make_async_copy(ys_ref.at[pl.ds(0, tc * SUBLANES)], buf_ref.at[s, k],
                                  sem.at[s]).wait()

    @pl.when(i == 0)
    def _():
        gather(dest_ref, 0)

    wait_slot(slot)

    for r in range(tc):
        for k in range(TOP_K):
            pltpu.make_async_copy(
                _token(ys_ref, dnext_ref[k, r]),
                buf_ref.at[1 - slot, k, pl.ds(r * SUBLANES, SUBLANES)],
                sem.at[1 - slot]).start(priority=k % 2)

    acc_lo = acc_hi = None
    for k in range(TOP_K):
        wk = w_ref[:, k:k + 1]
        lo, hi = _unpack2(_load_rows(buf_ref.at[slot, k]))
        acc_lo = wk * lo if acc_lo is None else acc_lo + wk * lo
        acc_hi = wk * hi if acc_hi is None else acc_hi + wk * hi
    routed = jnp.concatenate([acc_lo, acc_hi], axis=1)
    moe = routed + sh_ref[...].astype(F32)
    x2 = x1_ref[...] + mod_ref[5:6, :] * moe
    o_ref[...] = _rms(x2) * g_ref[...] if final_norm else x2

    @pl.when(i == pl.num_programs(0) - 1)
    def _():
        wait_slot(1 - slot)


def _combine_call(dest, wsel_t, ys, shared, x1, mod3, g_final, seq, final_norm):
    t, d = x1.shape
    tc = min(256, seq)
    per_b = seq // tc
    n_tiles = t // tc
    return pl.pallas_call(
        functools.partial(_combine_kernel, tc=tc, final_norm=final_norm),
        grid=(n_tiles,),
        in_specs=[pl.BlockSpec((TOP_K, tc), lambda i: (0, i), memory_space=pltpu.SMEM),
                  pl.BlockSpec((TOP_K, tc), lambda i: (0, jnp.minimum(i + 1, n_tiles - 1)),
                               memory_space=pltpu.SMEM),
                  pl.BlockSpec((tc, TOP_K), lambda i: (i, 0)),
                  pl.BlockSpec(memory_space=pl.ANY),
                  pl.BlockSpec((tc, d), lambda i: (i, 0)),
                  pl.BlockSpec((tc, d), lambda i: (i, 0)),
                  pl.BlockSpec((None, 6, d), lambda i: (i // per_b, 0, 0)),
                  pl.BlockSpec((1, d), lambda i: (0, 0))],
        out_specs=pl.BlockSpec((tc, d), lambda i: (i, 0)),
        out_shape=jax.ShapeDtypeStruct((t, d), F32),
        scratch_shapes=[pltpu.VMEM((2, TOP_K, tc * SUBLANES, LANES), U32),
                        pltpu.SemaphoreType.DMA((2,))],
        compiler_params=_params(("arbitrary",)),
        name="combine_final",
    )(dest, dest, wsel_t, ys, shared, x1, mod3, g_final.reshape(1, d))


def _moe_plan(counts):
    pcounts = (counts + MOE_BLOCK - 1) // MOE_BLOCK * MOE_BLOCK
    pends = jnp.cumsum(pcounts)
    pstarts = pends - pcounts
    return pstarts.astype(I32), pcounts.astype(I32), pends.astype(I32)


def kernel(x, c, positions, w_ada, b_ada, g_mix, w_in, w_conv, b_conv, w_rg_a, b_rg_a, w_rg_x, b_rg_x, lam, beta_ret, beta_lru, w_out, g_moe, w_router, b_router, w_exp_in, w_exp_out, w_sh_in, w_sh_out, g_final):
    batch, seq, d = x.shape
    t = batch * seq
    depth = w_ada.shape[0]
    xcur = x.reshape(t, d)
    cos, sin = _rope_call(positions)
    n_rows = (t * TOP_K + MOE_BLOCK - 1) // MOE_BLOCK * MOE_BLOCK + N_EXPERTS * MOE_BLOCK
    for l in range(depth):
        mod3 = _mod_call(c, w_ada[l], b_ada[l]).reshape(batch, 6, d)
        proj = _in_call(xcur, mod3, g_mix[l], w_in[l].astype(BF16), seq)
        ret = _ret_call(proj, cos, sin, beta_ret[l], batch, seq)
        lru = _lru_call(proj, w_conv[l], b_conv[l], w_rg_a[l], b_rg_a[l], w_rg_x[l], b_rg_x[l],
                        lam[l], beta_lru[l], batch, seq)
        x1, h2p = _out_call(ret, lru, xcur, mod3, g_moe[l], w_out[l].astype(BF16), seq)

        eidx, wsel, pos, cnt = _route_call(h2p, w_router[l], b_router[l])
        counts = cnt[:, 0].astype(I32)
        pstarts, pcounts, pends = _moe_plan(counts)
        dest = _dest_call(pstarts, eidx, pos)
        n_used = (pends[-1] // MOE_BLOCK).astype(I32)
        meta = jnp.stack([pstarts, counts, pcounts, jnp.full_like(counts, n_used)])

        xs, shared = _dispatch_call(meta, dest, h2p, w_sh_in[l].astype(BF16),
                                    w_sh_out[l].astype(BF16), n_rows)
        ys = _expert_call(meta, xs, w_exp_in[l], w_exp_out[l])
        xcur = _combine_call(dest, wsel.T, ys, shared, x1, mod3, g_final, seq,
                             final_norm=(l == depth - 1))
    return xcur.reshape(batch, seq, d)
```

```python
import functools

import numpy as np
import jax
import jax.numpy as jnp
from jax import lax
from jax.experimental import pallas as pl
from jax.experimental.pallas import tpu as pltpu

F32 = jnp.float32
BF16 = jnp.bfloat16
I32 = jnp.int32
U32 = jnp.uint32

RET_HEADS = 8
HEAD_DIM = 128
CHUNK = 128
ROPE_THETA = 10000.0
LRU_BLOCKS = 8
CONV_WIDTH = 4
LRU_C = 8.0
N_EXPERTS = 64
N_GROUPS = 8
GROUP_SIZE = N_EXPERTS // N_GROUPS
TOPK_GROUPS = 4
TOP_K = 8
ROUTED_SCALE = 2.5
EPS = 1e-6

LANES = 128
SUBLANES = 8
VMEM_LIMIT_BYTES = 56 * 1024 * 1024

MOE_BLOCK = 256
NEG_INF = float("-inf")


def _params(semantics, vmem=VMEM_LIMIT_BYTES):
    return pltpu.CompilerParams(dimension_semantics=semantics, vmem_limit_bytes=vmem)


def _silu(x):
    return x * jax.nn.sigmoid(x)


def _rms(x):
    return x * lax.rsqrt(jnp.mean(x * x, axis=-1, keepdims=True) + EPS)


def _pack2(lo, hi):
    return pltpu.pack_elementwise([lo, hi], packed_dtype=jnp.bfloat16)


def _unpack2(w):
    lo = pltpu.unpack_elementwise(w, index=0, packed_dtype=jnp.bfloat16, unpacked_dtype=F32)
    hi = pltpu.unpack_elementwise(w, index=1, packed_dtype=jnp.bfloat16, unpacked_dtype=F32)
    return lo, hi


def _load_rows(ref):
    rows = ref.shape[0] // SUBLANES
    return jnp.concatenate([ref[pl.ds(j, rows, stride=SUBLANES), :] for j in range(SUBLANES)],
                           axis=1)


def _store_rows(ref, v):
    rows = ref.shape[0] // SUBLANES
    for j in range(SUBLANES):
        ref[pl.ds(j, rows, stride=SUBLANES), :] = v[:, j * LANES:(j + 1) * LANES]


def _zero_rows(ref):
    rows = ref.shape[0] // SUBLANES
    z = jnp.zeros((rows, LANES), F32)
    w = _pack2(z, z)
    for j in range(SUBLANES):
        ref[pl.ds(j, rows, stride=SUBLANES), :] = w


def _token(ref, r):
    return ref.at[pl.ds(pl.multiple_of(r * SUBLANES, SUBLANES), SUBLANES)]


def _mod_kernel(c_ref, w_ref, b_ref, o_ref):
    cs = _silu(c_ref[...])
    o_ref[...] = jnp.dot(cs.astype(BF16), w_ref[...].astype(BF16),
                         preferred_element_type=F32) + b_ref[...]


def _mod_call(c, w_ada, b_ada):
    b, d = c.shape
    n = w_ada.shape[1]
    tn = 1024
    return pl.pallas_call(
        _mod_kernel,
        grid=(n // tn,),
        in_specs=[pl.BlockSpec((b, d), lambda j: (0, 0)),
                  pl.BlockSpec((d, tn), lambda j: (0, j)),
                  pl.BlockSpec((1, tn), lambda j: (0, j))],
        out_specs=pl.BlockSpec((b, tn), lambda j: (0, j)),
        out_shape=jax.ShapeDtypeStruct((b, n), F32),
        compiler_params=_params(("parallel",)),
        name="adaln_mod",
    )(c, w_ada, b_ada.reshape(1, n))


def _in_kernel(x_ref, mod_ref, g_ref, w_ref, o_ref, h_ref):
    j = pl.program_id(1)
    tm = x_ref.shape[0]
    chunk = min(256, tm)

    @pl.when(j == 0)
    def _():
        for r0 in range(0, tm, chunk):
            rows = slice(r0, r0 + chunk)
            y = _rms(x_ref[rows, :]) * g_ref[...]
            h = (y * (1.0 + mod_ref[1:2, :]) + mod_ref[0:1, :]).astype(BF16)
            h_ref[rows, :] = h
            o_ref[rows, :] = jnp.dot(h, w_ref[...],
                                     preferred_element_type=F32).astype(o_ref.dtype)

    @pl.when(j > 0)
    def _():
        o_ref[...] = jnp.dot(h_ref[...], w_ref[...],
                             preferred_element_type=F32).astype(o_ref.dtype)


def _in_call(x2, mod3, g_mix, w_in_bf, seq):
    t, d = x2.shape
    n_out = w_in_bf.shape[1] // 1024
    tm = min(1024, seq)
    per_b = seq // tm
    return pl.pallas_call(
        _in_kernel,
        grid=(t // tm, n_out),
        in_specs=[pl.BlockSpec((tm, d), lambda i, j: (i, 0)),
                  pl.BlockSpec((None, 6, d), lambda i, j: (i // per_b, 0, 0)),
                  pl.BlockSpec((1, d), lambda i, j: (0, 0)),
                  pl.BlockSpec((d, 1024), lambda i, j: (0, j))],
        out_specs=pl.BlockSpec((None, tm, 1024), lambda i, j: (j, i, 0)),
        out_shape=jax.ShapeDtypeStruct((n_out, t, 1024), BF16),
        scratch_shapes=[pltpu.VMEM((tm, d), BF16)],
        compiler_params=_params(("parallel", "arbitrary")),
        name="in_proj",
    )(x2, mod3, g_mix.reshape(1, d), w_in_bf)


def _rope_kernel(pos_ref, invf_ref, sign_ref, cos_ref, sin_ref):
    ang = pos_ref[...].astype(F32) * invf_ref[...]
    cos_ref[...] = jnp.cos(ang)
    sin_ref[...] = jnp.sin(ang) * sign_ref[...]


def _rope_call(positions):
    t = positions.size
    tp = min(2048, t)
    half = HEAD_DIM // 2
    inv_freq = 1.0 / (ROPE_THETA ** (jnp.arange(0, HEAD_DIM, 2, dtype=F32) / HEAD_DIM))
    invf = jnp.concatenate([inv_freq, inv_freq]).reshape(1, HEAD_DIM)
    sign = jnp.concatenate([-jnp.ones((half,), F32), jnp.ones((half,), F32)]).reshape(1, HEAD_DIM)
    return pl.pallas_call(
        _rope_kernel,
        grid=(t // tp,),
        in_specs=[pl.BlockSpec((tp, 1), lambda i: (i, 0)),
                  pl.BlockSpec((1, HEAD_DIM), lambda i: (0, 0)),
                  pl.BlockSpec((1, HEAD_DIM), lambda i: (0, 0))],
        out_specs=[pl.BlockSpec((tp, HEAD_DIM), lambda i: (i, 0)),
                   pl.BlockSpec((tp, HEAD_DIM), lambda i: (i, 0))],
        out_shape=[jax.ShapeDtypeStruct((t, HEAD_DIM), F32)] * 2,
        compiler_params=_params(("parallel",)),
        name="rope_tables",
    )(positions.reshape(t, 1), invf, sign)


def _ret_tables():
    h = np.arange(RET_HEADS, dtype=np.float64)
    lg = np.log1p(-np.exp2(-5.0 - h))
    idx = np.arange(CHUNK, dtype=np.float64)
    diff = idx[:, None] - idx[None, :]
    intra = np.where(diff >= 0, np.exp(lg[:, None, None] * np.maximum(diff, 0.0)), 0.0)
    kdec = np.exp(lg[:, None] * (CHUNK - 1.0 - idx)[None, :])
    qdec = np.exp(lg[:, None] * (idx + 1.0)[None, :])
    cdec = np.exp(lg * CHUNK)
    bc = lambda a: np.ascontiguousarray(np.broadcast_to(a[:, :, None], (RET_HEADS, CHUNK, HEAD_DIM)))
    return (intra.astype(np.float32), bc(qdec).astype(np.float32), bc(kdec).astype(np.float32),
            [float(v) for v in cdec])


def _ret_kernel(q_ref, k_ref, v_ref, g_ref, cos_ref, sin_ref, intra_ref, qdec_ref, kdec_ref,
                beta_ref, o_ref, st_ref, *, cdec):
    @pl.when(pl.program_id(1) == 0)
    def _():
        st_ref[...] = jnp.zeros_like(st_ref)

    scale = HEAD_DIM ** -0.5
    nt = (((1,), (1,)), ((), ()))
    tn = (((0,), (0,)), ((), ()))
    n_chunks = q_ref.shape[0] // CHUNK
    outs = [[] for _ in range(n_chunks)]
    for h in range(RET_HEADS):
        sl = slice(h * HEAD_DIM, (h + 1) * HEAD_DIM)
        st = st_ref[h]
        for c in range(n_chunks):
            rows = slice(c * CHUNK, (c + 1) * CHUNK)
            cos = cos_ref[rows, :]
            sin = sin_ref[rows, :]
            q = q_ref[rows, sl].astype(F32)
            k = k_ref[rows, sl].astype(F32)
            v = v_ref[rows, sl]
            qr = (q * cos + pltpu.roll(q, HEAD_DIM // 2, 1) * sin) * scale
            kr = k * cos + pltpu.roll(k, HEAD_DIM // 2, 1) * sin
            s = lax.dot_general(qr.astype(BF16), kr.astype(BF16), nt,
                                preferred_element_type=F32) * intra_ref[h]
            inner = jnp.dot(s.astype(BF16), v, preferred_element_type=F32)
            cross = jnp.dot((qr * qdec_ref[h]).astype(BF16), st.astype(BF16),
                            preferred_element_type=F32)
            o = inner + cross
            kv = lax.dot_general((kr * kdec_ref[h]).astype(BF16), v, tn,
                                 preferred_element_type=F32)
            st = cdec[h] * st + kv
            oc = o - jnp.mean(o, axis=-1, keepdims=True)
            on = oc * lax.rsqrt(jnp.mean(oc * oc, axis=-1, keepdims=True) + EPS)
            outs[c].append(_silu(g_ref[rows, sl].astype(F32)) * on)
        st_ref[h] = st
    for c in range(n_chunks):
        ret = jnp.concatenate(outs[c], axis=1)
        o_ref[c * CHUNK:(c + 1) * CHUNK, :] = (_rms(ret) * beta_ref[...]).astype(o_ref.dtype)


def _ret_call(proj, cos, sin, beta_ret, batch, seq):
    t = batch * seq
    rw = RET_HEADS * HEAD_DIM
    tr = min(8 * CHUNK, seq)
    nc = seq // tr
    intra, qdec, kdec, cdec = _ret_tables()
    row = lambda which: pl.BlockSpec((None, tr, rw), lambda b, n: (which, b * nc + n, 0))
    tab = lambda: pl.BlockSpec((RET_HEADS, CHUNK, HEAD_DIM), lambda b, n: (0, 0, 0))
    cs = lambda: pl.BlockSpec((tr, HEAD_DIM), lambda b, n: (b * nc + n, 0))
    return pl.pallas_call(
        functools.partial(_ret_kernel, cdec=cdec),
        grid=(batch, nc),
        in_specs=[row(0), row(1), row(2), row(3), cs(), cs(), tab(), tab(), tab(),
                  pl.BlockSpec((1, rw), lambda b, n: (0, 0))],
        out_specs=pl.BlockSpec((tr, rw), lambda b, n: (b * nc + n, 0)),
        out_shape=jax.ShapeDtypeStruct((t, rw), BF16),
        scratch_shapes=[pltpu.VMEM((RET_HEADS, HEAD_DIM, HEAD_DIM), F32)],
        compiler_params=_params(("parallel", "arbitrary")),
        name="retention",
    )(proj, proj, proj, proj, cos, sin, jnp.asarray(intra), jnp.asarray(qdec), jnp.asarray(kdec),
      beta_ret.reshape(1, rw))


def _lru_kernel(xr_ref, yr_ref, wc_ref, bc_ref, wg_ref, ba_ref, bx_ref, lam_ref, beta_ref,
                o_ref, xtail_ref, h_ref, *, ts):
    @pl.when(pl.program_id(1) == 0)
    def _():
        xtail_ref[...] = jnp.zeros_like(xtail_ref)
        h_ref[...] = jnp.zeros_like(h_ref)

    x = xr_ref[...].astype(F32)
    w = x.shape[1]
    rows = lax.broadcasted_iota(I32, (ts, w), 0)
    head_rows = lax.broadcasted_iota(I32, (SUBLANES, w), 0)
    tail = xtail_ref[...]
    xc = bc_ref[...]
    for tap in range(CONV_WIDTH):
        back = CONV_WIDTH - 1 - tap
        if back == 0:
            xs = x
        else:
            rolled = pltpu.roll(x, back, 0)
            head = jnp.where(head_rows < back, pltpu.roll(tail, back, 0), rolled[:SUBLANES])
            xs = jnp.concatenate([head, rolled[SUBLANES:]], axis=0)
        xc = xc + xs * wc_ref[tap:tap + 1, :]
    xtail_ref[...] = x[ts - SUBLANES:ts, :]

    xcb = xc.astype(BF16)
    bd = w // LRU_BLOCKS
    rs, gs = [], []
    for g in range(LRU_BLOCKS):
        z = jnp.dot(xcb[:, g * bd:(g + 1) * bd], wg_ref[g], preferred_element_type=F32)
        rs.append(z[:, :bd])
        gs.append(z[:, bd:])
    r = jax.nn.sigmoid(jnp.concatenate(rs, axis=1) + ba_ref[...])
    i = jax.nn.sigmoid(jnp.concatenate(gs, axis=1) + bx_ref[...])
    log_a = -LRU_C * r * jax.nn.softplus(-lam_ref[...])
    a = jnp.exp(log_a)
    bt = jnp.sqrt(-jnp.tanh(log_a) * (a * a + 1.0)) * (i * xc)

    acc_a, acc_b = a, bt
    sh = 1
    while sh < ts:
        if sh < SUBLANES:
            m = rows >= sh
            acc_b = jnp.where(m, acc_a * pltpu.roll(acc_b, sh, 0) + acc_b, acc_b)
            acc_a = jnp.where(m, acc_a * pltpu.roll(acc_a, sh, 0), acc_a)
        else:
            new_b = acc_a[sh:] * acc_b[:ts - sh] + acc_b[sh:]
            acc_a = jnp.concatenate([acc_a[:sh], acc_a[sh:] * acc_a[:ts - sh]], axis=0)
            acc_b = jnp.concatenate([acc_b[:sh], new_b], axis=0)
        sh *= 2
    hs = acc_a * h_ref[...] + acc_b
    h_ref[...] = hs[ts - 1:ts, :]

    lru = hs * jax.nn.gelu(yr_ref[...].astype(F32), approximate=True)
    o_ref[...] = (_rms(lru) * beta_ref[...]).astype(o_ref.dtype)


def _lru_call(proj, w_conv, b_conv, w_rg_a, b_rg_a, w_rg_x, b_rg_x, lam, beta_lru, batch, seq):
    t = batch * seq
    w = w_conv.shape[1]
    ts = min(256, seq)
    nt = seq // ts
    wg = jnp.concatenate([w_rg_a, w_rg_x], axis=-1).astype(BF16)
    vec = lambda: pl.BlockSpec((1, w), lambda b, j: (0, 0))
    row = lambda which: pl.BlockSpec((None, ts, w), lambda b, j: (which, b * nt + j, 0))
    return pl.pallas_call(
        functools.partial(_lru_kernel, ts=ts),
        grid=(batch, nt),
        in_specs=[row(4), row(5),
                  pl.BlockSpec((CONV_WIDTH, w), lambda b, j: (0, 0)), vec(),
                  pl.BlockSpec(wg.shape, lambda b, j: (0, 0, 0)),
                  vec(), vec(), vec(), vec()],
        out_specs=pl.BlockSpec((ts, w), lambda b, j: (b * nt + j, 0)),
        out_shape=jax.ShapeDtypeStruct((t, w), BF16),
        scratch_shapes=[pltpu.VMEM((SUBLANES, w), F32), pltpu.VMEM((1, w), F32)],
        compiler_params=_params(("parallel", "arbitrary")),
        name="rg_lru",
    )(proj, proj, w_conv, b_conv.reshape(1, w), wg, b_rg_a.reshape(1, w), b_rg_x.reshape(1, w),
      lam.reshape(1, w), beta_lru.reshape(1, w))


def _out_kernel(ret_ref, lru_ref, x_ref, mod_ref, g_ref, w_ref, x1_ref, h2_ref):
    rw = ret_ref.shape[1]
    mix = jnp.dot(ret_ref[...], w_ref[0:rw, :], preferred_element_type=F32)
    mix = mix + jnp.dot(lru_ref[...], w_ref[rw:, :], preferred_element_type=F32)
    x1 = x_ref[...] + mod_ref[2:3, :] * mix
    x1_ref[...] = x1
    h2 = _rms(x1) * g_ref[...] * (1.0 + mod_ref[4:5, :]) + mod_ref[3:4, :]
    half = h2.shape[1] // 2
    _store_rows(h2_ref, _pack2(h2[:, :half], h2[:, half:]))


def _out_call(ret, lru, x2, mod3, g_moe, w_out_bf, seq):
    t, d = x2.shape
    assert d // 2 == SUBLANES * LANES, "packed token rows are exactly one (8, 128) word tile"
    rw, lw = ret.shape[1], lru.shape[1]
    tm = min(512, seq)
    per_b = seq // tm
    return pl.pallas_call(
        _out_kernel,
        grid=(t // tm,),
        in_specs=[pl.BlockSpec((tm, rw), lambda i: (i, 0)),
                  pl.BlockSpec((tm, lw), lambda i: (i, 0)),
                  pl.BlockSpec((tm, d), lambda i: (i, 0)),
                  pl.BlockSpec((None, 6, d), lambda i: (i // per_b, 0, 0)),
                  pl.BlockSpec((1, d), lambda i: (0, 0)),
                  pl.BlockSpec((rw + lw, d), lambda i: (0, 0))],
        out_specs=[pl.BlockSpec((tm, d), lambda i: (i, 0)),
                   pl.BlockSpec((tm * SUBLANES, LANES), lambda i: (i, 0))],
        out_shape=[jax.ShapeDtypeStruct((t, d), F32),
                   jax.ShapeDtypeStruct((t * SUBLANES, LANES), U32)],
        compiler_params=_params(("parallel",)),
        name="out_proj",
    )(ret, lru, x2, mod3, g_moe.reshape(1, d), w_out_bf)


def _route_kernel(h_ref, wlo_ref, whi_ref, bias_ref, tri_ref, eidx_ref, wsel_ref, pos_ref,
                  cnt_ref, carry_ref):
    @pl.when(pl.program_id(0) == 0)
    def _():
        carry_ref[...] = jnp.zeros_like(carry_ref)

    tr = h_ref.shape[0] // SUBLANES
    lo, hi = _unpack2(_load_rows(h_ref))
    nt = (((1,), (1,)), ((), ()))
    logits = lax.dot_general(wlo_ref[...], lo.astype(BF16), nt, preferred_element_type=F32)
    logits = logits + lax.dot_general(whi_ref[...], hi.astype(BF16), nt,
                                      preferred_element_type=F32)
    scores = jax.nn.sigmoid(logits)
    biased = scores + bias_ref[:, 0:1]
    shape3 = (N_GROUPS, GROUP_SIZE, tr)
    s3 = scores.reshape(shape3)
    b3 = biased.reshape(shape3)
    member = lax.broadcasted_iota(I32, shape3, 1)
    group = lax.broadcasted_iota(I32, shape3, 0)
    expert = group * GROUP_SIZE + member

    m1 = jnp.max(b3, axis=1, keepdims=True)
    i1 = jnp.min(jnp.where(b3 == m1, member, GROUP_SIZE), axis=1, keepdims=True)
    m2 = jnp.max(jnp.where(member == i1, NEG_INF, b3), axis=1, keepdims=True)
    gscore = m1 + m2

    gid = lax.broadcasted_iota(I32, (N_GROUPS, 1, tr), 0)
    rank = jnp.zeros((N_GROUPS, 1, tr), I32)
    for j in range(N_GROUPS):
        gj = gscore[j:j + 1]
        ahead = (gj > gscore) | ((gj == gscore) & (gid > j))
        rank = rank + ahead.astype(I32)
    masked = jnp.where(rank < TOPK_GROUPS, b3, NEG_INF)

    idxs, vals = [], []
    picked = jnp.zeros(shape3, F32)
    for _ in range(TOP_K):
        m = jnp.max(jnp.max(masked, axis=0, keepdims=True), axis=1, keepdims=True)
        cand = jnp.where(masked == m, expert, N_EXPERTS)
        idx = jnp.min(jnp.min(cand, axis=0, keepdims=True), axis=1, keepdims=True)
        hit = expert == idx
        val = jnp.sum(jnp.sum(jnp.where(hit, s3, 0.0), axis=0, keepdims=True), axis=1, keepdims=True)
        masked = jnp.where(hit, NEG_INF, masked)
        picked = jnp.where(hit, 1.0, picked)
        idxs.append(idx)
        vals.append(val)
    total = vals[0]
    for v in vals[1:]:
        total = total + v

    before = jnp.dot(picked.reshape(N_EXPERTS, tr).astype(BF16), tri_ref[...],
                     preferred_element_type=F32)
    carry = carry_ref[...]
    posm = (before + carry[:, 0:1]).reshape(shape3)
    carry = carry + jnp.sum(picked.reshape(N_EXPERTS, tr), axis=1, keepdims=True)
    carry_ref[...] = carry
    cnt_ref[...] = carry

    for k in range(TOP_K):
        hit = expert == idxs[k]
        p = jnp.sum(jnp.sum(jnp.where(hit, posm, 0.0), axis=0, keepdims=True), axis=1, keepdims=True)
        eidx_ref[k:k + 1, :] = idxs[k].reshape(1, tr)
        wsel_ref[k:k + 1, :] = (vals[k] / total * ROUTED_SCALE).reshape(1, tr)
        pos_ref[k:k + 1, :] = p.reshape(1, tr).astype(I32)


def _route_call(h2p, w_router, b_router):
    t = h2p.shape[0] // SUBLANES
    half = SUBLANES * LANES
    tr = min(512, t)
    wt = w_router.T.astype(BF16)
    bias = jnp.broadcast_to(b_router.astype(F32)[:, None], (N_EXPERTS, LANES))
    tri = jnp.asarray(np.triu(np.ones((tr, tr), np.float32), k=1), dtype=BF16)
    kt = lambda dt: jax.ShapeDtypeStruct((TOP_K, t), dt)
    krow = lambda: pl.BlockSpec((TOP_K, tr), lambda i: (0, i))
    return pl.pallas_call(
        _route_kernel,
        grid=(t // tr,),
        in_specs=[pl.BlockSpec((tr * SUBLANES, LANES), lambda i: (i, 0)),
                  pl.BlockSpec((N_EXPERTS, half), lambda i: (0, 0)),
                  pl.BlockSpec((N_EXPERTS, half), lambda i: (0, 0)),
                  pl.BlockSpec((N_EXPERTS, LANES), lambda i: (0, 0)),
                  pl.BlockSpec((tr, tr), lambda i: (0, 0))],
        out_specs=[krow(), krow(), krow(),
                   pl.BlockSpec((N_EXPERTS, LANES), lambda i: (0, 0))],
        out_shape=[kt(I32), kt(F32), kt(I32), jax.ShapeDtypeStruct((N_EXPERTS, LANES), F32)],
        scratch_shapes=[pltpu.VMEM((N_EXPERTS, LANES), F32)],
        compiler_params=_params(("arbitrary",)),
        name="router",
    )(h2p, wt[:, :half], wt[:, half:], bias, tri)


def _dest_kernel(pst_ref, eidx_ref, pos_ref, o_ref):
    e = eidx_ref[...]
    acc = pos_ref[...]
    for j in range(N_EXPERTS):
        acc = acc + jnp.where(e == j, pst_ref[j], 0)
    o_ref[...] = acc


def _dest_call(pstarts, eidx, pos):
    k, t = eidx.shape
    tb = min(2048, t)
    blk = lambda: pl.BlockSpec((k, tb), lambda i, pst: (0, i))
    return pl.pallas_call(
        _dest_kernel,
        grid_spec=pltpu.PrefetchScalarGridSpec(
            num_scalar_prefetch=1, grid=(t // tb,),
            in_specs=[blk(), blk()], out_specs=blk()),
        out_shape=jax.ShapeDtypeStruct((k, t), I32),
        compiler_params=_params(("parallel",)),
        name="dest_rows",
    )(pstarts, eidx, pos)


def _swiglu_packed(x_ref, w1_ref, w2_ref):
    xw = _load_rows(x_ref)
    half = xw.shape[1]
    lo, hi = _unpack2(xw)
    gu = jnp.dot(lo.astype(BF16), w1_ref[0:half, :], preferred_element_type=F32)
    gu = gu + jnp.dot(hi.astype(BF16), w1_ref[half:, :], preferred_element_type=F32)
    hid = gu.shape[1] // 2
    act = _silu(gu[:, :hid]) * gu[:, hid:]
    return jnp.dot(act.astype(BF16), w2_ref[...], preferred_element_type=F32)


def _dispatch_kernel(meta_ref, dest_ref, h_ref, w1_ref, w2_ref, xs_ref, sh_ref, zero_ref,
                     sem_z, sem_r, *, td):

    @pl.when(pl.program_id(0) == 0)
    def _():
        _zero_rows(zero_ref)

        blk_rows = MOE_BLOCK * SUBLANES

        def block_copy(first_token):
            start = pl.multiple_of(first_token * SUBLANES, blk_rows)
            return pltpu.make_async_copy(zero_ref, xs_ref.at[pl.ds(start, blk_rows)], sem_z)

        def tail_copy(e):
            return block_copy(meta_ref[0, e] + meta_ref[2, e] - MOE_BLOCK)

        def unused_copy(j):
            return block_copy((meta_ref[3, 0] + j) * MOE_BLOCK)

        n_blocks = xs_ref.shape[0] // blk_rows
        for e in range(N_EXPERTS):
            @pl.when(meta_ref[2, e] > meta_ref[1, e])
            def _():
                tail_copy(e).start()

            @pl.when(meta_ref[3, 0] + e < n_blocks)
            def _():
                unused_copy(e).start()
        for e in range(N_EXPERTS):
            @pl.when(meta_ref[2, e] > meta_ref[1, e])
            def _():
                tail_copy(e).wait()

            @pl.when(meta_ref[3, 0] + e < n_blocks)
            def _():
                unused_copy(e).wait()

    for r in range(td):
        src = h_ref.at[pl.ds(r * SUBLANES, SUBLANES)]
        for k in range(TOP_K):
            pltpu.make_async_copy(src, _token(xs_ref, dest_ref[k, r]), sem_r).start(priority=k % 2)

    sh_ref[...] = _swiglu_packed(h_ref, w1_ref, w2_ref).astype(sh_ref.dtype)

    for k in range(TOP_K):
        pltpu.make_async_copy(h_ref, xs_ref.at[pl.ds(0, td * SUBLANES)], sem_r).wait()


def _dispatch_call(meta, dest, h2p, w_sh_in_bf, w_sh_out_bf, n_rows):
    t = h2p.shape[0] // SUBLANES
    d = 2 * SUBLANES * LANES
    td = min(256, t)
    return pl.pallas_call(
        functools.partial(_dispatch_kernel, td=td),
        grid=(t // td,),
        in_specs=[pl.BlockSpec(memory_space=pltpu.SMEM),
                  pl.BlockSpec((TOP_K, td), lambda i: (0, i), memory_space=pltpu.SMEM),
                  pl.BlockSpec((td * SUBLANES, LANES), lambda i: (i, 0)),
                  pl.BlockSpec(w_sh_in_bf.shape, lambda i: (0, 0)),
                  pl.BlockSpec(w_sh_out_bf.shape, lambda i: (0, 0))],
        out_specs=[pl.BlockSpec(memory_space=pl.ANY),
                   pl.BlockSpec((td, d), lambda i: (i, 0))],
        out_shape=[jax.ShapeDtypeStruct((n_rows * SUBLANES, LANES), U32),
                   jax.ShapeDtypeStruct((t, d), BF16)],
        scratch_shapes=[pltpu.VMEM((MOE_BLOCK * SUBLANES, LANES), U32),
                        pltpu.SemaphoreType.DMA, pltpu.SemaphoreType.DMA],
        compiler_params=_params(("arbitrary",)),
        name="dispatch_shared",
    )(meta, dest, h2p, w_sh_in_bf, w_sh_out_bf)


def _expert_kernel(meta_ref, xs_ref, w1_hbm, w2_hbm, ys_ref, w1f, w2f, w1b_ref, w2b_ref, xbuf, ybuf,
                   sem_w, sem_in, sem_out, *, n_blocks):
    e = pl.program_id(0)
    n_exp = pl.num_programs(0)
    blk_rows = MOE_BLOCK * SUBLANES
    wslot = lax.rem(e, 2)
    low = 1

    def weights(ex, s):
        return (pltpu.make_async_copy(w1_hbm.at[ex], w1f.at[s], sem_w.at[0, s]),
                pltpu.make_async_copy(w2_hbm.at[ex], w2f.at[s], sem_w.at[1, s]))

    @pl.when(e == 0)
    def _():
        for cp in weights(0, 0):
            cp.start()

    for cp in weights(e, wslot):
        cp.wait()

    @pl.when(e + 1 < n_exp)
    def _():
        for cp in weights(e + 1, 1 - wslot):
            cp.start(priority=low)
    first = meta_ref[0, e] // MOE_BLOCK
    nblk = meta_ref[2, e] // MOE_BLOCK
    nfull = nblk // 2

    def rows_of(b):
        return pl.ds(pl.multiple_of((first + b) * blk_rows, blk_rows), blk_rows)

    def pair_sem(s):
        return (s // 2) * 2 if isinstance(s, int) else lax.div(s, 2) * 2

    def fetch(b, s, base=None):
        src = rows_of(b) if base is None else pl.ds(
            pl.multiple_of((base + b) * blk_rows, blk_rows), blk_rows)
        return pltpu.make_async_copy(xs_ref.at[src], xbuf.at[s], sem_in.at[pair_sem(s)])

    def store(b, s):
        return pltpu.make_async_copy(ybuf.at[s], ys_ref.at[rows_of(b)],
                                     sem_out.at[pair_sem(s)])

    def compute(s):
        out = _swiglu_packed(xbuf.at[s], w1b_ref, w2b_ref)
        half = out.shape[1] // 2
        _store_rows(ybuf.at[s], _pack2(out[:, :half], out[:, half:]))

    odd = lax.rem(nblk, 2) == 1
    tail_slot = 4
    row_queue = 0

    @pl.when(nblk > 0)
    def _():
        @pl.when(jnp.logical_and(nfull > 0, e == 0))
        def _():
            fetch(0, 0).start(priority=row_queue)
            fetch(1, 1).start(priority=row_queue)

        @pl.when(odd)
        def _():
            fetch(nblk - 1, tail_slot).start(priority=row_queue)

        w1b_ref[...] = w1f[wslot].astype(BF16)
        w2b_ref[...] = w2f[wslot].astype(BF16)

    def drain(n_blocks_of):
        nf = n_blocks_of // 2
        for back in (2, 1):
            @pl.when(nf >= back)
            def _():
                s0 = 2 * lax.rem(nf - back, 2)
                store(0, s0).wait()
                store(0, s0 + 1).wait()

        @pl.when(lax.rem(n_blocks_of, 2) == 1)
        def _():
            store(0, tail_slot).wait()

    @pl.when(e > 0)
    def _():
        drain(meta_ref[2, jnp.maximum(e - 1, 0)] // MOE_BLOCK)

    @pl.when(nblk > 0)
    def _():
        def pair(p, carry):
            s0 = 2 * lax.rem(p, 2)
            n0 = 2 - s0
            fetch(2 * p, s0).wait()
            fetch(2 * p + 1, s0 + 1).wait()

            @pl.when(p + 1 < nfull)
            def _():
                fetch(2 * p + 2, n0).start(priority=row_queue)
                fetch(2 * p + 3, n0 + 1).start(priority=row_queue)

            @pl.when(p >= 2)
            def _():
                store(2 * p - 4, s0).wait()
                store(2 * p - 3, s0 + 1).wait()

            compute(s0)
            compute(s0 + 1)
            store(2 * p, s0).start(priority=row_queue)
            store(2 * p + 1, s0 + 1).start(priority=row_queue)
            return carry

        lax.fori_loop(0, nfull, pair, 0)

        @pl.when(odd)
        def _():
            fetch(nblk - 1, tail_slot).wait()
            compute(tail_slot)
            store(nblk - 1, tail_slot).start(priority=row_queue)

    @pl.when(e + 1 < n_exp)
    def _():
        nxt = jnp.minimum(e + 1, n_exp - 1)
        nxt_first = meta_ref[0, nxt] // MOE_BLOCK

        @pl.when(meta_ref[2, nxt] // MOE_BLOCK >= 2)
        def _():
            fetch(0, 0, base=nxt_first).start(priority=row_queue)
            fetch(1, 1, base=nxt_first).start(priority=row_queue)

    @pl.when(e == pl.num_programs(0) - 1)
    def _():
        drain(nblk)
        _zero_rows(ybuf.at[0])

        def unused(j):
            start = pl.multiple_of((meta_ref[3, 0] + j) * blk_rows, blk_rows)
            return pltpu.make_async_copy(ybuf.at[0], ys_ref.at[pl.ds(start, blk_rows)],
                                         sem_out.at[0])

        for j in range(N_EXPERTS):
            @pl.when(meta_ref[3, 0] + j < n_blocks)
            def _():
                unused(j).start()
        for j in range(N_EXPERTS):
            @pl.when(meta_ref[3, 0] + j < n_blocks)
            def _():
                unused(j).wait()


def _expert_call(meta, xs, w_exp_in, w_exp_out):
    blk_rows = MOE_BLOCK * SUBLANES
    n_exp, d, h2 = w_exp_in.shape
    hid = w_exp_out.shape[1]
    n_blocks = xs.shape[0] // blk_rows
    return pl.pallas_call(
        functools.partial(_expert_kernel, n_blocks=n_blocks),
        grid_spec=pltpu.PrefetchScalarGridSpec(
            num_scalar_prefetch=1, grid=(n_exp,),
            in_specs=[pl.BlockSpec(memory_space=pl.ANY),
                      pl.BlockSpec(memory_space=pl.ANY),
                      pl.BlockSpec(memory_space=pl.ANY)],
            out_specs=pl.BlockSpec(memory_space=pl.ANY),
            scratch_shapes=[pltpu.VMEM((2, d, h2), F32), pltpu.VMEM((2, hid, d), F32),
                            pltpu.VMEM((d, h2), BF16), pltpu.VMEM((hid, d), BF16),
                            pltpu.VMEM((5, blk_rows, LANES), U32),
                            pltpu.VMEM((5, blk_rows, LANES), U32),
                            pltpu.SemaphoreType.DMA((2, 2)),
                            pltpu.SemaphoreType.DMA((5,)), pltpu.SemaphoreType.DMA((5,))]),
        out_shape=jax.ShapeDtypeStruct(xs.shape, U32),
        compiler_params=_params(("arbitrary",)),
        name="experts",
    )(meta, xs, w_exp_in, w_exp_out)


def _combine_kernel(dest_ref, dnext_ref, w_ref, ys_ref, sh_ref, x1_ref, mod_ref, g_ref, o_ref,
                    buf_ref, sem, *, tc, final_norm):
    i = pl.program_id(0)
    slot = lax.rem(i, 2)

    def gather(d_ref, s):
        def issue(r, carry):
            for k in range(TOP_K):
                pltpu.make_async_copy(_token(ys_ref, d_ref[k, r]), _token(buf_ref.at[s, k], r),
                                      sem.at[s]).start(priority=k % 2)
            return carry

        lax.fori_loop(0, tc, issue, 0)

    def wait_slot(s):
        for k in range(TOP_K):
            pltpu.make_async_copy(ys_ref.at[pl.ds(0, tc * SUBLANES)], buf_ref.at[s, k],
                                  sem.at[s]).wait()

    @pl.when(i == 0)
    def _():
        gather(dest_ref, 0)

    wait_slot(slot)

    for r in range(tc):
        for k in range(TOP_K):
            pltpu.make_async_copy(
                _token(ys_ref, dnext_ref[k, r]),
                buf_ref.at[1 - slot, k, pl.ds(r * SUBLANES, SUBLANES)],
                sem.at[1 - slot]).start(priority=k % 2)

    acc_lo = acc_hi = None
    for k in range(TOP_K):
        wk = w_ref[:, k:k + 1]
        lo, hi = _unpack2(_load_rows(buf_ref.at[slot, k]))
        acc_lo = wk * lo if acc_lo is None else acc_lo + wk * lo
        acc_hi = wk * hi if acc_hi is None else acc_hi + wk * hi
    routed = jnp.concatenate([acc_lo, acc_hi], axis=1)
    moe = routed + sh_ref[...].astype(F32)
    x2 = x1_ref[...] + mod_ref[5:6, :] * moe
    o_ref[...] = _rms(x2) * g_ref[...] if final_norm else x2

    @pl.when(i == pl.num_programs(0) - 1)
    def _():
        wait_slot(1 - slot)


def _combine_call(dest, wsel_t, ys, shared, x1, mod3, g_final, seq, final_norm):
    t, d = x1.shape
    tc = min(256, seq)
    per_b = seq // tc
    n_tiles = t // tc
    return pl.pallas_call(
        functools.partial(_combine_kernel, tc=tc, final_norm=final_norm),
        grid=(n_tiles,),
        in_specs=[pl.BlockSpec((TOP_K, tc), lambda i: (0, i), memory_space=pltpu.SMEM),
                  pl.BlockSpec((TOP_K, tc), lambda i: (0, jnp.minimum(i + 1, n_tiles - 1)),
                               memory_space=pltpu.SMEM),
                  pl.BlockSpec((tc, TOP_K), lambda i: (i, 0)),
                  pl.BlockSpec(memory_space=pl.ANY),
                  pl.BlockSpec((tc, d), lambda i: (i, 0)),
                  pl.BlockSpec((tc, d), lambda i: (i, 0)),
                  pl.BlockSpec((None, 6, d), lambda i: (i // per_b, 0, 0)),
                  pl.BlockSpec((1, d), lambda i: (0, 0))],
        out_specs=pl.BlockSpec((tc, d), lambda i: (i, 0)),
        out_shape=jax.ShapeDtypeStruct((t, d), F32),
        scratch_shapes=[pltpu.VMEM((2, TOP_K, tc * SUBLANES, LANES), U32),
                        pltpu.SemaphoreType.DMA((2,))],
        compiler_params=_params(("arbitrary",)),
        name="combine_final",
    )(dest, dest, wsel_t, ys, shared, x1, mod3, g_final.reshape(1, d))


def _moe_plan(counts):
    pcounts = (counts + MOE_BLOCK - 1) // MOE_BLOCK * MOE_BLOCK
    pends = jnp.cumsum(pcounts)
    pstarts = pends - pcounts
    return pstarts.astype(I32), pcounts.astype(I32), pends.astype(I32)


def kernel(x, c, positions, w_ada, b_ada, g_mix, w_in, w_conv, b_conv, w_rg_a, b_rg_a, w_rg_x, b_rg_x, lam, beta_ret, beta_lru, w_out, g_moe, w_router, b_router, w_exp_in, w_exp_out, w_sh_in, w_sh_out, g_final):
    batch, seq, d = x.shape
    t = batch * seq
    depth = w_ada.shape[0]
    xcur = x.reshape(t, d)
    cos, sin = _rope_call(positions)
    n_rows = (t * TOP_K + MOE_BLOCK - 1) // MOE_BLOCK * MOE_BLOCK + N_EXPERTS * MOE_BLOCK
    for l in range(depth):
        mod3 = _mod_call(c, w_ada[l], b_ada[l]).reshape(batch, 6, d)
        proj = _in_call(xcur, mod3, g_mix[l], w_in[l].astype(BF16), seq)
        ret = _ret_call(proj, cos, sin, beta_ret[l], batch, seq)
        lru = _lru_call(proj, w_conv[l], b_conv[l], w_rg_a[l], b_rg_a[l], w_rg_x[l], b_rg_x[l],
                        lam[l], beta_lru[l], batch, seq)
        x1, h2p = _out_call(ret, lru, xcur, mod3, g_moe[l], w_out[l].astype(BF16), seq)

        eidx, wsel, pos, cnt = _route_call(h2p, w_router[l], b_router[l])
        counts = cnt[:, 0].astype(I32)
        pstarts, pcounts, pends = _moe_plan(counts)
        dest = _dest_call(pstarts, eidx, pos)
        n_used = (pends[-1] // MOE_BLOCK).astype(I32)
        meta = jnp.stack([pstarts, counts, pcounts, jnp.full_like(counts, n_used)])

        xs, shared = _dispatch_call(meta, dest, h2p, w_sh_in[l].astype(BF16),
                                    w_sh_out[l].astype(BF16), n_rows)
        ys = _expert_call(meta, xs, w_exp_in[l], w_exp_out[l])
        xcur = _combine_call(dest, wsel.T, ys, shared, x1, mod3, g_final, seq,
                             final_norm=(l == depth - 1))
    return xcur.reshape(batch, seq, d)
```
